```python
import math
import jax, jax.numpy as jnp
from jax import lax
import numpy as np

D_MODEL = 1024
BATCH = 2
SEQ = 8192
DEPTH = 2

CHUNK = 64
N_MIXERS = 2
N_HEADS = 16
HEAD_DIM = D_MODEL // N_HEADS
IDX_HEADS = 8
IDX_DIM = 64
TOPK_MAX = 256
Q_BLOCK = 128
ATTN_SPLITS = (D_MODEL, 2 * D_MODEL, 3 * D_MODEL,
               3 * D_MODEL + IDX_HEADS * IDX_DIM,
               3 * D_MODEL + IDX_HEADS * IDX_DIM + IDX_DIM)
ATTN_PROJ = 3 * D_MODEL + IDX_HEADS * IDX_DIM + IDX_DIM + IDX_HEADS
SSM_WIDTH = D_MODEL
SSM_GROUP = 16
SSM_GROUPS = SSM_WIDTH // SSM_GROUP
SSM_STATE = 64
SSM_GROUP_BLOCK = 16
D_FF = 2816
CONV_WIDTH = 3
EPS = 1e-6

kernel_name = 'hybrid_dsa_s5_convffn_adaln'


def rms_norm(x, g):
    x32 = x.astype(jnp.float32)
    y = x32 * lax.rsqrt(jnp.mean(x32 * x32, axis=-1, keepdims=True) + EPS)
    return (y * g.astype(jnp.float32)).astype(x.dtype)


def modulate(h, shift, scale):
    return h * (1.0 + scale[:, None, :]) + shift[:, None, :]


def alibi_slopes(n_heads):
    return 2.0 ** (-8.0 * jnp.arange(1, n_heads + 1, dtype=jnp.float32) / n_heads)


def dsa_attention(h, w_in, w_out):
    bsz, seq, _ = h.shape
    proj = h @ w_in
    q, k, v, qi, ki, wi = jnp.split(proj, ATTN_SPLITS, axis=-1)
    q = q.reshape(bsz, seq, N_HEADS, HEAD_DIM)
    k = k.reshape(bsz, seq, N_HEADS, HEAD_DIM)
    v = v.reshape(bsz, seq, N_HEADS, HEAD_DIM)
    qi = qi.reshape(bsz, seq, IDX_HEADS, IDX_DIM)
    wi = wi * (IDX_HEADS ** -0.5)
    n_keep = min(TOPK_MAX, seq // 4)
    n_blk = seq // Q_BLOCK
    pos = jnp.arange(seq)
    key_chunk = pos // CHUNK
    slopes = alibi_slopes(N_HEADS)

    def to_blocks(a):
        return jnp.moveaxis(a.reshape(bsz, n_blk, Q_BLOCK, *a.shape[2:]), 1, 0)

    def one_block(args):
        qb, qib, wb, tb = args
        logits = jnp.einsum('bqhd,bsd->bqhs', qib, ki) * (IDX_DIM ** -0.5)
        score = jnp.einsum('bqh,bqhs->bqs', wb, jax.nn.relu(logits)).astype(jnp.float32)
        q_chunk = tb // CHUNK
        admissible = key_chunk[None, :] <= q_chunk[:, None]
        score = jnp.where(admissible[None], score, -jnp.inf)
        _, idx = lax.top_k(score, n_keep)
        kg = jax.vmap(lambda kk, ii: kk[ii])(k, idx)
        vg = jax.vmap(lambda vv, ii: vv[ii])(v, idx)
        s_att = jnp.einsum('bqhd,bqkhd->bhqk', qb, kg).astype(jnp.float32) * (HEAD_DIM ** -0.5)
        dist = jnp.abs(tb[None, :, None] - idx).astype(jnp.float32)
        s_att = s_att - slopes[None, :, None, None] * dist[:, None]
        valid = (idx // CHUNK) <= q_chunk[None, :, None]
        s_att = jnp.where(valid[:, None], s_att, -jnp.inf)
        p = jax.nn.softmax(s_att, axis=-1).astype(vg.dtype)
        return jnp.einsum('bhqk,bqkhd->bqhd', p, vg)

    out = lax.map(one_block, (to_blocks(q), to_blocks(qi), to_blocks(wi),
                              pos.reshape(n_blk, Q_BLOCK)))
    out = jnp.moveaxis(out, 0, 1).reshape(bsz, seq, D_MODEL)
    return out @ w_out


def _complex_affine_combine(e1, e2):
    a1r, a1i, b1r, b1i = e1
    a2r, a2i, b2r, b2i = e2
    ar = a2r * a1r - a2i * a1i
    ai = a2r * a1i + a2i * a1r
    br = a2r * b1r - a2i * b1i + b2r
    bi = a2r * b1i + a2i * b1r + b2i
    return (ar, ai, br, bi)


def _s5_group_block(args):
    u, lam_re, lam_im, log_dt, b_re, b_im, c_re, c_im, d_skip = args
    dt = jnp.exp(log_dt)[:, None]
    mag = jnp.exp(lam_re * dt)
    a_re = mag * jnp.cos(lam_im * dt)
    a_im = mag * jnp.sin(lam_im * dt)
    den = lam_re * lam_re + lam_im * lam_im
    coef_re = ((a_re - 1.0) * lam_re + a_im * lam_im) / den
    coef_im = (a_im * lam_re - (a_re - 1.0) * lam_im) / den
    bb_re = coef_re[..., None] * b_re - coef_im[..., None] * b_im
    bb_im = coef_re[..., None] * b_im + coef_im[..., None] * b_re
    bu_re = jnp.einsum('bsgc,gpc->bsgp', u, bb_re)
    bu_im = jnp.einsum('bsgc,gpc->bsgp', u, bb_im)
    a_re_t = jnp.broadcast_to(a_re, bu_re.shape)
    a_im_t = jnp.broadcast_to(a_im, bu_im.shape)
    _, _, h_re, h_im = lax.associative_scan(
        _complex_affine_combine, (a_re_t, a_im_t, bu_re, bu_im), axis=1)
    y = (jnp.einsum('bsgp,gcp->bsgc', h_re, c_re)
         - jnp.einsum('bsgp,gcp->bsgc', h_im, c_im)
         + d_skip * u)
    return y


def s5_layer(h, w_in, lam_re, lam_im, log_dt, b_re, b_im, c_re, c_im, d_skip, w_glu):
    bsz, seq, _ = h.shape
    n_gb = SSM_GROUPS // SSM_GROUP_BLOCK
    u = (h @ w_in).astype(jnp.float32).reshape(bsz, seq, n_gb, SSM_GROUP_BLOCK, SSM_GROUP)
    u = jnp.moveaxis(u, 2, 0)

    def grp(p):
        return p.astype(jnp.float32).reshape(n_gb, SSM_GROUP_BLOCK, *p.shape[1:])

    y = lax.map(_s5_group_block, (u, grp(lam_re), grp(lam_im), grp(log_dt), grp(b_re),
                                   grp(b_im), grp(c_re), grp(c_im), grp(d_skip)))
    y = jnp.moveaxis(y, 0, 2).reshape(bsz, seq, SSM_WIDTH)
    y = jax.nn.gelu(y).astype(h.dtype)
    val, gate = jnp.split(y @ w_glu, 2, axis=-1)
    return val * jax.nn.sigmoid(gate)


def conv_ffn(h, w_up, conv_w, conv_b, w_down):
    z = h @ w_up
    n_ch = z.shape[-1]
    z = lax.conv_general_dilated(
        z, conv_w[:, None, :].astype(z.dtype), window_strides=(1,),
        padding=[(CONV_WIDTH - 1, 0)], dimension_numbers=('NWC', 'WIO', 'NWC'),
        feature_group_count=n_ch) + conv_b
    val, gate = jnp.split(z, 2, axis=-1)
    return (jax.nn.silu(gate) * val) @ w_down


def setup_inputs(seed: int = 0) -> dict:
    key = jax.random.key(seed)
    ks = jax.random.split(key, 24)
    f32 = jnp.float32
    n_attn = (DEPTH + 1) // 2
    n_ssm = DEPTH // 2
    G, P, GS = SSM_GROUPS, SSM_STATE, SSM_GROUP

    def nrm(k, shape, std):
        return std * jax.random.normal(k, shape, f32)

    lam_im_base = math.pi * jnp.arange(P, dtype=f32)
    return {
        'x': nrm(ks[0], (BATCH, SEQ, D_MODEL), 1.0),
        'c': nrm(ks[1], (BATCH, D_MODEL), 1.0),
        'mod_w': nrm(ks[2], (DEPTH, D_MODEL, 6 * D_MODEL), 0.5 * D_MODEL ** -0.5),
        'mod_b': nrm(ks[3], (DEPTH, 6 * D_MODEL), 0.02),
        'norm_mix_g': 1.0 + nrm(ks[4], (DEPTH, D_MODEL), 0.02),
        'norm_ffn_g': 1.0 + nrm(ks[5], (DEPTH, D_MODEL), 0.02),
        'attn_w_in': nrm(ks[6], (n_attn, D_MODEL, ATTN_PROJ), D_MODEL ** -0.5),
        'attn_w_out': nrm(ks[7], (n_attn, D_MODEL, D_MODEL), D_MODEL ** -0.5),
        'ssm_w_in': nrm(ks[8], (n_ssm, D_MODEL, SSM_WIDTH), D_MODEL ** -0.5),
        'ssm_lam_re': -0.5 + nrm(ks[9], (n_ssm, G, P), 0.01),
        'ssm_lam_im': lam_im_base + nrm(ks[10], (n_ssm, G, P), 0.01),
        'ssm_log_dt': jax.random.uniform(ks[11], (n_ssm, G), f32,
                                         minval=math.log(1e-3), maxval=math.log(1e-1)),
        'ssm_b_re': nrm(ks[12], (n_ssm, G, P, GS), (2.0 * GS) ** -0.5),
        'ssm_b_im': nrm(ks[13], (n_ssm, G, P, GS), (2.0 * GS) ** -0.5),
        'ssm_c_re': nrm(ks[14], (n_ssm, G, GS, P), 0.5),
        'ssm_c_im': nrm(ks[15], (n_ssm, G, GS, P), 0.5),
        'ssm_d': nrm(ks[16], (n_ssm, G, GS), 1.0),
        'ssm_w_glu': nrm(ks[17], (n_ssm, SSM_WIDTH, 2 * D_MODEL), SSM_WIDTH ** -0.5),
        'ffn_w_up': nrm(ks[18], (DEPTH, D_MODEL, 2 * D_FF), D_MODEL ** -0.5),
        'ffn_conv_w': nrm(ks[19], (DEPTH, CONV_WIDTH, 2 * D_FF), CONV_WIDTH ** -0.5),
        'ffn_conv_b': nrm(ks[20], (DEPTH, 2 * D_FF), 0.02),
        'ffn_w_down': nrm(ks[21], (DEPTH, D_FF, D_MODEL), D_FF ** -0.5),
        'final_g': 1.0 + nrm(ks[22], (D_MODEL,), 0.02),
    }


def reference(x, c, mod_w, mod_b, norm_mix_g, norm_ffn_g, attn_w_in, attn_w_out,
              ssm_w_in, ssm_lam_re, ssm_lam_im, ssm_log_dt, ssm_b_re, ssm_b_im,
              ssm_c_re, ssm_c_im, ssm_d, ssm_w_glu, ffn_w_up, ffn_conv_w, ffn_conv_b,
              ffn_w_down, final_g):
    cond = jax.nn.silu(c)
    for i in range(DEPTH):
        mod = cond @ mod_w[i] + mod_b[i]
        sh1, sc1, g1, sh2, sc2, g2 = jnp.split(mod, 6, axis=-1)
        h = modulate(rms_norm(x, norm_mix_g[i]), sh1, sc1)
        j = i // N_MIXERS
        if i % N_MIXERS == 0:
            y = dsa_attention(h, attn_w_in[j], attn_w_out[j])
        else:
            y = s5_layer(h, ssm_w_in[j], ssm_lam_re[j], ssm_lam_im[j], ssm_log_dt[j],
                         ssm_b_re[j], ssm_b_im[j], ssm_c_re[j], ssm_c_im[j], ssm_d[j],
                         ssm_w_glu[j])
        x = x + g1[:, None, :] * y
        h = modulate(rms_norm(x, norm_ffn_g[i]), sh2, sc2)
        x = x + g2[:, None, :] * conv_ffn(h, ffn_w_up[i], ffn_conv_w[i], ffn_conv_b[i],
                                          ffn_w_down[i])
    return rms_norm(x, final_g)
```

```python
import functools
import math

import jax
import jax.numpy as jnp
from jax import lax
from jax.experimental import pallas as pl
from jax.experimental.pallas import tpu as pltpu

F32 = jnp.float32
BF16 = jnp.bfloat16
I32 = jnp.int32

EPS = 1e-6
CHUNK = 64
N_HEADS = 16
HEAD_DIM = 64
IDX_HEADS = 8
IDX_DIM = 64
TOPK_MAX = 256
SSM_GROUP = 16
SSM_STATE = 64
CONV_WIDTH = 3

LANES = 128
SUBLANES = 8
MASK_BITS = 32
VMEM_LIMIT = 56 * 1024 * 1024

INT_MIN = -2 ** 31
KEY_NEG_INF = INT_MIN + 0x7FFFFF
MASKED_DIST = 1e30
M_INIT = -3e38

_NT = (((1,), (1,)), ((), ()))


def _cparams(sem):
    return pltpu.CompilerParams(dimension_semantics=sem, vmem_limit_bytes=VMEM_LIMIT)


def _norm_mod(x, g, sh, sc):
    ms = jnp.mean(x * x, axis=-1, keepdims=True)
    y = x * lax.rsqrt(ms + EPS)
    return (y * g) * (1.0 + sc) + sh


def _rms(x, g):
    ms = jnp.mean(x * x, axis=-1, keepdims=True)
    return (x * lax.rsqrt(ms + EPS)) * g


def _mod_kernel(c_ref, w_ref, b_ref, o_ref):
    c = c_ref[...]
    cond = c * (1.0 / (1.0 + jnp.exp(-c)))
    o_ref[...] = jnp.dot(cond, w_ref[...], preferred_element_type=F32,
                         precision=lax.Precision.HIGHEST) + b_ref[...]


def _mod_vectors(c, mod_w, mod_b):
    depth, d, n6 = mod_w.shape
    bsz = c.shape[0]
    rows = SUBLANES
    c_pad = jnp.zeros((rows, d), F32).at[:bsz].set(c)
    out = pl.pallas_call(
        _mod_kernel,
        grid=(depth, n6 // d),
        in_specs=[pl.BlockSpec((rows, d), lambda i, j: (0, 0)),
                  pl.BlockSpec((None, d, d), lambda i, j: (i, 0, j)),
                  pl.BlockSpec((None, 1, d), lambda i, j: (i, 0, j))],
        out_specs=pl.BlockSpec((None, rows, d), lambda i, j: (i, 0, j)),
        out_shape=jax.ShapeDtypeStruct((depth, rows, n6), F32),
        compiler_params=_cparams(("arbitrary", "arbitrary")),
        name="mod_vectors",
    )(c_pad, mod_w, mod_b.reshape(depth, 1, n6))
    return out[:, :bsz, :]


def _attn_proj_kernel(x_ref, g_ref, sh_ref, sc_ref, wqkv_ref, wih_ref, wil_ref,
                      q_ref, k_ref, v_ref, qcat_ref, kcat_ref, wi_ref, *, d):
    h = _norm_mod(x_ref[0], g_ref[...], sh_ref[0], sc_ref[0])
    hb = h.astype(BF16)
    hl = (h - hb.astype(F32)).astype(BF16)
    qkv = jnp.dot(hb, wqkv_ref[...], preferred_element_type=F32)
    q_ref[0] = (qkv[:, :d] * (HEAD_DIM ** -0.5)).astype(BF16)
    k_ref[0] = qkv[:, d:2 * d].astype(BF16)
    v_ref[0] = qkv[:, 2 * d:3 * d].astype(BF16)
    wih = wih_ref[...]
    idx = (jnp.dot(hb, wih, preferred_element_type=F32)
           + jnp.dot(hl, wih, preferred_element_type=F32)
           + jnp.dot(hb, wil_ref[...], preferred_element_type=F32))
    nq = IDX_HEADS * LANES
    qi2 = idx[:, :nq] * (IDX_DIM ** -0.5)
    qhi = qi2.astype(BF16)
    qlo = (qi2 - qhi.astype(F32)).astype(BF16)
    lane = lax.broadcasted_iota(I32, qi2.shape, 1)
    sel = jnp.where((lane & (LANES - 1)) < IDX_DIM, qhi, qlo)
    for hh in range(IDX_HEADS):
        piece = sel[:, hh * LANES:(hh + 1) * LANES]
        qcat_ref[0, :, 2 * hh * LANES:(2 * hh + 1) * LANES] = piece
        qcat_ref[0, :, (2 * hh + 1) * LANES:(2 * hh + 2) * LANES] = piece
    ki2 = idx[:, nq:nq + LANES]
    khi = ki2.astype(BF16)
    klo = (ki2 - khi.astype(F32)).astype(BF16)
    kcat_ref[0, :, :LANES] = khi
    kcat_ref[0, :, LANES:] = klo
    wi_ref[0] = idx[:, nq + LANES:] * (IDX_HEADS ** -0.5)


def _attn_proj(x, g, sh, sc, w_in, *, tm):
    bsz, seq, d = x.shape
    nq = IDX_HEADS * IDX_DIM
    wqkv = w_in[:, :3 * d].astype(BF16)
    wq_i = w_in[:, 3 * d:3 * d + nq].reshape(d, IDX_HEADS, IDX_DIM)
    wq_i = jnp.concatenate([wq_i, wq_i], axis=-1).reshape(d, IDX_HEADS * LANES)
    wk_i = w_in[:, 3 * d + nq:3 * d + nq + IDX_DIM]
    wk_i = jnp.concatenate([wk_i, wk_i], axis=-1)
    ww_i = w_in[:, 3 * d + nq + IDX_DIM:]
    ww_i = jnp.pad(ww_i, ((0, 0), (0, LANES - ww_i.shape[1])))
    w_idx = jnp.concatenate([wq_i, wk_i, ww_i], axis=-1)
    w_hi = w_idx.astype(BF16)
    w_lo = (w_idx - w_hi.astype(F32)).astype(BF16)
    n_idx = w_idx.shape[1]
    row = lambda b, i: (b, i, 0)
    const = lambda b, i: (0, 0)
    per_b = lambda b, i: (b, 0, 0)
    return pl.pallas_call(
        functools.partial(_attn_proj_kernel, d=d),
        grid=(bsz, seq // tm),
        in_specs=[pl.BlockSpec((1, tm, d), row),
                  pl.BlockSpec((1, d), const),
                  pl.BlockSpec((1, 1, d), per_b),
                  pl.BlockSpec((1, 1, d), per_b),
                  pl.BlockSpec((d, 3 * d), const),
                  pl.BlockSpec((d, n_idx), const),
                  pl.BlockSpec((d, n_idx), const)],
        out_specs=[pl.BlockSpec((1, tm, d), row),
                   pl.BlockSpec((1, tm, d), row),
                   pl.BlockSpec((1, tm, d), row),
                   pl.BlockSpec((1, tm, 2 * IDX_HEADS * LANES), row),
                   pl.BlockSpec((1, tm, 2 * LANES), row),
                   pl.BlockSpec((1, tm, LANES), row)],
        out_shape=[jax.ShapeDtypeStruct((bsz, seq, d), BF16),
                   jax.ShapeDtypeStruct((bsz, seq, d), BF16),
                   jax.ShapeDtypeStruct((bsz, seq, d), BF16),
                   jax.ShapeDtypeStruct((bsz, seq, 2 * IDX_HEADS * LANES), BF16),
                   jax.ShapeDtypeStruct((bsz, seq, 2 * LANES), BF16),
                   jax.ShapeDtypeStruct((bsz, seq, LANES), F32)],
        compiler_params=_cparams(("arbitrary", "arbitrary")),
        name="attn_proj",
    )(x, g, sh, sc, wqkv, w_hi, w_lo)


def _topk_mask_kernel(qcat_ref, kcat_ref, wi_ref, bits_ref, keys_ref, pk_ref,
                      *, tq, n_keep, cr, tie_bits):
    qi = pl.program_id(1)
    n_adm = (qi + 1) * tq
    t_chunk = (qi * tq + lax.broadcasted_iota(I32, (1, tq), 1)) // CHUNK
    w_t = wi_ref[0].T

    def score_body(kt, carry):
        r0 = pl.multiple_of(kt * tq, tq)
        kc = kcat_ref[0, pl.ds(r0, tq), :]
        sc = jnp.zeros((tq, tq), F32)
        for hh in range(IDX_HEADS):
            lg = lax.dot_general(kc, qcat_ref[0, :, 2 * hh * LANES:(2 * hh + 2) * LANES],
                                 _NT, preferred_element_type=F32)
            sc = sc + w_t[hh:hh + 1, :] * jnp.maximum(lg, 0.0)
        sc = sc + 0.0
        raw = lax.bitcast_convert_type(sc, I32)
        key = raw ^ ((raw >> 31) & 0x7FFFFFFF)
        j_chunk = (r0 + lax.broadcasted_iota(I32, (tq, 1), 0)) // CHUNK
        keys_ref[pl.ds(r0, tq), :] = jnp.where(j_chunk <= t_chunk, key, KEY_NEG_INF)
        return carry

    lax.fori_loop(0, qi + 1, score_body, 0)

    n_ch = n_adm // cr

    def count(pred):
        def body(c, acc):
            r0 = pl.multiple_of(c * cr, cr)
            return acc + pred(keys_ref[pl.ds(r0, cr), :], r0).astype(I32)
        acc = lax.fori_loop(0, n_ch, body, jnp.zeros((cr, tq), I32))
        return jnp.sum(acc, axis=0, keepdims=True)

    def search_round(i, carry):
        thr, c_thr = carry
        cand = thr + jnp.left_shift(jnp.int32(1), 31 - i)
        cnt = count(lambda blk, r0: blk >= cand)
        ok = cnt >= n_keep
        return jnp.where(ok, cand, thr), jnp.where(ok, cnt, c_thr)

    thr, c_thr = lax.fori_loop(
        0, 32, search_round,
        (jnp.full((1, tq), INT_MIN, I32), jnp.full((1, tq), n_adm, I32)))

    n_tied_ok = n_keep - count(lambda blk, r0: blk > thr)
    need_ties = jnp.max(c_thr) > n_keep

    def tie_round(i, cut):
        cand = cut + jnp.left_shift(jnp.int32(1), tie_bits - 1 - i)
        cnt = count(lambda blk, r0: (blk == thr)
                    & ((r0 + lax.broadcasted_iota(I32, (cr, 1), 0)) < cand))
        return jnp.where(cnt <= n_tied_ok, cand, cut)

    cut = lax.fori_loop(0, jnp.where(need_ties, tie_bits, 0), tie_round,
                        jnp.zeros((1, tq), I32))
    cut = jnp.where(need_ties, cut, jnp.int32(2 ** 30))

    pk_ref[...] = jnp.zeros(pk_ref.shape, I32)

    def pack_body(c, carry):
        r0 = pl.multiple_of(c * LANES, LANES)
        blk = keys_ref[pl.ds(r0, LANES), :]
        j = r0 + lax.broadcasted_iota(I32, (LANES, 1), 0)
        sel = (blk > thr) | ((blk == thr) & (j < cut))
        sel = sel & ((j // CHUNK) <= t_chunk)
        w0 = pl.multiple_of((c // MASK_BITS) * LANES, LANES)
        pk_ref[pl.ds(w0, LANES), :] = pk_ref[pl.ds(w0, LANES), :] | jnp.left_shift(
            sel.astype(I32), c % MASK_BITS)
        return carry

    lax.fori_loop(0, n_adm // LANES, pack_body, 0)
    bits_ref[0] = pk_ref[...].T


def _topk_mask(qcat, kcat, wi, *, tq, n_keep):
    bsz, seq, _ = qcat.shape
    n_words = -(-seq // (MASK_BITS * LANES))
    tie_bits = max(1, int(seq).bit_length())
    return pl.pallas_call(
        functools.partial(_topk_mask_kernel, tq=tq, n_keep=n_keep, cr=32, tie_bits=tie_bits),
        grid=(bsz, seq // tq),
        in_specs=[pl.BlockSpec((1, tq, qcat.shape[2]), lambda b, i: (b, i, 0)),
                  pl.BlockSpec((1, seq, kcat.shape[2]), lambda b, i: (b, 0, 0)),
                  pl.BlockSpec((1, tq, LANES), lambda b, i: (b, i, 0))],
        out_specs=pl.BlockSpec((1, tq, n_words * LANES), lambda b, i: (b, i, 0)),
        out_shape=jax.ShapeDtypeStruct((bsz, seq, n_words * LANES), I32),
        scratch_shapes=[pltpu.VMEM((seq, tq), I32),
                        pltpu.VMEM((n_words * LANES, tq), I32)],
        compiler_params=_cparams(("arbitrary", "arbitrary")),
        name="topk_mask",
    )(qcat, kcat, wi)


def _attn_kernel(q_ref, k_ref, v_ref, bits_ref, o_ref, qm_ref, m_ref, l_ref, acc_ref,
                 *, tq, tk, nk):
    qi = pl.program_id(1)
    kt = pl.program_id(2)
    last = ((qi + 1) * tq - 1) // tk
    n_sub = tk // LANES
    lane = lax.broadcasted_iota(I32, (tq, LANES), 1)
    lo_half = lane < HEAD_DIM

    @pl.when(kt == 0)
    def _init():
        m_ref[...] = jnp.full(m_ref.shape, M_INIT, F32)
        l_ref[...] = jnp.zeros(l_ref.shape, F32)
        acc_ref[...] = jnp.zeros(acc_ref.shape, F32)
        for hp in range(N_HEADS // 2):
            q2 = q_ref[0, :, hp * LANES:(hp + 1) * LANES]
            zero = jnp.zeros_like(q2)
            qm_ref[2 * hp] = jnp.where(lo_half, q2, zero)
            qm_ref[2 * hp + 1] = jnp.where(lo_half, zero, q2)

    @pl.when(kt <= last)
    def _compute():
        t = qi * tq + lax.broadcasted_iota(I32, (tq, 1), 0)
        j = kt * tk + lax.broadcasted_iota(I32, (1, tk), 1)
        dist = jnp.abs(t - j).astype(F32)
        bits = bits_ref[0]
        sel = jnp.concatenate(
            [(bits >> ((kt * n_sub + i) % MASK_BITS)) & 1 for i in range(n_sub)], axis=1)
        dm = jnp.where(sel == 1, dist, MASKED_DIST)
        for hp in range(N_HEADS // 2):
            k2 = k_ref[0, :, hp * LANES:(hp + 1) * LANES]
            v2 = v_ref[0, :, hp * LANES:(hp + 1) * LANES]
            alphas, pvs = [], []
            for e in range(2):
                h = 2 * hp + e
                slope = 2.0 ** (-8.0 * (h + 1) / N_HEADS)
                s = lax.dot_general(qm_ref[h], k2, _NT, preferred_element_type=F32)
                s = s - slope * dm
                m_old = m_ref[h]
                m_new = jnp.maximum(m_old, jnp.max(s, axis=1, keepdims=True))
                p = jnp.exp(s - jnp.tile(m_new, (1, n_sub)))
                alpha = jnp.exp(m_old - m_new)
                l_ref[h] = alpha * l_ref[h] + jnp.sum(p, axis=1, keepdims=True)
                m_ref[h] = m_new
                pvs.append(jnp.dot(p.astype(BF16), v2, preferred_element_type=F32))
                alphas.append(alpha)
            sl = slice(hp * LANES, (hp + 1) * LANES)
            acc_ref[:, sl] = (acc_ref[:, sl] * jnp.where(lo_half, alphas[0], alphas[1])
                              + jnp.where(lo_half, pvs[0], pvs[1]))

    @pl.when(kt == nk - 1)
    def _finish():
        for hp in range(N_HEADS // 2):
            sl = slice(hp * LANES, (hp + 1) * LANES)
            denom = jnp.where(lo_half, l_ref[2 * hp], l_ref[2 * hp + 1])
            o_ref[0, :, sl] = (acc_ref[:, sl] / denom).astype(o_ref.dtype)


def _attention(q, k, v, bits, *, tq, tk):
    bsz, seq, d = q.shape
    nk = seq // tk
    n_sub = tk // LANES

    def last_kt(i):
        return ((i + 1) * tq - 1) // tk

    kv_map = lambda b, i, t: (b, jnp.minimum(t, last_kt(i)), 0)
    bits_map = lambda b, i, t: (b, i, (jnp.minimum(t, last_kt(i)) * n_sub) // MASK_BITS)
    return pl.pallas_call(
        functools.partial(_attn_kernel, tq=tq, tk=tk, nk=nk),
        grid=(bsz, seq // tq, nk),
        in_specs=[pl.BlockSpec((1, tq, d), lambda b, i, t: (b, i, 0)),
                  pl.BlockSpec((1, tk, d), kv_map),
                  pl.BlockSpec((1, tk, d), kv_map),
                  pl.BlockSpec((1, tq, LANES), bits_map)],
        out_specs=pl.BlockSpec((1, tq, d), lambda b, i, t: (b, i, 0)),
        out_shape=jax.ShapeDtypeStruct((bsz, seq, d), BF16),
        scratch_shapes=[pltpu.VMEM((N_HEADS, tq, LANES), BF16),
                        pltpu.VMEM((N_HEADS, tq, LANES), F32),
                        pltpu.VMEM((N_HEADS, tq, LANES), F32),
                        pltpu.VMEM((tq, d), F32)],
        compiler_params=_cparams(("arbitrary", "arbitrary", "arbitrary")),
        name="masked_attention",
    )(q, k, v, bits)


def _proj_residual_kernel(y_ref, w_ref, x_ref, gate_ref, o_ref, *, glu, d):
    r = jnp.dot(y_ref[0], w_ref[...], preferred_element_type=F32)
    if glu:
        val, gt = r[:, :d], r[:, d:]
        r = val * (1.0 / (1.0 + jnp.exp(-gt)))
    o_ref[0] = x_ref[0] + gate_ref[0] * r


def _proj_residual(y, w, x, gate, *, tm, glu, name):
    bsz, seq, d = x.shape
    kdim, n = w.shape
    row = lambda b, i: (b, i, 0)
    return pl.pallas_call(
        functools.partial(_proj_residual_kernel, glu=glu, d=d),
        grid=(bsz, seq // tm),
        in_specs=[pl.BlockSpec((1, tm, kdim), row),
                  pl.BlockSpec((kdim, n), lambda b, i: (0, 0)),
                  pl.BlockSpec((1, tm, d), row),
                  pl.BlockSpec((1, 1, d), lambda b, i: (b, 0, 0))],
        out_specs=pl.BlockSpec((1, tm, d), row),
        out_shape=jax.ShapeDtypeStruct((bsz, seq, d), F32),
        compiler_params=_cparams(("arbitrary", "arbitrary")),
        name=name,
    )(y, w.astype(BF16), x, gate)


def _ffn_up_kernel(x_ref, g_ref, sh_ref, sc_ref, wv_ref, wg_ref, cwv_ref, cwg_ref,
                   cbv_ref, cbg_ref, a_ref, zv_ref, zg_ref, *, tm):
    i = pl.program_id(2)
    halo = SUBLANES

    @pl.when(i == 0)
    def _seq_start():
        zv_ref[0:halo, :] = jnp.zeros((halo, zv_ref.shape[1]), F32)
        zg_ref[0:halo, :] = jnp.zeros((halo, zg_ref.shape[1]), F32)

    h = _norm_mod(x_ref[0], g_ref[...], sh_ref[0], sc_ref[0]).astype(BF16)
    zv_ref[halo:halo + tm, :] = jnp.dot(h, wv_ref[...], preferred_element_type=F32)
    zg_ref[halo:halo + tm, :] = jnp.dot(h, wg_ref[...], preferred_element_type=F32)

    def conv(z_ref, cw_ref, cb_ref):
        out = cb_ref[...]
        for tap in range(CONV_WIDTH):
            off = halo - (CONV_WIDTH - 1) + tap
            out = out + cw_ref[tap:tap + 1, :] * z_ref[off:off + tm, :]
        return out

    val = conv(zv_ref, cwv_ref, cbv_ref)
    gt = conv(zg_ref, cwg_ref, cbg_ref)
    a_ref[0] = (gt * (1.0 / (1.0 + jnp.exp(-gt))) * val).astype(a_ref.dtype)
    zv_ref[0:halo, :] = zv_ref[tm:tm + halo, :]
    zg_ref[0:halo, :] = zg_ref[tm:tm + halo, :]


def _ffn_up(x, g, sh, sc, w_up, conv_w, conv_b, *, tm, tn):
    bsz, seq, d = x.shape
    dff = w_up.shape[1] // 2
    nj = dff // tn
    wv = w_up[:, :dff].astype(BF16)
    wg = w_up[:, dff:].astype(BF16)
    cwv = jnp.pad(conv_w[:, :dff], ((0, SUBLANES - CONV_WIDTH), (0, 0)))
    cwg = jnp.pad(conv_w[:, dff:], ((0, SUBLANES - CONV_WIDTH), (0, 0)))
    cbv = conv_b[:dff].reshape(1, dff)
    cbg = conv_b[dff:].reshape(1, dff)
    col = lambda j, b, i: (0, j)
    return pl.pallas_call(
        functools.partial(_ffn_up_kernel, tm=tm),
        grid=(nj, bsz, seq // tm),
        in_specs=[pl.BlockSpec((1, tm, d), lambda j, b, i: (b, i, 0)),
                  pl.BlockSpec((1, d), lambda j, b, i: (0, 0)),
                  pl.BlockSpec((1, 1, d), lambda j, b, i: (b, 0, 0)),
                  pl.BlockSpec((1, 1, d), lambda j, b, i: (b, 0, 0)),
                  pl.BlockSpec((d, tn), col),
                  pl.BlockSpec((d, tn), col),
                  pl.BlockSpec((SUBLANES, tn), col),
                  pl.BlockSpec((SUBLANES, tn), col),
                  pl.BlockSpec((1, tn), col),
                  pl.BlockSpec((1, tn), col)],
        out_specs=pl.BlockSpec((1, tm, tn), lambda j, b, i: (b, i, j)),
        out_shape=jax.ShapeDtypeStruct((bsz, seq, dff), BF16),
        scratch_shapes=[pltpu.VMEM((tm + SUBLANES, tn), F32),
                        pltpu.VMEM((tm + SUBLANES, tn), F32)],
        compiler_params=_cparams(("arbitrary", "arbitrary", "arbitrary")),
        name="ffn_up",
    )(x, g, sh, sc, wv, wg, cwv, cwg, cbv, cbg)


def _ffn_down_kernel(a_ref, w_ref, x_ref, gate_ref, fg_ref, o_ref, *, final):
    r = jnp.dot(a_ref[0], w_ref[...], preferred_element_type=F32)
    out = x_ref[0] + gate_ref[0] * r
    if final:
        out = _rms(out, fg_ref[...])
    o_ref[0] = out


def _ffn_down(a, w_down, x, gate, final_g, *, tm, final):
    bsz, seq, d = x.shape
    dff = a.shape[2]
    row = lambda b, i: (b, i, 0)
    return pl.pallas_call(
        functools.partial(_ffn_down_kernel, final=final),
        grid=(bsz, seq // tm),
        in_specs=[pl.BlockSpec((1, tm, dff), row),
                  pl.BlockSpec((dff, d), lambda b, i: (0, 0)),
                  pl.BlockSpec((1, tm, d), row),
                  pl.BlockSpec((1, 1, d), lambda b, i: (b, 0, 0)),
                  pl.BlockSpec((1, d), lambda b, i: (0, 0))],
        out_specs=pl.BlockSpec((1, tm, d), row),
        out_shape=jax.ShapeDtypeStruct((bsz, seq, d), F32),
        compiler_params=_cparams(("arbitrary", "arbitrary")),
        name="ffn_down_final" if final else "ffn_down",
    )(a, w_down.astype(BF16), x, gate, final_g)


def _ssm_in_kernel(x_ref, g_ref, sh_ref, sc_ref, w_ref, u_ref):
    h = _norm_mod(x_ref[0], g_ref[...], sh_ref[0], sc_ref[0]).astype(BF16)
    u_ref[0] = jnp.dot(h, w_ref[...], preferred_element_type=F32)


def _ssm_in(x, g, sh, sc, w, *, tm):
    bsz, seq, d = x.shape
    n = w.shape[1]
    row = lambda b, i: (b, i, 0)
    return pl.pallas_call(
        _ssm_in_kernel,
        grid=(bsz, seq // tm),
        in_specs=[pl.BlockSpec((1, tm, d), row),
                  pl.BlockSpec((1, d), lambda b, i: (0, 0)),
                  pl.BlockSpec((1, 1, d), lambda b, i: (b, 0, 0)),
                  pl.BlockSpec((1, 1, d), lambda b, i: (b, 0, 0)),
                  pl.BlockSpec((d, n), lambda b, i: (0, 0))],
        out_specs=pl.BlockSpec((1, tm, n), row),
        out_shape=jax.ShapeDtypeStruct((bsz, seq, n), F32),
        compiler_params=_cparams(("arbitrary", "arbitrary")),
        name="ssm_in",
    )(x, g, sh, sc, w.astype(BF16))


def _ssm_scan_kernel(u_ref, bbt_ref, apow_ref, ppow_ref, cc_ref, dskip_ref, y_ref,
                     carry_ref, *, tt, ns, n_steps):
    i = pl.program_id(2)

    @pl.when(i == 0)
    def _seq_start():
        carry_ref[...] = jnp.zeros(carry_ref.shape, F32)

    u = u_ref[0]
    bu = jnp.dot(u.astype(BF16), bbt_ref[0], preferred_element_type=F32)
    h_re, h_im = bu[:, :ns], bu[:, ns:]
    row = lax.broadcasted_iota(I32, (tt, ns), 0)
    for kk in range(n_steps):
        dsh = 1 << kk
        a_re = apow_ref[0, kk, 0:1, :]
        a_im = apow_ref[0, kk, 1:2, :]
        keep = row >= dsh
        s_re = jnp.where(keep, pltpu.roll(h_re, dsh, axis=0), 0.0)
        s_im = jnp.where(keep, pltpu.roll(h_im, dsh, axis=0), 0.0)
        h_re, h_im = (h_re + a_re * s_re - a_im * s_im,
                      h_im + a_re * s_im + a_im * s_re)
    c_re = carry_ref[0:1, :]
    c_im = carry_ref[1:2, :]
    p_re = ppow_ref[0, 0]
    p_im = ppow_ref[0, 1]
    h_re, h_im = (h_re + p_re * c_re - p_im * c_im,
                  h_im + p_re * c_im + p_im * c_re)
    carry_ref[0:1, :] = h_re[tt - 1:tt, :]
    carry_ref[1:2, :] = h_im[tt - 1:tt, :]
    hcat = jnp.concatenate([h_re, h_im], axis=1).astype(BF16)
    y = jnp.dot(hcat, cc_ref[0], preferred_element_type=F32) + dskip_ref[0] * u
    y_ref[0] = jax.nn.gelu(y).astype(y_ref.dtype)


def _ssm_scan(u, lam_re, lam_im, log_dt, b_re, b_im, c_re, c_im, d_skip, *, tt):
    bsz, seq, width = u.shape
    n_groups, n_state = lam_re.shape
    gs = SSM_GROUP
    gpb = LANES // gs
    n_gb = n_groups // gpb
    ns = gpb * n_state
    n_steps = int(math.log2(tt))
    dt = jnp.exp(log_dt)[:, None]
    mag = jnp.exp(lam_re * dt)
    a_re = mag * jnp.cos(lam_im * dt)
    a_im = mag * jnp.sin(lam_im * dt)
    den = lam_re * lam_re + lam_im * lam_im
    coef_re = ((a_re - 1.0) * lam_re + a_im * lam_im) / den
    coef_im = (a_im * lam_re - (a_re - 1.0) * lam_im) / den
    bb_re = coef_re[..., None] * b_re - coef_im[..., None] * b_im
    bb_im = coef_re[..., None] * b_im + coef_im[..., None] * b_re
    eye = jnp.eye(gpb, dtype=F32)

    def block_diag_in(bb):
        t = bb.reshape(n_gb, gpb, n_state, gs)
        return jnp.einsum('ngpc,gh->ngchp', t, eye).reshape(n_gb, gpb * gs, gpb * n_state)

    def block_diag_out(cm):
        t = cm.reshape(n_gb, gpb, gs, n_state)
        return jnp.einsum('ngcp,gh->ngphc', t, eye).reshape(n_gb, gpb * n_state, gpb * gs)

    bbt = jnp.concatenate([block_diag_in(bb_re), block_diag_in(bb_im)], axis=2).astype(BF16)
    cc = jnp.concatenate([block_diag_out(c_re), -block_diag_out(c_im)], axis=1).astype(BF16)

    def powers(e):
        ang = lam_im[None] * dt[None] * e[:, None, None]
        mg = jnp.exp(lam_re[None] * dt[None] * e[:, None, None])
        pw = jnp.stack([mg * jnp.cos(ang), mg * jnp.sin(ang)], axis=1)
        pw = pw.reshape(e.shape[0], 2, n_gb, ns)
        return jnp.transpose(pw, (2, 0, 1, 3))

    apow = powers(2.0 ** jnp.arange(n_steps, dtype=F32))
    ppow = jnp.transpose(powers(jnp.arange(1, tt + 1, dtype=F32)), (0, 2, 1, 3))
    dsk = d_skip.reshape(n_gb, 1, gpb * gs)
    return pl.pallas_call(
        functools.partial(_ssm_scan_kernel, tt=tt, ns=ns, n_steps=n_steps),
        grid=(bsz, n_gb, seq // tt),
        in_specs=[pl.BlockSpec((1, tt, LANES), lambda b, g, i: (b, i, g)),
                  pl.BlockSpec((1, LANES, 2 * ns), lambda b, g, i: (g, 0, 0)),
                  pl.BlockSpec((1, n_steps, 2, ns), lambda b, g, i: (g, 0, 0, 0)),
                  pl.BlockSpec((1, 2, tt, ns), lambda b, g, i: (g, 0, 0, 0)),
                  pl.BlockSpec((1, 2 * ns, LANES), lambda b, g, i: (g, 0, 0)),
                  pl.BlockSpec((1, 1, LANES), lambda b, g, i: (g, 0, 0))],
        out_specs=pl.BlockSpec((1, tt, LANES), lambda b, g, i: (b, i, g)),
        out_shape=jax.ShapeDtypeStruct((bsz, seq, width), BF16),
        scratch_shapes=[pltpu.VMEM((SUBLANES, ns), F32)],
        compiler_params=_cparams(("arbitrary", "arbitrary", "arbitrary")),
        name="ssm_scan",
    )(u, bbt, apow, ppow, cc, dsk)


def kernel(x, c, mod_w, mod_b, norm_mix_g, norm_ffn_g, attn_w_in, attn_w_out, ssm_w_in,
           ssm_lam_re, ssm_lam_im, ssm_log_dt, ssm_b_re, ssm_b_im, ssm_c_re, ssm_c_im,
           ssm_d, ssm_w_glu, ffn_w_up, ffn_conv_w, ffn_conv_b, ffn_w_down, final_g):
    bsz, seq, d = x.shape
    depth = mod_w.shape[0]
    n_keep = min(TOPK_MAX, seq // 4)
    tq = 256
    tk = min(512, seq)
    tm = min(512, seq)
    dff = ffn_w_down.shape[1]
    tn = dff // 2

    mod = _mod_vectors(c, mod_w, mod_b)
    fg = final_g.reshape(1, d)
    for i in range(depth):
        sh1, sc1, g1, sh2, sc2, g2 = [mod[i, :, None, k * d:(k + 1) * d] for k in range(6)]
        gm = norm_mix_g[i].reshape(1, d)
        gf = norm_ffn_g[i].reshape(1, d)
        j = i // 2
        if i % 2 == 0:
            q, k, v, qcat, kcat, wi = _attn_proj(x, gm, sh1, sc1, attn_w_in[j], tm=min(256, seq))
            bits = _topk_mask(qcat, kcat, wi, tq=tq, n_keep=n_keep)
            att = _attention(q, k, v, bits, tq=tq, tk=tk)
            x = _proj_residual(att, attn_w_out[j], x, g1, tm=tm, glu=False, name="attn_out")
        else:
            u = _ssm_in(x, gm, sh1, sc1, ssm_w_in[j], tm=tm)
            y = _ssm_scan(u, ssm_lam_re[j], ssm_lam_im[j], ssm_log_dt[j], ssm_b_re[j],
                          ssm_b_im[j], ssm_c_re[j], ssm_c_im[j], ssm_d[j], tt=min(256, seq))
            x = _proj_residual(y, ssm_w_glu[j], x, g1, tm=tm, glu=True, name="ssm_glu")
        a = _ffn_up(x, gf, sh2, sc2, ffn_w_up[i], ffn_conv_w[i], ffn_conv_b[i], tm=tm, tn=tn)
        x = _ffn_down(a, ffn_w_down[i], x, g2, fg, tm=tm, final=(i == depth - 1))
    return x
```

```python
import functools
import math

import jax
import jax.numpy as jnp
from jax import lax
from jax.experimental import pallas as pl
from jax.experimental.pallas import tpu as pltpu

F32 = jnp.float32
BF16 = jnp.bfloat16
I32 = jnp.int32
I16 = jnp.int16

EPS = 1e-6
CHUNK = 64
N_HEADS = 16
HEAD_DIM = 64
IDX_HEADS = 8
IDX_DIM = 64
TOPK_MAX = 256
SSM_GROUP = 16
SSM_STATE = 64
CONV_WIDTH = 3

LANES = 128
SUBLANES = 8
MASK_BITS = 32
VMEM_LIMIT = 56 * 1024 * 1024

INT_MIN = -2 ** 31
KEY_NEG_INF = INT_MIN + 0x7FFFFF
MASKED_DIST = 1e30
M_INIT = -3e38

_NT = (((1,), (1,)), ((), ()))


def _cparams(sem):
    return pltpu.CompilerParams(dimension_semantics=sem, vmem_limit_bytes=VMEM_LIMIT)


def _norm_mod(x, g, sh, sc):
    ms = jnp.mean(x * x, axis=-1, keepdims=True)
    y = x * lax.rsqrt(ms + EPS)
    return (y * g) * (1.0 + sc) + sh


def _rms(x, g):
    ms = jnp.mean(x * x, axis=-1, keepdims=True)
    return (x * lax.rsqrt(ms + EPS)) * g


def _mod_kernel(c_ref, w_ref, b_ref, o_ref):
    c = c_ref[...]
    cond = c * (1.0 / (1.0 + jnp.exp(-c)))
    o_ref[...] = jnp.dot(cond, w_ref[...], preferred_element_type=F32,
                         precision=lax.Precision.HIGHEST) + b_ref[...]


def _mod_vectors(c, mod_w, mod_b):
    depth, d, n6 = mod_w.shape
    bsz = c.shape[0]
    rows = SUBLANES
    c_pad = jnp.zeros((rows, d), F32).at[:bsz].set(c)
    out = pl.pallas_call(
        _mod_kernel,
        grid=(depth, n6 // d),
        in_specs=[pl.BlockSpec((rows, d), lambda i, j: (0, 0)),
                  pl.BlockSpec((None, d, d), lambda i, j: (i, 0, j)),
                  pl.BlockSpec((None, 1, d), lambda i, j: (i, 0, j))],
        out_specs=pl.BlockSpec((None, rows, d), lambda i, j: (i, 0, j)),
        out_shape=jax.ShapeDtypeStruct((depth, rows, n6), F32),
        compiler_params=_cparams(("arbitrary", "arbitrary")),
        name="mod_vectors",
    )(c_pad, mod_w, mod_b.reshape(depth, 1, n6))
    return out[:, :bsz, :]


def _attn_proj_kernel(x_ref, g_ref, sh_ref, sc_ref, wqkv_ref, wih_ref, wil_ref,
                      q_ref, k_ref, v_ref, qcat_ref, kcat_ref, wi_ref, *, d):
    h = _norm_mod(x_ref[0], g_ref[...], sh_ref[0], sc_ref[0])
    hb = h.astype(BF16)
    hl = (h - hb.astype(F32)).astype(BF16)
    qkv = jnp.dot(hb, wqkv_ref[...], preferred_element_type=F32)
    q_ref[0] = (qkv[:, :d] * (HEAD_DIM ** -0.5)).astype(BF16)
    k_ref[0] = qkv[:, d:2 * d].astype(BF16)
    v_ref[0] = qkv[:, 2 * d:3 * d].astype(BF16)
    wih = wih_ref[...]
    idx = (jnp.dot(hb, wih, preferred_element_type=F32)
           + jnp.dot(hl, wih, preferred_element_type=F32)
           + jnp.dot(hb, wil_ref[...], preferred_element_type=F32))
    nq = IDX_HEADS * LANES
    qi2 = idx[:, :nq] * (IDX_DIM ** -0.5)
    qhi = qi2.astype(BF16)
    qlo = (qi2 - qhi.astype(F32)).astype(BF16)
    lane = lax.broadcasted_iota(I32, qi2.shape, 1)
    sel = jnp.where((lane & (LANES - 1)) < IDX_DIM, qhi, qlo)
    for hh in range(IDX_HEADS):
        piece = sel[:, hh * LANES:(hh + 1) * LANES]
        qcat_ref[0, :, 2 * hh * LANES:(2 * hh + 1) * LANES] = piece
        qcat_ref[0, :, (2 * hh + 1) * LANES:(2 * hh + 2) * LANES] = piece
    ki2 = idx[:, nq:nq + LANES]
    khi = ki2.astype(BF16)
    klo = (ki2 - khi.astype(F32)).astype(BF16)
    kcat_ref[0, :, :LANES] = khi
    kcat_ref[0, :, LANES:] = klo
    wi_ref[0] = idx[:, nq + LANES:] * (IDX_HEADS ** -0.5)


def _attn_proj(x, g, sh, sc, w_in, *, tm):
    bsz, seq, d = x.shape
    nq = IDX_HEADS * IDX_DIM
    wqkv = w_in[:, :3 * d].astype(BF16)
    wq_i = w_in[:, 3 * d:3 * d + nq].reshape(d, IDX_HEADS, IDX_DIM)
    wq_i = jnp.concatenate([wq_i, wq_i], axis=-1).reshape(d, IDX_HEADS * LANES)
    wk_i = w_in[:, 3 * d + nq:3 * d + nq + IDX_DIM]
    wk_i = jnp.concatenate([wk_i, wk_i], axis=-1)
    ww_i = w_in[:, 3 * d + nq + IDX_DIM:]
    ww_i = jnp.pad(ww_i, ((0, 0), (0, LANES - ww_i.shape[1])))
    w_idx = jnp.concatenate([wq_i, wk_i, ww_i], axis=-1)
    w_hi = w_idx.astype(BF16)
    w_lo = (w_idx - w_hi.astype(F32)).astype(BF16)
    n_idx = w_idx.shape[1]
    row = lambda b, i: (b, i, 0)
    const = lambda b, i: (0, 0)
    per_b = lambda b, i: (b, 0, 0)
    return pl.pallas_call(
        functools.partial(_attn_proj_kernel, d=d),
        grid=(bsz, seq // tm),
        in_specs=[pl.BlockSpec((1, tm, d), row),
                  pl.BlockSpec((1, d), const),
                  pl.BlockSpec((1, 1, d), per_b),
                  pl.BlockSpec((1, 1, d), per_b),
                  pl.BlockSpec((d, 3 * d), const),
                  pl.BlockSpec((d, n_idx), const),
                  pl.BlockSpec((d, n_idx), const)],
        out_specs=[pl.BlockSpec((1, tm, d), row),
                   pl.BlockSpec((1, tm, d), row),
                   pl.BlockSpec((1, tm, d), row),
                   pl.BlockSpec((1, tm, 2 * IDX_HEADS * LANES), row),
                   pl.BlockSpec((1, tm, 2 * LANES), row),
                   pl.BlockSpec((1, tm, LANES), row)],
        out_shape=[jax.ShapeDtypeStruct((bsz, seq, d), BF16),
                   jax.ShapeDtypeStruct((bsz, seq, d), BF16),
                   jax.ShapeDtypeStruct((bsz, seq, d), BF16),
                   jax.ShapeDtypeStruct((bsz, seq, 2 * IDX_HEADS * LANES), BF16),
                   jax.ShapeDtypeStruct((bsz, seq, 2 * LANES), BF16),
                   jax.ShapeDtypeStruct((bsz, seq, LANES), F32)],
        compiler_params=_cparams(("arbitrary", "arbitrary")),
        name="attn_proj",
    )(x, g, sh, sc, wqkv, w_hi, w_lo)


def _topk_mask_kernel(qcat_ref, kcat_ref, wi_ref, bits_ref, hi_ref, lo_ref, pk_ref,
                      *, tq, n_keep, cr, unroll, tie_bits):
    qi = pl.program_id(1)
    n_adm = (qi + 1) * tq
    t_chunk = (qi * tq + lax.broadcasted_iota(I32, (1, tq), 1)) // CHUNK
    w_t = wi_ref[0].T

    def score_body(kt, carry):
        r0 = pl.multiple_of(kt * tq, tq)
        kc = kcat_ref[0, pl.ds(r0, tq), :]
        sc = jnp.zeros((tq, tq), F32)
        for hh in range(IDX_HEADS):
            lg = lax.dot_general(kc, qcat_ref[0, :, 2 * hh * LANES:(2 * hh + 2) * LANES],
                                 _NT, preferred_element_type=F32)
            sc = sc + w_t[hh:hh + 1, :] * jnp.maximum(lg, 0.0)
        sc = sc + 0.0
        raw = lax.bitcast_convert_type(sc, I32)
        key = raw ^ ((raw >> 31) & 0x7FFFFFFF)
        j_chunk = (r0 + lax.broadcasted_iota(I32, (tq, 1), 0)) // CHUNK
        key = jnp.where(j_chunk <= t_chunk, key, KEY_NEG_INF)
        hi_ref[pl.ds(r0, tq), :] = (key >> 16).astype(I16)
        lo_ref[pl.ds(r0, tq), :] = ((key & 0xFFFF) - 2 ** 15).astype(I16)
        return carry

    lax.fori_loop(0, qi + 1, score_body, 0)

    n_ch = n_adm // (cr * unroll)
    one, zero = jnp.int16(1), jnp.int16(0)

    def count(pred):
        def body(c, acc):
            for uu in range(unroll):
                r0 = pl.multiple_of((c * unroll + uu) * cr, cr)
                acc = acc + jnp.where(pred(r0), one, zero)
            return acc
        acc = lax.fori_loop(0, n_ch, body, jnp.zeros((cr, tq), I16))
        return jnp.sum(acc.astype(I32), axis=0, keepdims=True)

    def hi_at(r0):
        return hi_ref[pl.ds(r0, cr), :]

    def lo_at(r0):
        return lo_ref[pl.ds(r0, cr), :]

    def bcast16(v):
        return jnp.broadcast_to(v, (cr, tq)).astype(I16)

    def hi_round(i, carry):
        thr, c_thr = carry
        cand = thr + jnp.left_shift(jnp.int32(1), 15 - i)
        cand16 = bcast16(cand)
        cnt = count(lambda r0: hi_at(r0) >= cand16)
        ok = cnt >= n_keep
        return jnp.where(ok, cand, thr), jnp.where(ok, cnt, c_thr)

    thr_hi, c_ge_hi = lax.fori_loop(
        0, 16, hi_round,
        (jnp.full((1, tq), -2 ** 15, I32), jnp.full((1, tq), n_adm, I32)))
    thr_hi16 = bcast16(thr_hi)
    c_above = count(lambda r0: hi_at(r0) > thr_hi16)

    def bucket_body(c, carry):
        r0 = pl.multiple_of(c * cr, cr)
        lo_ref[pl.ds(r0, cr), :] = jnp.where(hi_at(r0) == thr_hi16, lo_at(r0),
                                             jnp.int16(-2 ** 15))
        return carry

    lax.fori_loop(0, n_adm // cr, bucket_body, 0)

    def lo_round(i, carry):
        thr, c_thr = carry
        cand = thr + jnp.left_shift(jnp.int32(1), 15 - i)
        cand16 = bcast16(cand)
        cnt = c_above + count(lambda r0: lo_at(r0) >= cand16)
        ok = cnt >= n_keep
        return jnp.where(ok, cand, thr), jnp.where(ok, cnt, c_thr)

    thr_lo, c_thr = lax.fori_loop(
        0, 16, lo_round, (jnp.full((1, tq), -2 ** 15, I32), c_ge_hi))
    thr_lo16 = bcast16(thr_lo)

    n_tied_ok = n_keep - (c_above + count(lambda r0: lo_at(r0) > thr_lo16))
    need_ties = jnp.max(c_thr) > n_keep

    def tied(r0):
        return (lo_at(r0) == thr_lo16) & (hi_at(r0) == thr_hi16)

    def tie_round(i, cut):
        cand = cut + jnp.left_shift(jnp.int32(1), tie_bits - 1 - i)
        cand16 = bcast16(cand)
        cnt = count(lambda r0: tied(r0)
                    & ((r0 + lax.broadcasted_iota(I32, (cr, tq), 0)).astype(I16) < cand16))
        return jnp.where(cnt <= n_tied_ok, cand, cut)

    cut = lax.fori_loop(0, jnp.where(need_ties, tie_bits, 0), tie_round,
                        jnp.zeros((1, tq), I32))
    cut = jnp.where(need_ties, cut, jnp.int32(2 ** 15 - 1))
    cut16 = bcast16(cut)

    pk_ref[...] = jnp.zeros(pk_ref.shape, I32)
    n_sub = LANES // cr

    def pack_body(c, carry):
        pieces = []
        for uu in range(n_sub):
            r0 = pl.multiple_of(c * LANES + uu * cr, cr)
            j16 = (r0 + lax.broadcasted_iota(I32, (cr, tq), 0)).astype(I16)
            sel = ((hi_at(r0) > thr_hi16) | (lo_at(r0) > thr_lo16)
                   | (tied(r0) & (j16 < cut16)))
            pieces.append(jnp.where(sel, one, zero).astype(I32))
        sel32 = jnp.concatenate(pieces, axis=0)
        j = c * LANES + lax.broadcasted_iota(I32, (LANES, 1), 0)
        sel32 = jnp.where((j // CHUNK) <= t_chunk, sel32, 0)
        w0 = pl.multiple_of((c // MASK_BITS) * LANES, LANES)
        pk_ref[pl.ds(w0, LANES), :] = pk_ref[pl.ds(w0, LANES), :] | jnp.left_shift(
            sel32, c % MASK_BITS)
        return carry

    lax.fori_loop(0, n_adm // LANES, pack_body, 0)
    bits_ref[0] = pk_ref[...].T


def _topk_mask(qcat, kcat, wi, *, tq, n_keep):
    bsz, seq, _ = qcat.shape
    assert seq < 2 ** 15, "key indices are compared as int16"
    n_words = -(-seq // (MASK_BITS * LANES))
    tie_bits = max(1, int(seq).bit_length())
    return pl.pallas_call(
        functools.partial(_topk_mask_kernel, tq=tq, n_keep=n_keep, cr=64, unroll=tq // 64,
                          tie_bits=tie_bits),
        grid=(bsz, seq // tq),
        in_specs=[pl.BlockSpec((1, tq, qcat.shape[2]), lambda b, i: (b, i, 0)),
                  pl.BlockSpec((1, seq, kcat.shape[2]), lambda b, i: (b, 0, 0)),
                  pl.BlockSpec((1, tq, LANES), lambda b, i: (b, i, 0))],
        out_specs=pl.BlockSpec((1, tq, n_words * LANES), lambda b, i: (b, i, 0)),
        out_shape=jax.ShapeDtypeStruct((bsz, seq, n_words * LANES), I32),
        scratch_shapes=[pltpu.VMEM((seq, tq), I16),
                        pltpu.VMEM((seq, tq), I16),
                        pltpu.VMEM((n_words * LANES, tq), I32)],
        compiler_params=_cparams(("arbitrary", "arbitrary")),
        name="topk_mask",
    )(qcat, kcat, wi)


def _attn_kernel(q_ref, k_ref, v_ref, bits_ref, o_ref, qm_ref, m_ref, l_ref, acc_ref,
                 *, tq, tk, nk):
    qi = pl.program_id(1)
    kt = pl.program_id(2)
    last = ((qi + 1) * tq - 1) // tk
    n_sub = tk // LANES
    lane = lax.broadcasted_iota(I32, (tq, LANES), 1)
    lo_half = lane < HEAD_DIM

    @pl.when(kt == 0)
    def _init():
        m_ref[...] = jnp.full(m_ref.shape, M_INIT, F32)
        l_ref[...] = jnp.zeros(l_ref.shape, F32)
        acc_ref[...] = jnp.zeros(acc_ref.shape, F32)
        for hp in range(N_HEADS // 2):
            q2 = q_ref[0, :, hp * LANES:(hp + 1) * LANES]
            zero = jnp.zeros_like(q2)
            qm_ref[2 * hp] = jnp.where(lo_half, q2, zero)
            qm_ref[2 * hp + 1] = jnp.where(lo_half, zero, q2)

    @pl.when(kt <= last)
    def _compute():
        t = qi * tq + lax.broadcasted_iota(I32, (tq, 1), 0)
        j = kt * tk + lax.broadcasted_iota(I32, (1, tk), 1)
        dist = jnp.abs(t - j).astype(F32)
        bits = bits_ref[0]
        sel = jnp.concatenate(
            [(bits >> ((kt * n_sub + i) % MASK_BITS)) & 1 for i in range(n_sub)], axis=1)
        dm = jnp.where(sel == 1, dist, MASKED_DIST)
        for hp in range(N_HEADS // 2):
            k2 = k_ref[0, :, hp * LANES:(hp + 1) * LANES]
            v2 = v_ref[0, :, hp * LANES:(hp + 1) * LANES]
            alphas, pvs = [], []
            for e in range(2):
                h = 2 * hp + e
                slope = 2.0 ** (-8.0 * (h + 1) / N_HEADS)
                s = lax.dot_general(qm_ref[h], k2, _NT, preferred_element_type=F32)
                s = s - slope * dm
                m_old = m_ref[h]
                m_new = jnp.maximum(m_old, jnp.max(s, axis=1, keepdims=True))
                p = jnp.exp(s - jnp.tile(m_new, (1, n_sub)))
                alpha = jnp.exp(m_old - m_new)
                l_ref[h] = alpha * l_ref[h] + jnp.sum(p, axis=1, keepdims=True)
                m_ref[h] = m_new
                pvs.append(jnp.dot(p.astype(BF16), v2, preferred_element_type=F32))
                alphas.append(alpha)
            sl = slice(hp * LANES, (hp + 1) * LANES)
            acc_ref[:, sl] = (acc_ref[:, sl] * jnp.where(lo_half, alphas[0], alphas[1])
                              + jnp.where(lo_half, pvs[0], pvs[1]))

    @pl.when(kt == nk - 1)
    def _finish():
        for hp in range(N_HEADS // 2):
            sl = slice(hp * LANES, (hp + 1) * LANES)
            denom = jnp.where(lo_half, l_ref[2 * hp], l_ref[2 * hp + 1])
            o_ref[0, :, sl] = (acc_ref[:, sl] / denom).astype(o_ref.dtype)


def _attention(q, k, v, bits, *, tq, tk):
    bsz, seq, d = q.shape
    nk = seq // tk
    n_sub = tk // LANES

    def last_kt(i):
        return ((i + 1) * tq - 1) // tk

    kv_map = lambda b, i, t: (b, jnp.minimum(t, last_kt(i)), 0)
    bits_map = lambda b, i, t: (b, i, (jnp.minimum(t, last_kt(i)) * n_sub) // MASK_BITS)
    return pl.pallas_call(
        functools.partial(_attn_kernel, tq=tq, tk=tk, nk=nk),
        grid=(bsz, seq // tq, nk),
        in_specs=[pl.BlockSpec((1, tq, d), lambda b, i, t: (b, i, 0)),
                  pl.BlockSpec((1, tk, d), kv_map),
                  pl.BlockSpec((1, tk, d), kv_map),
                  pl.BlockSpec((1, tq, LANES), bits_map)],
        out_specs=pl.BlockSpec((1, tq, d), lambda b, i, t: (b, i, 0)),
        out_shape=jax.ShapeDtypeStruct((bsz, seq, d), BF16),
        scratch_shapes=[pltpu.VMEM((N_HEADS, tq, LANES), BF16),
                        pltpu.VMEM((N_HEADS, tq, LANES), F32),
                        pltpu.VMEM((N_HEADS, tq, LANES), F32),
                        pltpu.VMEM((tq, d), F32)],
        compiler_params=_cparams(("arbitrary", "arbitrary", "arbitrary")),
        name="masked_attention",
    )(q, k, v, bits)


def _proj_residual_kernel(y_ref, w_ref, x_ref, gate_ref, o_ref, *, glu, d):
    r = jnp.dot(y_ref[0], w_ref[...], preferred_element_type=F32)
    if glu:
        val, gt = r[:, :d], r[:, d:]
        r = val * (1.0 / (1.0 + jnp.exp(-gt)))
    o_ref[0] = x_ref[0] + gate_ref[0] * r


def _proj_residual(y, w, x, gate, *, tm, glu, name):
    bsz, seq, d = x.shape
    kdim, n = w.shape
    row = lambda b, i: (b, i, 0)
    return pl.pallas_call(
        functools.partial(_proj_residual_kernel, glu=glu, d=d),
        grid=(bsz, seq // tm),
        in_specs=[pl.BlockSpec((1, tm, kdim), row),
                  pl.BlockSpec((kdim, n), lambda b, i: (0, 0)),
                  pl.BlockSpec((1, tm, d), row),
                  pl.BlockSpec((1, 1, d), lambda b, i: (b, 0, 0))],
        out_specs=pl.BlockSpec((1, tm, d), row),
        out_shape=jax.ShapeDtypeStruct((bsz, seq, d), F32),
        compiler_params=_cparams(("arbitrary", "arbitrary")),
        name=name,
    )(y, w.astype(BF16), x, gate)


def _ffn_up_kernel(x_ref, g_ref, sh_ref, sc_ref, wv_ref, wg_ref, cwv_ref, cwg_ref,
                   cbv_ref, cbg_ref, a_ref, zv_ref, zg_ref, *, tm):
    i = pl.program_id(2)
    halo = SUBLANES

    @pl.when(i == 0)
    def _seq_start():
        zv_ref[0:halo, :] = jnp.zeros((halo, zv_ref.shape[1]), F32)
        zg_ref[0:halo, :] = jnp.zeros((halo, zg_ref.shape[1]), F32)

    h = _norm_mod(x_ref[0], g_ref[...], sh_ref[0], sc_ref[0]).astype(BF16)
    zv_ref[halo:halo + tm, :] = jnp.dot(h, wv_ref[...], preferred_element_type=F32)
    zg_ref[halo:halo + tm, :] = jnp.dot(h, wg_ref[...], preferred_element_type=F32)

    def conv(z_ref, cw_ref, cb_ref):
        out = cb_ref[...]
        for tap in range(CONV_WIDTH):
            off = halo - (CONV_WIDTH - 1) + tap
            out = out + cw_ref[tap:tap + 1, :] * z_ref[off:off + tm, :]
        return out

    val = conv(zv_ref, cwv_ref, cbv_ref)
    gt = conv(zg_ref, cwg_ref, cbg_ref)
    a_ref[0] = (gt * (1.0 / (1.0 + jnp.exp(-gt))) * val).astype(a_ref.dtype)
    zv_ref[0:halo, :] = zv_ref[tm:tm + halo, :]
    zg_ref[0:halo, :] = zg_ref[tm:tm + halo, :]


def _ffn_up(x, g, sh, sc, w_up, conv_w, conv_b, *, tm, tn):
    bsz, seq, d = x.shape
    dff = w_up.shape[1] // 2
    nj = dff // tn
    wv = w_up[:, :dff].astype(BF16)
    wg = w_up[:, dff:].astype(BF16)
    cwv = jnp.pad(conv_w[:, :dff], ((0, SUBLANES - CONV_WIDTH), (0, 0)))
    cwg = jnp.pad(conv_w[:, dff:], ((0, SUBLANES - CONV_WIDTH), (0, 0)))
    cbv = conv_b[:dff].reshape(1, dff)
    cbg = conv_b[dff:].reshape(1, dff)
    col = lambda j, b, i: (0, j)
    return pl.pallas_call(
        functools.partial(_ffn_up_kernel, tm=tm),
        grid=(nj, bsz, seq // tm),
        in_specs=[pl.BlockSpec((1, tm, d), lambda j, b, i: (b, i, 0)),
                  pl.BlockSpec((1, d), lambda j, b, i: (0, 0)),
                  pl.BlockSpec((1, 1, d), lambda j, b, i: (b, 0, 0)),
                  pl.BlockSpec((1, 1, d), lambda j, b, i: (b, 0, 0)),
                  pl.BlockSpec((d, tn), col),
                  pl.BlockSpec((d, tn), col),
                  pl.BlockSpec((SUBLANES, tn), col),
                  pl.BlockSpec((SUBLANES, tn), col),
                  pl.BlockSpec((1, tn), col),
                  pl.BlockSpec((1, tn), col)],
        out_specs=pl.BlockSpec((1, tm, tn), lambda j, b, i: (b, i, j)),
        out_shape=jax.ShapeDtypeStruct((bsz, seq, dff), BF16),
        scratch_shapes=[pltpu.VMEM((tm + SUBLANES, tn), F32),
                        pltpu.VMEM((tm + SUBLANES, tn), F32)],
        compiler_params=_cparams(("arbitrary", "arbitrary", "arbitrary")),
        name="ffn_up",
    )(x, g, sh, sc, wv, wg, cwv, cwg, cbv, cbg)


def _ffn_down_kernel(a_ref, w_ref, x_ref, gate_ref, fg_ref, o_ref, *, final):
    r = jnp.dot(a_ref[0], w_ref[...], preferred_element_type=F32)
    out = x_ref[0] + gate_ref[0] * r
    if final:
        out = _rms(out, fg_ref[...])
    o_ref[0] = out


def _ffn_down(a, w_down, x, gate, final_g, *, tm, final):
    bsz, seq, d = x.shape
    dff = a.shape[2]
    row = lambda b, i: (b, i, 0)
    return pl.pallas_call(
        functools.partial(_ffn_down_kernel, final=final),
        grid=(bsz, seq // tm),
        in_specs=[pl.BlockSpec((1, tm, dff), row),
                  pl.BlockSpec((dff, d), lambda b, i: (0, 0)),
                  pl.BlockSpec((1, tm, d), row),
                  pl.BlockSpec((1, 1, d), lambda b, i: (b, 0, 0)),
                  pl.BlockSpec((1, d), lambda b, i: (0, 0))],
        out_specs=pl.BlockSpec((1, tm, d), row),
        out_shape=jax.ShapeDtypeStruct((bsz, seq, d), F32),
        compiler_params=_cparams(("arbitrary", "arbitrary")),
        name="ffn_down_final" if final else "ffn_down",
    )(a, w_down.astype(BF16), x, gate, final_g)


def _ssm_in_kernel(x_ref, g_ref, sh_ref, sc_ref, w_ref, u_ref):
    h = _norm_mod(x_ref[0], g_ref[...], sh_ref[0], sc_ref[0]).astype(BF16)
    u_ref[0] = jnp.dot(h, w_ref[...], preferred_element_type=F32)


def _ssm_in(x, g, sh, sc, w, *, tm):
    bsz, seq, d = x.shape
    n = w.shape[1]
    row = lambda b, i: (b, i, 0)
    return pl.pallas_call(
        _ssm_in_kernel,
        grid=(bsz, seq // tm),
        in_specs=[pl.BlockSpec((1, tm, d), row),
                  pl.BlockSpec((1, d), lambda b, i: (0, 0)),
                  pl.BlockSpec((1, 1, d), lambda b, i: (b, 0, 0)),
                  pl.BlockSpec((1, 1, d), lambda b, i: (b, 0, 0)),
                  pl.BlockSpec((d, n), lambda b, i: (0, 0))],
        out_specs=pl.BlockSpec((1, tm, n), row),
        out_shape=jax.ShapeDtypeStruct((bsz, seq, n), F32),
        compiler_params=_cparams(("arbitrary", "arbitrary")),
        name="ssm_in",
    )(x, g, sh, sc, w.astype(BF16))


def _ssm_scan_kernel(u_ref, bbt_ref, apow_ref, ppow_ref, cc_ref, dskip_ref, y_ref,
                     carry_ref, *, tt, ns, n_steps):
    i = pl.program_id(2)

    @pl.when(i == 0)
    def _seq_start():
        carry_ref[...] = jnp.zeros(carry_ref.shape, F32)

    u = u_ref[0]
    bu = jnp.dot(u.astype(BF16), bbt_ref[0], preferred_element_type=F32)
    h_re, h_im = bu[:, :ns], bu[:, ns:]
    row = lax.broadcasted_iota(I32, (tt, ns), 0)
    for kk in range(n_steps):
        dsh = 1 << kk
        a_re = apow_ref[0, kk, 0:1, :]
        a_im = apow_ref[0, kk, 1:2, :]
        keep = row >= dsh
        s_re = jnp.where(keep, pltpu.roll(h_re, dsh, axis=0), 0.0)
        s_im = jnp.where(keep, pltpu.roll(h_im, dsh, axis=0), 0.0)
        h_re, h_im = (h_re + a_re * s_re - a_im * s_im,
                      h_im + a_re * s_im + a_im * s_re)
    c_re = carry_ref[0:1, :]
    c_im = carry_ref[1:2, :]
    p_re = ppow_ref[0, 0]
    p_im = ppow_ref[0, 1]
    h_re, h_im = (h_re + p_re * c_re - p_im * c_im,
                  h_im + p_re * c_im + p_im * c_re)
    carry_ref[0:1, :] = h_re[tt - 1:tt, :]
    carry_ref[1:2, :] = h_im[tt - 1:tt, :]
    hcat = jnp.concatenate([h_re, h_im], axis=1).astype(BF16)
    y = jnp.dot(hcat, cc_ref[0], preferred_element_type=F32) + dskip_ref[0] * u
    y_ref[0] = jax.nn.gelu(y).astype(y_ref.dtype)


def _ssm_scan(u, lam_re, lam_im, log_dt, b_re, b_im, c_re, c_im, d_skip, *, tt):
    bsz, seq, width = u.shape
    n_groups, n_state = lam_re.shape
    gs = SSM_GROUP
    gpb = LANES // gs
    n_gb = n_groups // gpb
    ns = gpb * n_state
    n_steps = int(math.log2(tt))
    dt = jnp.exp(log_dt)[:, None]
    mag = jnp.exp(lam_re * dt)
    a_re = mag * jnp.cos(lam_im * dt)
    a_im = mag * jnp.sin(lam_im * dt)
    den = lam_re * lam_re + lam_im * lam_im
    coef_re = ((a_re - 1.0) * lam_re + a_im * lam_im) / den
    coef_im = (a_im * lam_re - (a_re - 1.0) * lam_im) / den
    bb_re = coef_re[..., None] * b_re - coef_im[..., None] * b_im
    bb_im = coef_re[..., None] * b_im + coef_im[..., None] * b_re
    eye = jnp.eye(gpb, dtype=F32)

    def block_diag_in(bb):
        t = bb.reshape(n_gb, gpb, n_state, gs)
        return jnp.einsum('ngpc,gh->ngchp', t, eye).reshape(n_gb, gpb * gs, gpb * n_state)

    def block_diag_out(cm):
        t = cm.reshape(n_gb, gpb, gs, n_state)
        return jnp.einsum('ngcp,gh->ngphc', t, eye).reshape(n_gb, gpb * n_state, gpb * gs)

    bbt = jnp.concatenate([block_diag_in(bb_re), block_diag_in(bb_im)], axis=2).astype(BF16)
    cc = jnp.concatenate([block_diag_out(c_re), -block_diag_out(c_im)], axis=1).astype(BF16)

    def powers(e):
        ang = lam_im[None] * dt[None] * e[:, None, None]
        mg = jnp.exp(lam_re[None] * dt[None] * e[:, None, None])
        pw = jnp.stack([mg * jnp.cos(ang), mg * jnp.sin(ang)], axis=1)
        pw = pw.reshape(e.shape[0], 2, n_gb, ns)
        return jnp.transpose(pw, (2, 0, 1, 3))

    apow = powers(2.0 ** jnp.arange(n_steps, dtype=F32))
    ppow = jnp.transpose(powers(jnp.arange(1, tt + 1, dtype=F32)), (0, 2, 1, 3))
    dsk = d_skip.reshape(n_gb, 1, gpb * gs)
    return pl.pallas_call(
        functools.partial(_ssm_scan_kernel, tt=tt, ns=ns, n_steps=n_steps),
        grid=(bsz, n_gb, seq // tt),
        in_specs=[pl.BlockSpec((1, tt, LANES), lambda b, g, i: (b, i, g)),
                  pl.BlockSpec((1, LANES, 2 * ns), lambda b, g, i: (g, 0, 0)),
                  pl.BlockSpec((1, n_steps, 2, ns), lambda b, g, i: (g, 0, 0, 0)),
                  pl.BlockSpec((1, 2, tt, ns), lambda b, g, i: (g, 0, 0, 0)),
                  pl.BlockSpec((1, 2 * ns, LANES), lambda b, g, i: (g, 0, 0)),
                  pl.BlockSpec((1, 1, LANES), lambda b, g, i: (g, 0, 0))],
        out_specs=pl.BlockSpec((1, tt, LANES), lambda b, g, i: (b, i, g)),
        out_shape=jax.ShapeDtypeStruct((bsz, seq, width), BF16),
        scratch_shapes=[pltpu.VMEM((SUBLANES, ns), F32)],
        compiler_params=_cparams(("arbitrary", "arbitrary", "arbitrary")),
        name="ssm_scan",
    )(u, bbt, apow, ppow, cc, dsk)


def kernel(x, c, mod_w, mod_b, norm_mix_g, norm_ffn_g, attn_w_in, attn_w_out, ssm_w_in,
           ssm_lam_re, ssm_lam_im, ssm_log_dt, ssm_b_re, ssm_b_im, ssm_c_re, ssm_c_im,
           ssm_d, ssm_w_glu, ffn_w_up, ffn_conv_w, ffn_conv_b, ffn_w_down, final_g):
    bsz, seq, d = x.shape
    depth = mod_w.shape[0]
    n_keep = min(TOPK_MAX, seq // 4)
    tq = 256
    tk = min(512, seq)
    tm = min(512, seq)
    dff = ffn_w_down.shape[1]
    tn = dff // 2

    mod = _mod_vectors(c, mod_w, mod_b)
    fg = final_g.reshape(1, d)
    for i in range(depth):
        sh1, sc1, g1, sh2, sc2, g2 = [mod[i, :, None, k * d:(k + 1) * d] for k in range(6)]
        gm = norm_mix_g[i].reshape(1, d)
        gf = norm_ffn_g[i].reshape(1, d)
        j = i // 2
        if i % 2 == 0:
            q, k, v, qcat, kcat, wi = _attn_proj(x, gm, sh1, sc1, attn_w_in[j], tm=min(256, seq))
            bits = _topk_mask(qcat, kcat, wi, tq=tq, n_keep=n_keep)
            att = _attention(q, k, v, bits, tq=tq, tk=tk)
            x = _proj_residual(att, attn_w_out[j], x, g1, tm=tm, glu=False, name="attn_out")
        else:
            u = _ssm_in(x, gm, sh1, sc1, ssm_w_in[j], tm=tm)
            y = _ssm_scan(u, ssm_lam_re[j], ssm_lam_im[j], ssm_log_dt[j], ssm_b_re[j],
                          ssm_b_im[j], ssm_c_re[j], ssm_c_im[j], ssm_d[j], tt=min(256, seq))
            x = _proj_residual(y, ssm_w_glu[j], x, g1, tm=tm, glu=True, name="ssm_glu")
        a = _ffn_up(x, gf, sh2, sc2, ffn_w_up[i], ffn_conv_w[i], ffn_conv_b[i], tm=tm, tn=tn)
        x = _ffn_down(a, ffn_w_down[i], x, g2, fg, tm=tm, final=(i == depth - 1))
    return x
```

```python
import functools
import math

import jax
import jax.numpy as jnp
import numpy as np
from jax import lax
from jax.experimental import pallas as pl
from jax.experimental.pallas import tpu as pltpu

F32 = jnp.float32
BF16 = jnp.bfloat16
I32 = jnp.int32
I16 = jnp.int16

EPS = 1e-6
CHUNK = 64
N_HEADS = 16
HEAD_DIM = 64
IDX_HEADS = 8
IDX_DIM = 64
TOPK_MAX = 256
SSM_GROUP = 16
SSM_STATE = 64
CONV_WIDTH = 3

LANES = 128
SUBLANES = 8
MASK_BITS = 32
VMEM_LIMIT = 56 * 1024 * 1024

INT_MIN = -2 ** 31
KEY_NEG_INF = INT_MIN + 0x7FFFFF
MASK_BIAS = -1e30
M_INIT = -3e38
LOG2E = math.log2(math.e)


def _bf16_pieces(x, n=3):
    out, rest = [], np.float32(x)
    for _ in range(n):
        piece = np.float32(np.asarray(rest, np.float32).astype(BF16).astype(np.float32))
        out.append(float(piece))
        rest = np.float32(rest - piece)
    return out


def _alibi_slope2(h):
    return 2.0 ** (-8.0 * (h + 1) / N_HEADS) * LOG2E

_NT = (((1,), (1,)), ((), ()))


def _cparams(sem):
    return pltpu.CompilerParams(dimension_semantics=sem, vmem_limit_bytes=VMEM_LIMIT)


def _norm_mod(x, g, sh, sc):
    ms = jnp.mean(x * x, axis=-1, keepdims=True)
    y = x * lax.rsqrt(ms + EPS)
    return (y * g) * (1.0 + sc) + sh


def _rms(x, g):
    ms = jnp.mean(x * x, axis=-1, keepdims=True)
    return (x * lax.rsqrt(ms + EPS)) * g


def _mod_kernel(c_ref, w_ref, b_ref, o_ref):
    c = c_ref[...]
    cond = c * (1.0 / (1.0 + jnp.exp(-c)))
    o_ref[...] = jnp.dot(cond, w_ref[...], preferred_element_type=F32,
                         precision=lax.Precision.HIGHEST) + b_ref[...]


def _mod_vectors(c, mod_w, mod_b):
    depth, d, n6 = mod_w.shape
    bsz = c.shape[0]
    rows = SUBLANES
    c_pad = jnp.zeros((rows, d), F32).at[:bsz].set(c)
    out = pl.pallas_call(
        _mod_kernel,
        grid=(depth, n6 // d),
        in_specs=[pl.BlockSpec((rows, d), lambda i, j: (0, 0)),
                  pl.BlockSpec((None, d, d), lambda i, j: (i, 0, j)),
                  pl.BlockSpec((None, 1, d), lambda i, j: (i, 0, j))],
        out_specs=pl.BlockSpec((None, rows, d), lambda i, j: (i, 0, j)),
        out_shape=jax.ShapeDtypeStruct((depth, rows, n6), F32),
        compiler_params=_cparams(("arbitrary", "arbitrary")),
        name="mod_vectors",
    )(c_pad, mod_w, mod_b.reshape(depth, 1, n6))
    return out[:, :bsz, :]


def _attn_proj_kernel(x_ref, g_ref, sh_ref, sc_ref, wqkv_ref, wih_ref, wil_ref,
                      q_ref, k_ref, v_ref, qcat_ref, kcat_ref, wi_ref, *, d):
    h = _norm_mod(x_ref[0], g_ref[...], sh_ref[0], sc_ref[0])
    hb = h.astype(BF16)
    hl = (h - hb.astype(F32)).astype(BF16)
    qkv = jnp.dot(hb, wqkv_ref[...], preferred_element_type=F32)
    q_ref[0] = (qkv[:, :d] * (HEAD_DIM ** -0.5 * LOG2E)).astype(BF16)
    kb = qkv[:, d:2 * d].astype(BF16)
    vb = qkv[:, 2 * d:3 * d].astype(BF16)
    tm = kb.shape[0]
    j = pl.program_id(1) * tm + lax.broadcasted_iota(I32, (tm, LANES), 0)
    ln = lax.broadcasted_iota(I32, (tm, LANES), 1)
    j_hi = ((j // CHUNK) * CHUNK).astype(F32)
    j_lo = (j % CHUNK).astype(F32)
    kx = jnp.where(ln < 2, 1.0, jnp.where(ln < 5, j_hi, jnp.where(ln < 8, j_lo, 0.0))).astype(BF16)
    lo_half = ln < HEAD_DIM
    ones = jnp.ones((tm, LANES), BF16)
    for hp in range(N_HEADS // 2):
        k_ref[0, :, 2 * hp * LANES:(2 * hp + 1) * LANES] = kb[:, hp * LANES:(hp + 1) * LANES]
        k_ref[0, :, (2 * hp + 1) * LANES:(2 * hp + 2) * LANES] = kx
        v2 = vb[:, hp * LANES:(hp + 1) * LANES]
        v_ref[0, :, 2 * hp * LANES:(2 * hp + 1) * LANES] = jnp.where(lo_half, v2, ones)
        v_ref[0, :, (2 * hp + 1) * LANES:(2 * hp + 2) * LANES] = jnp.where(lo_half, ones, v2)
    wih = wih_ref[...]
    idx = (jnp.dot(hb, wih, preferred_element_type=F32)
           + jnp.dot(hl, wih, preferred_element_type=F32)
           + jnp.dot(hb, wil_ref[...], preferred_element_type=F32))
    nq = IDX_HEADS * LANES
    qi2 = idx[:, :nq] * (IDX_DIM ** -0.5)
    qhi = qi2.astype(BF16)
    qlo = (qi2 - qhi.astype(F32)).astype(BF16)
    lane = lax.broadcasted_iota(I32, qi2.shape, 1)
    sel = jnp.where((lane & (LANES - 1)) < IDX_DIM, qhi, qlo)
    for hh in range(IDX_HEADS):
        piece = sel[:, hh * LANES:(hh + 1) * LANES]
        qcat_ref[0, :, 2 * hh * LANES:(2 * hh + 1) * LANES] = piece
        qcat_ref[0, :, (2 * hh + 1) * LANES:(2 * hh + 2) * LANES] = piece
    ki2 = idx[:, nq:nq + LANES]
    khi = ki2.astype(BF16)
    klo = (ki2 - khi.astype(F32)).astype(BF16)
    kcat_ref[0, :, :LANES] = khi
    kcat_ref[0, :, LANES:] = klo
    wi_ref[0] = idx[:, nq + LANES:] * (IDX_HEADS ** -0.5)


def _attn_proj(x, g, sh, sc, w_in, *, tm):
    bsz, seq, d = x.shape
    nq = IDX_HEADS * IDX_DIM
    wqkv = w_in[:, :3 * d].astype(BF16)
    wq_i = w_in[:, 3 * d:3 * d + nq].reshape(d, IDX_HEADS, IDX_DIM)
    wq_i = jnp.concatenate([wq_i, wq_i], axis=-1).reshape(d, IDX_HEADS * LANES)
    wk_i = w_in[:, 3 * d + nq:3 * d + nq + IDX_DIM]
    wk_i = jnp.concatenate([wk_i, wk_i], axis=-1)
    ww_i = w_in[:, 3 * d + nq + IDX_DIM:]
    ww_i = jnp.pad(ww_i, ((0, 0), (0, LANES - ww_i.shape[1])))
    w_idx = jnp.concatenate([wq_i, wk_i, ww_i], axis=-1)
    w_hi = w_idx.astype(BF16)
    w_lo = (w_idx - w_hi.astype(F32)).astype(BF16)
    n_idx = w_idx.shape[1]
    row = lambda b, i: (b, i, 0)
    const = lambda b, i: (0, 0)
    per_b = lambda b, i: (b, 0, 0)
    return pl.pallas_call(
        functools.partial(_attn_proj_kernel, d=d),
        grid=(bsz, seq // tm),
        in_specs=[pl.BlockSpec((1, tm, d), row),
                  pl.BlockSpec((1, d), const),
                  pl.BlockSpec((1, 1, d), per_b),
                  pl.BlockSpec((1, 1, d), per_b),
                  pl.BlockSpec((d, 3 * d), const),
                  pl.BlockSpec((d, n_idx), const),
                  pl.BlockSpec((d, n_idx), const)],
        out_specs=[pl.BlockSpec((1, tm, d), row),
                   pl.BlockSpec((1, tm, 2 * d), row),
                   pl.BlockSpec((1, tm, 2 * d), row),
                   pl.BlockSpec((1, tm, 2 * IDX_HEADS * LANES), row),
                   pl.BlockSpec((1, tm, 2 * LANES), row),
                   pl.BlockSpec((1, tm, LANES), row)],
        out_shape=[jax.ShapeDtypeStruct((bsz, seq, d), BF16),
                   jax.ShapeDtypeStruct((bsz, seq, 2 * d), BF16),
                   jax.ShapeDtypeStruct((bsz, seq, 2 * d), BF16),
                   jax.ShapeDtypeStruct((bsz, seq, 2 * IDX_HEADS * LANES), BF16),
                   jax.ShapeDtypeStruct((bsz, seq, 2 * LANES), BF16),
                   jax.ShapeDtypeStruct((bsz, seq, LANES), F32)],
        compiler_params=_cparams(("arbitrary", "arbitrary")),
        name="attn_proj",
    )(x, g, sh, sc, wqkv, w_hi, w_lo)


def _topk_mask_kernel(qcat_ref, kcat_ref, wi_ref, bits_ref, hi_ref, lo_ref, pk_ref,
                      *, tq, n_keep, cr, unroll, tie_bits):
    qi = pl.program_id(1)
    n_adm = (qi + 1) * tq
    t_chunk = (qi * tq + lax.broadcasted_iota(I32, (1, tq), 1)) // CHUNK
    w_t = wi_ref[0].T

    def score_body(kt, carry):
        r0 = pl.multiple_of(kt * tq, tq)
        kc = kcat_ref[0, pl.ds(r0, tq), :]
        sc = jnp.zeros((tq, tq), F32)
        for hh in range(IDX_HEADS):
            lg = lax.dot_general(kc, qcat_ref[0, :, 2 * hh * LANES:(2 * hh + 2) * LANES],
                                 _NT, preferred_element_type=F32)
            sc = sc + w_t[hh:hh + 1, :] * jnp.maximum(lg, 0.0)
        sc = sc + 0.0
        raw = lax.bitcast_convert_type(sc, I32)
        key = raw ^ ((raw >> 31) & 0x7FFFFFFF)
        j_chunk = (r0 + lax.broadcasted_iota(I32, (tq, 1), 0)) // CHUNK
        key = jnp.where(j_chunk <= t_chunk, key, KEY_NEG_INF)
        hi_ref[pl.ds(r0, tq), :] = (key >> 16).astype(I16)
        lo_ref[pl.ds(r0, tq), :] = ((key & 0xFFFF) - 2 ** 15).astype(I16)
        return carry

    lax.fori_loop(0, qi + 1, score_body, 0)

    n_ch = n_adm // (cr * unroll)
    one, zero = jnp.int16(1), jnp.int16(0)

    def count(pred):
        def body(c, acc):
            for uu in range(unroll):
                r0 = pl.multiple_of((c * unroll + uu) * cr, cr)
                acc = acc + jnp.where(pred(r0), one, zero)
            return acc
        acc = lax.fori_loop(0, n_ch, body, jnp.zeros((cr, tq), I16))
        return jnp.sum(acc.astype(I32), axis=0, keepdims=True)

    def hi_at(r0):
        return hi_ref[pl.ds(r0, cr), :]

    def lo_at(r0):
        return lo_ref[pl.ds(r0, cr), :]

    def bcast16(v):
        return jnp.broadcast_to(v, (cr, tq)).astype(I16)

    def hi_round(i, carry):
        thr, c_thr = carry
        cand = thr + jnp.left_shift(jnp.int32(1), 15 - i)
        cand16 = bcast16(cand)
        cnt = count(lambda r0: hi_at(r0) >= cand16)
        ok = cnt >= n_keep
        return jnp.where(ok, cand, thr), jnp.where(ok, cnt, c_thr)

    thr_hi, c_ge_hi = lax.fori_loop(
        0, 16, hi_round,
        (jnp.full((1, tq), -2 ** 15, I32), jnp.full((1, tq), n_adm, I32)))
    thr_hi16 = bcast16(thr_hi)
    c_above = count(lambda r0: hi_at(r0) > thr_hi16)

    def bucket_body(c, carry):
        r0 = pl.multiple_of(c * cr, cr)
        lo_ref[pl.ds(r0, cr), :] = jnp.where(hi_at(r0) == thr_hi16, lo_at(r0),
                                             jnp.int16(-2 ** 15))
        return carry

    lax.fori_loop(0, n_adm // cr, bucket_body, 0)

    def lo_round(carry):
        i, thr, c_thr = carry
        cand = thr + jnp.left_shift(jnp.int32(1), 15 - i)
        cand16 = bcast16(cand)
        cnt = c_above + count(lambda r0: lo_at(r0) >= cand16)
        ok = cnt >= n_keep
        return i + 1, jnp.where(ok, cand, thr), jnp.where(ok, cnt, c_thr)

    _, thr_lo, c_thr = lax.while_loop(
        lambda carry: (carry[0] < 16) & (jnp.max(carry[2]) > n_keep), lo_round,
        (jnp.int32(0), jnp.full((1, tq), -2 ** 15, I32), c_ge_hi))
    thr_lo16 = bcast16(thr_lo)

    need_ties = jnp.max(c_thr) > n_keep
    n_tied_ok = n_keep - (c_above + count(lambda r0: lo_at(r0) > thr_lo16))

    def tied(r0):
        return (lo_at(r0) == thr_lo16) & (hi_at(r0) == thr_hi16)

    def tie_round(i, cut):
        cand = cut + jnp.left_shift(jnp.int32(1), tie_bits - 1 - i)
        cand16 = bcast16(cand)
        cnt = count(lambda r0: tied(r0)
                    & ((r0 + lax.broadcasted_iota(I32, (cr, tq), 0)).astype(I16) < cand16))
        return jnp.where(cnt <= n_tied_ok, cand, cut)

    cut = lax.fori_loop(0, jnp.where(need_ties, tie_bits, 0), tie_round,
                        jnp.zeros((1, tq), I32))
    cut = jnp.where(need_ties, cut, jnp.int32(2 ** 15 - 1))
    cut16 = bcast16(cut)

    pk_ref[...] = jnp.zeros(pk_ref.shape, I32)
    n_sub = LANES // cr

    def pack_body(c, carry):
        pieces = []
        for uu in range(n_sub):
            r0 = pl.multiple_of(c * LANES + uu * cr, cr)
            j16 = (r0 + lax.broadcasted_iota(I32, (cr, tq), 0)).astype(I16)
            sel = ((hi_at(r0) > thr_hi16) | (lo_at(r0) > thr_lo16)
                   | (tied(r0) & (j16 < cut16)))
            pieces.append(jnp.where(sel, one, zero).astype(I32))
        sel32 = jnp.concatenate(pieces, axis=0)
        j = c * LANES + lax.broadcasted_iota(I32, (LANES, 1), 0)
        sel32 = jnp.where((j // CHUNK) <= t_chunk, sel32, 0)
        w0 = pl.multiple_of((c // MASK_BITS) * LANES, LANES)
        pk_ref[pl.ds(w0, LANES), :] = pk_ref[pl.ds(w0, LANES), :] | jnp.left_shift(
            sel32, c % MASK_BITS)
        return carry

    lax.fori_loop(0, n_adm // LANES, pack_body, 0)
    bits_ref[0] = pk_ref[...].T


def _topk_mask(qcat, kcat, wi, *, tq, n_keep):
    bsz, seq, _ = qcat.shape
    assert seq < 2 ** 15, "key indices are compared as int16"
    n_words = -(-seq // (MASK_BITS * LANES))
    tie_bits = max(1, int(seq).bit_length())
    return pl.pallas_call(
        functools.partial(_topk_mask_kernel, tq=tq, n_keep=n_keep, cr=64, unroll=tq // 64,
                          tie_bits=tie_bits),
        grid=(bsz, seq // tq),
        in_specs=[pl.BlockSpec((1, tq, qcat.shape[2]), lambda b, i: (b, i, 0)),
                  pl.BlockSpec((1, seq, kcat.shape[2]), lambda b, i: (b, 0, 0)),
                  pl.BlockSpec((1, tq, LANES), lambda b, i: (b, i, 0))],
        out_specs=pl.BlockSpec((1, tq, n_words * LANES), lambda b, i: (b, i, 0)),
        out_shape=jax.ShapeDtypeStruct((bsz, seq, n_words * LANES), I32),
        scratch_shapes=[pltpu.VMEM((seq, tq), I16),
                        pltpu.VMEM((seq, tq), I16),
                        pltpu.VMEM((n_words * LANES, tq), I32)],
        compiler_params=_cparams(("arbitrary", "arbitrary")),
        name="topk_mask",
    )(qcat, kcat, wi)


def _alibi_q_lanes(t_f, lane, slope2):
    a = -slope2 * t_f
    a_hi = a.astype(BF16).astype(F32)
    s1, s2, s3 = _bf16_pieces(slope2)
    val = jnp.where(lane == 0, a_hi, jnp.where(lane == 1, a - a_hi, 0.0))
    val = jnp.where((lane == 2) | (lane == 5), s1, val)
    val = jnp.where((lane == 3) | (lane == 6), s2, val)
    val = jnp.where((lane == 4) | (lane == 7), s3, val)
    return val.astype(BF16)


def _attn_kernel(q_ref, k_ref, v_ref, bits_ref, o_ref, qx_ref, m_ref, acc_ref, mb_ref,
                 corr_ref, *, tq, tk, nk):
    qi = pl.program_id(1)
    kt = pl.program_id(2)
    last = ((qi + 1) * tq - 1) // tk
    n_sub = tk // LANES
    lane = lax.broadcasted_iota(I32, (tq, LANES), 1)
    lo_half = lane < HEAD_DIM

    @pl.when(kt == 0)
    def _init():
        m_ref[...] = jnp.full(m_ref.shape, M_INIT, F32)
        acc_ref[...] = jnp.zeros(acc_ref.shape, F32)
        t_f = (qi * tq + lax.broadcasted_iota(I32, (tq, LANES), 0)).astype(F32)
        for h in range(N_HEADS):
            q2 = q_ref[0, :, (h // 2) * LANES:(h // 2 + 1) * LANES]
            zero = jnp.zeros_like(q2)
            rows = slice((h % 2) * tq, (h % 2 + 1) * tq)
            qx_ref[h // 2, rows, :LANES] = (jnp.where(lo_half, q2, zero) if h % 2 == 0
                                            else jnp.where(lo_half, zero, q2))
            qx_ref[h // 2, rows, LANES:] = _alibi_q_lanes(t_f, lane, _alibi_slope2(h))

    def step(diag):
        bits = bits_ref[0]
        sel = jnp.concatenate(
            [(bits >> ((kt * n_sub + i) % MASK_BITS)) & 1 for i in range(n_sub)], axis=1)
        mb_ref[...] = jnp.where(sel == 1, 0.0, MASK_BIAS)
        if diag:
            t = qi * tq + lax.broadcasted_iota(I32, (tq, 1), 0)
            j = kt * tk + lax.broadcasted_iota(I32, (1, tk), 1)
            corr_ref[...] = (2 * jnp.maximum(j - t, 0)).astype(F32)
        for h in range(N_HEADS):
            hp = h // 2
            if h % 2 == 0:
                s_pair = lax.dot_general(qx_ref[hp],
                                         k_ref[0, :, 2 * hp * LANES:(2 * hp + 2) * LANES],
                                         _NT, preferred_element_type=F32)
            s = s_pair[(h % 2) * tq:(h % 2 + 1) * tq]
            if diag:
                s = s + (mb_ref[...] - _alibi_slope2(h) * corr_ref[...])
            else:
                s = s + mb_ref[...]
            m_old = m_ref[h]
            m_new = jnp.maximum(m_old, jnp.max(s, axis=1, keepdims=True))
            p = jnp.exp2(s - jnp.tile(m_new, (1, n_sub)))
            m_ref[h] = m_new
            acc_ref[h] = (jnp.exp2(m_old - m_new) * acc_ref[h]
                          + jnp.dot(p.astype(BF16), v_ref[0, :, h * LANES:(h + 1) * LANES],
                                    preferred_element_type=F32))

    pl.when(kt < last)(functools.partial(step, False))
    pl.when(kt == last)(functools.partial(step, True))

    @pl.when(kt == nk - 1)
    def _finish():
        for hp in range(N_HEADS // 2):
            a0 = acc_ref[2 * hp]
            a1 = acc_ref[2 * hp + 1]
            o0 = a0 / pltpu.roll(a0, HEAD_DIM, axis=1)
            o1 = a1 / pltpu.roll(a1, HEAD_DIM, axis=1)
            o_ref[0, :, hp * LANES:(hp + 1) * LANES] = jnp.where(lo_half, o0, o1).astype(o_ref.dtype)


def _attention(q, k_ext, v_ext, bits, *, tq, tk):
    bsz, seq, d = q.shape
    nk = seq // tk
    n_sub = tk // LANES

    def last_kt(i):
        return ((i + 1) * tq - 1) // tk

    kv_map = lambda b, i, t: (b, jnp.minimum(t, last_kt(i)), 0)
    bits_map = lambda b, i, t: (b, i, (jnp.minimum(t, last_kt(i)) * n_sub) // MASK_BITS)
    return pl.pallas_call(
        functools.partial(_attn_kernel, tq=tq, tk=tk, nk=nk),
        grid=(bsz, seq // tq, nk),
        in_specs=[pl.BlockSpec((1, tq, d), lambda b, i, t: (b, i, 0)),
                  pl.BlockSpec((1, tk, 2 * d), kv_map),
                  pl.BlockSpec((1, tk, 2 * d), kv_map),
                  pl.BlockSpec((1, tq, LANES), bits_map)],
        out_specs=pl.BlockSpec((1, tq, d), lambda b, i, t: (b, i, 0)),
        out_shape=jax.ShapeDtypeStruct((bsz, seq, d), BF16),
        scratch_shapes=[pltpu.VMEM((N_HEADS // 2, 2 * tq, 2 * LANES), BF16),
                        pltpu.VMEM((N_HEADS, tq, LANES), F32),
                        pltpu.VMEM((N_HEADS, tq, LANES), F32),
                        pltpu.VMEM((tq, tk), F32),
                        pltpu.VMEM((tq, tk), F32)],
        compiler_params=_cparams(("arbitrary", "arbitrary", "arbitrary")),
        name="masked_attention",
    )(q, k_ext, v_ext, bits)


def _proj_residual_kernel(y_ref, w_ref, x_ref, gate_ref, o_ref, *, glu, d):
    r = jnp.dot(y_ref[0], w_ref[...], preferred_element_type=F32)
    if glu:
        val, gt = r[:, :d], r[:, d:]
        r = val * (1.0 / (1.0 + jnp.exp(-gt)))
    o_ref[0] = x_ref[0] + gate_ref[0] * r


def _proj_residual(y, w, x, gate, *, tm, glu, name):
    bsz, seq, d = x.shape
    kdim, n = w.shape
    row = lambda b, i: (b, i, 0)
    return pl.pallas_call(
        functools.partial(_proj_residual_kernel, glu=glu, d=d),
        grid=(bsz, seq // tm),
        in_specs=[pl.BlockSpec((1, tm, kdim), row),
                  pl.BlockSpec((kdim, n), lambda b, i: (0, 0)),
                  pl.BlockSpec((1, tm, d), row),
                  pl.BlockSpec((1, 1, d), lambda b, i: (b, 0, 0))],
        out_specs=pl.BlockSpec((1, tm, d), row),
        out_shape=jax.ShapeDtypeStruct((bsz, seq, d), F32),
        compiler_params=_cparams(("arbitrary", "arbitrary")),
        name=name,
    )(y, w.astype(BF16), x, gate)


def _ffn_up_kernel(x_ref, g_ref, sh_ref, sc_ref, wv_ref, wg_ref, cwv_ref, cwg_ref,
                   cbv_ref, cbg_ref, a_ref, zv_ref, zg_ref, *, tm):
    i = pl.program_id(2)
    halo = SUBLANES

    @pl.when(i == 0)
    def _seq_start():
        zv_ref[0:halo, :] = jnp.zeros((halo, zv_ref.shape[1]), F32)
        zg_ref[0:halo, :] = jnp.zeros((halo, zg_ref.shape[1]), F32)

    h = _norm_mod(x_ref[0], g_ref[...], sh_ref[0], sc_ref[0]).astype(BF16)
    zv_ref[halo:halo + tm, :] = jnp.dot(h, wv_ref[...], preferred_element_type=F32)
    zg_ref[halo:halo + tm, :] = jnp.dot(h, wg_ref[...], preferred_element_type=F32)

    def conv(z_ref, cw_ref, cb_ref):
        out = cb_ref[...]
        for tap in range(CONV_WIDTH):
            off = halo - (CONV_WIDTH - 1) + tap
            out = out + cw_ref[tap:tap + 1, :] * z_ref[off:off + tm, :]
        return out

    val = conv(zv_ref, cwv_ref, cbv_ref)
    gt = conv(zg_ref, cwg_ref, cbg_ref)
    a_ref[0] = (gt * (1.0 / (1.0 + jnp.exp(-gt))) * val).astype(a_ref.dtype)
    zv_ref[0:halo, :] = zv_ref[tm:tm + halo, :]
    zg_ref[0:halo, :] = zg_ref[tm:tm + halo, :]


def _ffn_up(x, g, sh, sc, w_up, conv_w, conv_b, *, tm, tn):
    bsz, seq, d = x.shape
    dff = w_up.shape[1] // 2
    nj = dff // tn
    wv = w_up[:, :dff].astype(BF16)
    wg = w_up[:, dff:].astype(BF16)
    cwv = jnp.pad(conv_w[:, :dff], ((0, SUBLANES - CONV_WIDTH), (0, 0)))
    cwg = jnp.pad(conv_w[:, dff:], ((0, SUBLANES - CONV_WIDTH), (0, 0)))
    cbv = conv_b[:dff].reshape(1, dff)
    cbg = conv_b[dff:].reshape(1, dff)
    col = lambda j, b, i: (0, j)
    return pl.pallas_call(
        functools.partial(_ffn_up_kernel, tm=tm),
        grid=(nj, bsz, seq // tm),
        in_specs=[pl.BlockSpec((1, tm, d), lambda j, b, i: (b, i, 0)),
                  pl.BlockSpec((1, d), lambda j, b, i: (0, 0)),
                  pl.BlockSpec((1, 1, d), lambda j, b, i: (b, 0, 0)),
                  pl.BlockSpec((1, 1, d), lambda j, b, i: (b, 0, 0)),
                  pl.BlockSpec((d, tn), col),
                  pl.BlockSpec((d, tn), col),
                  pl.BlockSpec((SUBLANES, tn), col),
                  pl.BlockSpec((SUBLANES, tn), col),
                  pl.BlockSpec((1, tn), col),
                  pl.BlockSpec((1, tn), col)],
        out_specs=pl.BlockSpec((1, tm, tn), lambda j, b, i: (b, i, j)),
        out_shape=jax.ShapeDtypeStruct((bsz, seq, dff), BF16),
        scratch_shapes=[pltpu.VMEM((tm + SUBLANES, tn), F32),
                        pltpu.VMEM((tm + SUBLANES, tn), F32)],
        compiler_params=_cparams(("arbitrary", "arbitrary", "arbitrary")),
        name="ffn_up",
    )(x, g, sh, sc, wv, wg, cwv, cwg, cbv, cbg)


def _ffn_down_kernel(a_ref, w_ref, x_ref, gate_ref, fg_ref, o_ref, *, final):
    r = jnp.dot(a_ref[0], w_ref[...], preferred_element_type=F32)
    out = x_ref[0] + gate_ref[0] * r
    if final:
        out = _rms(out, fg_ref[...])
    o_ref[0] = out


def _ffn_down(a, w_down, x, gate, final_g, *, tm, final):
    bsz, seq, d = x.shape
    dff = a.shape[2]
    row = lambda b, i: (b, i, 0)
    return pl.pallas_call(
        functools.partial(_ffn_down_kernel, final=final),
        grid=(bsz, seq // tm),
        in_specs=[pl.BlockSpec((1, tm, dff), row),
                  pl.BlockSpec((dff, d), lambda b, i: (0, 0)),
                  pl.BlockSpec((1, tm, d), row),
                  pl.BlockSpec((1, 1, d), lambda b, i: (b, 0, 0)),
                  pl.BlockSpec((1, d), lambda b, i: (0, 0))],
        out_specs=pl.BlockSpec((1, tm, d), row),
        out_shape=jax.ShapeDtypeStruct((bsz, seq, d), F32),
        compiler_params=_cparams(("arbitrary", "arbitrary")),
        name="ffn_down_final" if final else "ffn_down",
    )(a, w_down.astype(BF16), x, gate, final_g)


def _ssm_in_kernel(x_ref, g_ref, sh_ref, sc_ref, w_ref, u_ref):
    h = _norm_mod(x_ref[0], g_ref[...], sh_ref[0], sc_ref[0]).astype(BF16)
    u_ref[0] = jnp.dot(h, w_ref[...], preferred_element_type=F32)


def _ssm_in(x, g, sh, sc, w, *, tm):
    bsz, seq, d = x.shape
    n = w.shape[1]
    row = lambda b, i: (b, i, 0)
    return pl.pallas_call(
        _ssm_in_kernel,
        grid=(bsz, seq // tm),
        in_specs=[pl.BlockSpec((1, tm, d), row),
                  pl.BlockSpec((1, d), lambda b, i: (0, 0)),
                  pl.BlockSpec((1, 1, d), lambda b, i: (b, 0, 0)),
                  pl.BlockSpec((1, 1, d), lambda b, i: (b, 0, 0)),
                  pl.BlockSpec((d, n), lambda b, i: (0, 0))],
        out_specs=pl.BlockSpec((1, tm, n), row),
        out_shape=jax.ShapeDtypeStruct((bsz, seq, n), F32),
        compiler_params=_cparams(("arbitrary", "arbitrary")),
        name="ssm_in",
    )(x, g, sh, sc, w.astype(BF16))


def _ssm_scan_kernel(u_ref, bbt_ref, apow_ref, ppow_ref, cc_ref, dskip_ref, y_ref,
                     carry_ref, *, tt, ns, n_steps):
    i = pl.program_id(2)
    ng = tt // SUBLANES

    @pl.when(i == 0)
    def _seq_start():
        carry_ref[...] = jnp.zeros(carry_ref.shape, F32)

    def cmul_add(x_re, x_im, a_re, a_im, y_re, y_im):
        return x_re + a_re * y_re - a_im * y_im, x_im + a_re * y_im + a_im * y_re

    u = u_ref[0]
    bu = jnp.dot(u.astype(BF16), bbt_ref[0], preferred_element_type=F32)
    loc_re = bu[:, :ns].reshape(ng, SUBLANES, ns)
    loc_im = bu[:, ns:].reshape(ng, SUBLANES, ns)
    for kk in range(n_steps - 1):
        loc_re, loc_im = cmul_add(loc_re, loc_im, apow_ref[0, kk, 0], apow_ref[0, kk, 1],
                                  pltpu.roll(loc_re, 1 << kk, axis=1),
                                  pltpu.roll(loc_im, 1 << kk, axis=1))
    last_re = jnp.broadcast_to(loc_re[:, SUBLANES - 1:, :], (ng, SUBLANES, ns))
    last_im = jnp.broadcast_to(loc_im[:, SUBLANES - 1:, :], (ng, SUBLANES, ns))
    a8_re = apow_ref[0, n_steps - 1, 0]
    a8_im = apow_ref[0, n_steps - 1, 1]
    c_re, c_im = carry_ref[0], carry_ref[1]
    cin_re, cin_im = [], []
    for r in range(ng):
        cin_re.append(c_re)
        cin_im.append(c_im)
        c_re, c_im = cmul_add(last_re[r], last_im[r], a8_re, a8_im, c_re, c_im)
    carry_ref[0] = c_re
    carry_ref[1] = c_im
    hg_re, hg_im = cmul_add(loc_re, loc_im, ppow_ref[0, 0], ppow_ref[0, 1],
                            jnp.stack(cin_re), jnp.stack(cin_im))
    hcat = jnp.concatenate([hg_re.reshape(tt, ns), hg_im.reshape(tt, ns)], axis=1).astype(BF16)
    y = jnp.dot(hcat, cc_ref[0], preferred_element_type=F32) + dskip_ref[0] * u
    y_ref[0] = jax.nn.gelu(y).astype(y_ref.dtype)


def _ssm_scan(u, lam_re, lam_im, log_dt, b_re, b_im, c_re, c_im, d_skip, *, tt):
    bsz, seq, width = u.shape
    n_groups, n_state = lam_re.shape
    gs = SSM_GROUP
    gpb = LANES // gs
    n_gb = n_groups // gpb
    ns = gpb * n_state
    n_steps = int(math.log2(SUBLANES)) + 1
    dt = jnp.exp(log_dt)[:, None]
    mag = jnp.exp(lam_re * dt)
    a_re = mag * jnp.cos(lam_im * dt)
    a_im = mag * jnp.sin(lam_im * dt)
    den = lam_re * lam_re + lam_im * lam_im
    coef_re = ((a_re - 1.0) * lam_re + a_im * lam_im) / den
    coef_im = (a_im * lam_re - (a_re - 1.0) * lam_im) / den
    bb_re = coef_re[..., None] * b_re - coef_im[..., None] * b_im
    bb_im = coef_re[..., None] * b_im + coef_im[..., None] * b_re
    eye = jnp.eye(gpb, dtype=F32)

    def block_diag_in(bb):
        t = bb.reshape(n_gb, gpb, n_state, gs)
        return jnp.einsum('ngpc,gh->ngchp', t, eye).reshape(n_gb, gpb * gs, gpb * n_state)

    def block_diag_out(cm):
        t = cm.reshape(n_gb, gpb, gs, n_state)
        return jnp.einsum('ngcp,gh->ngphc', t, eye).reshape(n_gb, gpb * n_state, gpb * gs)

    bbt = jnp.concatenate([block_diag_in(bb_re), block_diag_in(bb_im)], axis=2).astype(BF16)
    cc = jnp.concatenate([block_diag_out(c_re), -block_diag_out(c_im)], axis=1).astype(BF16)

    def powers(e):
        ang = lam_im[None] * dt[None] * e[:, None, None]
        mg = jnp.exp(lam_re[None] * dt[None] * e[:, None, None])
        pw = jnp.stack([mg * jnp.cos(ang), mg * jnp.sin(ang)], axis=1)
        pw = pw.reshape(e.shape[0], 2, n_gb, ns)
        return jnp.transpose(pw, (2, 0, 1, 3))

    apow = powers(2.0 ** jnp.arange(n_steps, dtype=F32))
    shift = (2 ** jnp.arange(n_steps))[:, None]
    live = (jnp.arange(SUBLANES)[None, :] >= shift) | (shift >= SUBLANES)
    apow = apow[:, :, :, None, :] * live[None, :, None, :, None].astype(F32)
    ppow = jnp.transpose(powers(jnp.arange(1, SUBLANES + 1, dtype=F32)), (0, 2, 1, 3))
    dsk = d_skip.reshape(n_gb, 1, gpb * gs)
    return pl.pallas_call(
        functools.partial(_ssm_scan_kernel, tt=tt, ns=ns, n_steps=n_steps),
        grid=(bsz, n_gb, seq // tt),
        in_specs=[pl.BlockSpec((1, tt, LANES), lambda b, g, i: (b, i, g)),
                  pl.BlockSpec((1, LANES, 2 * ns), lambda b, g, i: (g, 0, 0)),
                  pl.BlockSpec((1, n_steps, 2, SUBLANES, ns), lambda b, g, i: (g, 0, 0, 0, 0)),
                  pl.BlockSpec((1, 2, SUBLANES, ns), lambda b, g, i: (g, 0, 0, 0)),
                  pl.BlockSpec((1, 2 * ns, LANES), lambda b, g, i: (g, 0, 0)),
                  pl.BlockSpec((1, 1, LANES), lambda b, g, i: (g, 0, 0))],
        out_specs=pl.BlockSpec((1, tt, LANES), lambda b, g, i: (b, i, g)),
        out_shape=jax.ShapeDtypeStruct((bsz, seq, width), BF16),
        scratch_shapes=[pltpu.VMEM((2, SUBLANES, ns), F32)],
        compiler_params=_cparams(("arbitrary", "arbitrary", "arbitrary")),
        name="ssm_scan",
    )(u, bbt, apow, ppow, cc, dsk)


def kernel(x, c, mod_w, mod_b, norm_mix_g, norm_ffn_g, attn_w_in, attn_w_out, ssm_w_in,
           ssm_lam_re, ssm_lam_im, ssm_log_dt, ssm_b_re, ssm_b_im, ssm_c_re, ssm_c_im,
           ssm_d, ssm_w_glu, ffn_w_up, ffn_conv_w, ffn_conv_b, ffn_w_down, final_g):
    bsz, seq, d = x.shape
    depth = mod_w.shape[0]
    n_keep = min(TOPK_MAX, seq // 4)
    tq = 256
    tk = min(512, seq)
    tm = min(512, seq)
    dff = ffn_w_down.shape[1]
    tn = dff // 2

    mod = _mod_vectors(c, mod_w, mod_b)
    fg = final_g.reshape(1, d)
    for i in range(depth):
        sh1, sc1, g1, sh2, sc2, g2 = [mod[i, :, None, k * d:(k + 1) * d] for k in range(6)]
        gm = norm_mix_g[i].reshape(1, d)
        gf = norm_ffn_g[i].reshape(1, d)
        j = i // 2
        if i % 2 == 0:
            q, k, v, qcat, kcat, wi = _attn_proj(x, gm, sh1, sc1, attn_w_in[j], tm=min(256, seq))
            bits = _topk_mask(qcat, kcat, wi, tq=tq, n_keep=n_keep)
            att = _attention(q, k, v, bits, tq=min(512, seq), tk=tk)
            x = _proj_residual(att, attn_w_out[j], x, g1, tm=tm, glu=False, name="attn_out")
        else:
            u = _ssm_in(x, gm, sh1, sc1, ssm_w_in[j], tm=tm)
            y = _ssm_scan(u, ssm_lam_re[j], ssm_lam_im[j], ssm_log_dt[j], ssm_b_re[j],
                          ssm_b_im[j], ssm_c_re[j], ssm_c_im[j], ssm_d[j], tt=min(256, seq))
            x = _proj_residual(y, ssm_w_glu[j], x, g1, tm=tm, glu=True, name="ssm_glu")
        a = _ffn_up(x, gf, sh2, sc2, ffn_w_up[i], ffn_conv_w[i], ffn_conv_b[i], tm=tm, tn=tn)
        x = _ffn_down(a, ffn_w_down[i], x, g2, fg, tm=tm, final=(i == depth - 1))
    return x
```

```python
import functools
import math

import jax
import jax.numpy as jnp
import numpy as np
from jax import lax
from jax.experimental import pallas as pl
from jax.experimental.pallas import tpu as pltpu

F32 = jnp.float32
BF16 = jnp.bfloat16
I32 = jnp.int32
I16 = jnp.int16

EPS = 1e-6
CHUNK = 64
N_HEADS = 16
HEAD_DIM = 64
IDX_HEADS = 8
IDX_DIM = 64
TOPK_MAX = 256
SSM_GROUP = 16
SSM_STATE = 64
CONV_WIDTH = 3

LANES = 128
SUBLANES = 8
MASK_BITS = 32
VMEM_LIMIT = 56 * 1024 * 1024

INT_MIN = -2 ** 31
KEY_NEG_INF = INT_MIN + 0x7FFFFF
MASK_BIAS = -1e30
M_INIT = -3e38
LOG2E = math.log2(math.e)


def _bf16_pieces(x, n=3):
    out, rest = [], np.float32(x)
    for _ in range(n):
        piece = np.float32(np.asarray(rest, np.float32).astype(BF16).astype(np.float32))
        out.append(float(piece))
        rest = np.float32(rest - piece)
    return out


def _alibi_slope2(h):
    return 2.0 ** (-8.0 * (h + 1) / N_HEADS) * LOG2E

_NT = (((1,), (1,)), ((), ()))


def _cparams(sem):
    return pltpu.CompilerParams(dimension_semantics=sem, vmem_limit_bytes=VMEM_LIMIT)


def _norm_mod(x, g, sh, sc):
    ms = jnp.mean(x * x, axis=-1, keepdims=True)
    y = x * lax.rsqrt(ms + EPS)
    return (y * g) * (1.0 + sc) + sh


def _rms(x, g):
    ms = jnp.mean(x * x, axis=-1, keepdims=True)
    return (x * lax.rsqrt(ms + EPS)) * g


def _mod_kernel(c_ref, w_ref, b_ref, o_ref):
    c = c_ref[...]
    cond = c * (1.0 / (1.0 + jnp.exp(-c)))
    o_ref[...] = jnp.dot(cond, w_ref[...], preferred_element_type=F32,
                         precision=lax.Precision.HIGHEST) + b_ref[...]


def _mod_vectors(c, mod_w, mod_b):
    depth, d, n6 = mod_w.shape
    bsz = c.shape[0]
    rows = SUBLANES
    c_pad = jnp.zeros((rows, d), F32).at[:bsz].set(c)
    out = pl.pallas_call(
        _mod_kernel,
        grid=(depth, n6 // d),
        in_specs=[pl.BlockSpec((rows, d), lambda i, j: (0, 0)),
                  pl.BlockSpec((None, d, d), lambda i, j: (i, 0, j)),
                  pl.BlockSpec((None, 1, d), lambda i, j: (i, 0, j))],
        out_specs=pl.BlockSpec((None, rows, d), lambda i, j: (i, 0, j)),
        out_shape=jax.ShapeDtypeStruct((depth, rows, n6), F32),
        compiler_params=_cparams(("arbitrary", "arbitrary")),
        name="mod_vectors",
    )(c_pad, mod_w, mod_b.reshape(depth, 1, n6))
    return out[:, :bsz, :]


def _attn_proj_kernel(x_ref, g_ref, sh_ref, sc_ref, wqkv_ref, wih_ref, wil_ref,
                      q_ref, k_ref, v_ref, qcat_ref, kcat_ref, wi_ref, *, d):
    h = _norm_mod(x_ref[0], g_ref[...], sh_ref[0], sc_ref[0])
    hb = h.astype(BF16)
    hl = (h - hb.astype(F32)).astype(BF16)
    qkv = jnp.dot(hb, wqkv_ref[...], preferred_element_type=F32)
    q_ref[0] = (qkv[:, :d] * (HEAD_DIM ** -0.5 * LOG2E)).astype(BF16)
    kb = qkv[:, d:2 * d].astype(BF16)
    vb = qkv[:, 2 * d:3 * d].astype(BF16)
    tm = kb.shape[0]
    j = pl.program_id(1) * tm + lax.broadcasted_iota(I32, (tm, LANES), 0)
    ln = lax.broadcasted_iota(I32, (tm, LANES), 1)
    j_hi = ((j // CHUNK) * CHUNK).astype(F32)
    j_lo = (j % CHUNK).astype(F32)
    kx = jnp.where(ln < 2, 1.0, jnp.where(ln < 5, j_hi, jnp.where(ln < 8, j_lo, 0.0))).astype(BF16)
    lo_half = ln < HEAD_DIM
    ones = jnp.ones((tm, LANES), BF16)
    for hp in range(N_HEADS // 2):
        k_ref[0, :, 2 * hp * LANES:(2 * hp + 1) * LANES] = kb[:, hp * LANES:(hp + 1) * LANES]
        k_ref[0, :, (2 * hp + 1) * LANES:(2 * hp + 2) * LANES] = kx
        v2 = vb[:, hp * LANES:(hp + 1) * LANES]
        v_ref[0, :, 2 * hp * LANES:(2 * hp + 1) * LANES] = jnp.where(lo_half, v2, ones)
        v_ref[0, :, (2 * hp + 1) * LANES:(2 * hp + 2) * LANES] = jnp.where(lo_half, ones, v2)
    wih = wih_ref[...]
    idx = (jnp.dot(hb, wih, preferred_element_type=F32)
           + jnp.dot(hl, wih, preferred_element_type=F32)
           + jnp.dot(hb, wil_ref[...], preferred_element_type=F32))
    nq = IDX_HEADS * LANES
    qi2 = idx[:, :nq] * (IDX_DIM ** -0.5)
    qhi = qi2.astype(BF16)
    qlo = (qi2 - qhi.astype(F32)).astype(BF16)
    lane = lax.broadcasted_iota(I32, qi2.shape, 1)
    sel = jnp.where((lane & (LANES - 1)) < IDX_DIM, qhi, qlo)
    for hh in range(IDX_HEADS):
        piece = sel[:, hh * LANES:(hh + 1) * LANES]
        qcat_ref[0, :, 2 * hh * LANES:(2 * hh + 1) * LANES] = piece
        qcat_ref[0, :, (2 * hh + 1) * LANES:(2 * hh + 2) * LANES] = piece
    ki2 = idx[:, nq:nq + LANES]
    khi = ki2.astype(BF16)
    klo = (ki2 - khi.astype(F32)).astype(BF16)
    kcat_ref[0, :, :LANES] = khi
    kcat_ref[0, :, LANES:] = klo
    wi_ref[0] = idx[:, nq + LANES:] * (IDX_HEADS ** -0.5)


def _attn_proj(x, g, sh, sc, w_in, *, tm):
    bsz, seq, d = x.shape
    nq = IDX_HEADS * IDX_DIM
    wqkv = w_in[:, :3 * d].astype(BF16)
    wq_i = w_in[:, 3 * d:3 * d + nq].reshape(d, IDX_HEADS, IDX_DIM)
    wq_i = jnp.concatenate([wq_i, wq_i], axis=-1).reshape(d, IDX_HEADS * LANES)
    wk_i = w_in[:, 3 * d + nq:3 * d + nq + IDX_DIM]
    wk_i = jnp.concatenate([wk_i, wk_i], axis=-1)
    ww_i = w_in[:, 3 * d + nq + IDX_DIM:]
    ww_i = jnp.pad(ww_i, ((0, 0), (0, LANES - ww_i.shape[1])))
    w_idx = jnp.concatenate([wq_i, wk_i, ww_i], axis=-1)
    w_hi = w_idx.astype(BF16)
    w_lo = (w_idx - w_hi.astype(F32)).astype(BF16)
    n_idx = w_idx.shape[1]
    row = lambda b, i: (b, i, 0)
    const = lambda b, i: (0, 0)
    per_b = lambda b, i: (b, 0, 0)
    return pl.pallas_call(
        functools.partial(_attn_proj_kernel, d=d),
        grid=(bsz, seq // tm),
        in_specs=[pl.BlockSpec((1, tm, d), row),
                  pl.BlockSpec((1, d), const),
                  pl.BlockSpec((1, 1, d), per_b),
                  pl.BlockSpec((1, 1, d), per_b),
                  pl.BlockSpec((d, 3 * d), const),
                  pl.BlockSpec((d, n_idx), const),
                  pl.BlockSpec((d, n_idx), const)],
        out_specs=[pl.BlockSpec((1, tm, d), row),
                   pl.BlockSpec((1, tm, 2 * d), row),
                   pl.BlockSpec((1, tm, 2 * d), row),
                   pl.BlockSpec((1, tm, 2 * IDX_HEADS * LANES), row),
                   pl.BlockSpec((1, tm, 2 * LANES), row),
                   pl.BlockSpec((1, tm, LANES), row)],
        out_shape=[jax.ShapeDtypeStruct((bsz, seq, d), BF16),
                   jax.ShapeDtypeStruct((bsz, seq, 2 * d), BF16),
                   jax.ShapeDtypeStruct((bsz, seq, 2 * d), BF16),
                   jax.ShapeDtypeStruct((bsz, seq, 2 * IDX_HEADS * LANES), BF16),
                   jax.ShapeDtypeStruct((bsz, seq, 2 * LANES), BF16),
                   jax.ShapeDtypeStruct((bsz, seq, LANES), F32)],
        compiler_params=_cparams(("arbitrary", "arbitrary")),
        name="attn_proj",
    )(x, g, sh, sc, wqkv, w_hi, w_lo)


def _topk_mask_kernel(qcat_ref, kcat_ref, wi_ref, bits_ref, hi_ref, lo_ref, pk_ref,
                      *, tq, tkr, n_keep, cr, unroll, tie_bits):
    qi = pl.program_id(1)
    n_adm = (qi + 1) * tq
    t_chunk = (qi * tq + lax.broadcasted_iota(I32, (1, tq), 1)) // CHUNK
    w_t = wi_ref[0].T

    def score_body(kt, carry):
        r0 = pl.multiple_of(kt * tkr, tkr)
        kc = kcat_ref[0, pl.ds(r0, tkr), :]
        sc = jnp.zeros((tkr, tq), F32)
        for hh in range(IDX_HEADS):
            lg = lax.dot_general(kc, qcat_ref[0, :, 2 * hh * LANES:(2 * hh + 2) * LANES],
                                 _NT, preferred_element_type=F32)
            sc = sc + w_t[hh:hh + 1, :] * jnp.maximum(lg, 0.0)
        sc = sc + 0.0
        raw = lax.bitcast_convert_type(sc, I32)
        key = raw ^ ((raw >> 31) & 0x7FFFFFFF)
        j_chunk = (r0 + lax.broadcasted_iota(I32, (tkr, 1), 0)) // CHUNK
        key = jnp.where(j_chunk <= t_chunk, key, KEY_NEG_INF)
        hi_ref[pl.ds(r0, tkr), :] = (key >> 16).astype(I16)
        lo_ref[pl.ds(r0, tkr), :] = ((key & 0xFFFF) - 2 ** 15).astype(I16)
        return carry

    lax.fori_loop(0, (n_adm + tkr - 1) // tkr, score_body, 0)

    n_ch = n_adm // (cr * unroll)
    one, zero = jnp.int16(1), jnp.int16(0)

    def count(pred):
        def body(c, acc):
            for uu in range(unroll):
                r0 = pl.multiple_of((c * unroll + uu) * cr, cr)
                acc = acc + jnp.where(pred(r0), one, zero)
            return acc
        acc = lax.fori_loop(0, n_ch, body, jnp.zeros((cr, tq), I16))
        return jnp.sum(acc.astype(I32), axis=0, keepdims=True)

    def hi_at(r0):
        return hi_ref[pl.ds(r0, cr), :]

    def lo_at(r0):
        return lo_ref[pl.ds(r0, cr), :]

    def bcast16(v):
        return jnp.broadcast_to(v, (cr, tq)).astype(I16)

    def hi_round(i, carry):
        thr, c_thr = carry
        cand = thr + jnp.left_shift(jnp.int32(1), 15 - i)
        cand16 = bcast16(cand)
        cnt = count(lambda r0: hi_at(r0) >= cand16)
        ok = cnt >= n_keep
        return jnp.where(ok, cand, thr), jnp.where(ok, cnt, c_thr)

    thr_hi, c_ge_hi = lax.fori_loop(
        0, 16, hi_round,
        (jnp.full((1, tq), -2 ** 15, I32), jnp.full((1, tq), n_adm, I32)))
    thr_hi16 = bcast16(thr_hi)
    c_above = count(lambda r0: hi_at(r0) > thr_hi16)

    def bucket_body(c, carry):
        r0 = pl.multiple_of(c * cr, cr)
        lo_ref[pl.ds(r0, cr), :] = jnp.where(hi_at(r0) == thr_hi16, lo_at(r0),
                                             jnp.int16(-2 ** 15))
        return carry

    lax.fori_loop(0, n_adm // cr, bucket_body, 0)

    def lo_round(i, carry):
        thr, c_thr = carry
        cand = thr + jnp.left_shift(jnp.int32(1), 15 - i)
        cand16 = bcast16(cand)
        cnt = c_above + count(lambda r0: lo_at(r0) >= cand16)
        ok = cnt >= n_keep
        return jnp.where(ok, cand, thr), jnp.where(ok, cnt, c_thr)

    thr_lo, c_thr = lax.fori_loop(
        0, 16, lo_round, (jnp.full((1, tq), -2 ** 15, I32), c_ge_hi))
    thr_lo16 = bcast16(thr_lo)

    need_ties = jnp.max(c_thr) > n_keep
    n_tied_ok = n_keep - (c_above + count(lambda r0: lo_at(r0) > thr_lo16))

    def tied(r0):
        return (lo_at(r0) == thr_lo16) & (hi_at(r0) == thr_hi16)

    def tie_round(i, cut):
        cand = cut + jnp.left_shift(jnp.int32(1), tie_bits - 1 - i)
        cand16 = bcast16(cand)
        cnt = count(lambda r0: tied(r0)
                    & ((r0 + lax.broadcasted_iota(I32, (cr, tq), 0)).astype(I16) < cand16))
        return jnp.where(cnt <= n_tied_ok, cand, cut)

    cut = lax.fori_loop(0, jnp.where(need_ties, tie_bits, 0), tie_round,
                        jnp.zeros((1, tq), I32))
    cut = jnp.where(need_ties, cut, jnp.int32(2 ** 15 - 1))
    cut16 = bcast16(cut)

    pk_ref[...] = jnp.zeros(pk_ref.shape, I32)
    n_sub = LANES // cr

    def pack_body(c, carry):
        pieces = []
        for uu in range(n_sub):
            r0 = pl.multiple_of(c * LANES + uu * cr, cr)
            j16 = (r0 + lax.broadcasted_iota(I32, (cr, tq), 0)).astype(I16)
            sel = ((hi_at(r0) > thr_hi16) | (lo_at(r0) > thr_lo16)
                   | (tied(r0) & (j16 < cut16)))
            adm16 = bcast16(((r0 // CHUNK) <= t_chunk).astype(I32))
            pieces.append(jnp.where(sel, adm16, zero).astype(I32))
        sel32 = jnp.concatenate(pieces, axis=0)
        w0 = pl.multiple_of((c // MASK_BITS) * LANES, LANES)
        pk_ref[pl.ds(w0, LANES), :] = pk_ref[pl.ds(w0, LANES), :] | jnp.left_shift(
            sel32, c % MASK_BITS)
        return carry

    lax.fori_loop(0, n_adm // LANES, pack_body, 0)
    bits_ref[0] = pk_ref[...].T


def _topk_mask(qcat, kcat, wi, *, tq, n_keep):
    bsz, seq, _ = qcat.shape
    assert seq < 2 ** 15, "key indices are compared as int16"
    n_words = -(-seq // (MASK_BITS * LANES))
    tie_bits = max(1, int(seq).bit_length())
    return pl.pallas_call(
        functools.partial(_topk_mask_kernel, tq=tq, tkr=min(2 * tq, seq), n_keep=n_keep,
                          cr=CHUNK, unroll=tq // CHUNK, tie_bits=tie_bits),
        grid=(bsz, seq // tq),
        in_specs=[pl.BlockSpec((1, tq, qcat.shape[2]), lambda b, i: (b, i, 0)),
                  pl.BlockSpec((1, seq, kcat.shape[2]), lambda b, i: (b, 0, 0)),
                  pl.BlockSpec((1, tq, LANES), lambda b, i: (b, i, 0))],
        out_specs=pl.BlockSpec((1, tq, n_words * LANES), lambda b, i: (b, i, 0)),
        out_shape=jax.ShapeDtypeStruct((bsz, seq, n_words * LANES), I32),
        scratch_shapes=[pltpu.VMEM((seq, tq), I16),
                        pltpu.VMEM((seq, tq), I16),
                        pltpu.VMEM((n_words * LANES, tq), I32)],
        compiler_params=_cparams(("arbitrary", "arbitrary")),
        name="topk_mask",
    )(qcat, kcat, wi)


def _alibi_q_lanes(t_f, lane, slope2):
    a = -slope2 * t_f
    a_hi = a.astype(BF16).astype(F32)
    s1, s2, s3 = _bf16_pieces(slope2)
    val = jnp.where(lane == 0, a_hi, jnp.where(lane == 1, a - a_hi, 0.0))
    val = jnp.where((lane == 2) | (lane == 5), s1, val)
    val = jnp.where((lane == 3) | (lane == 6), s2, val)
    val = jnp.where((lane == 4) | (lane == 7), s3, val)
    return val.astype(BF16)


def _attn_kernel(q_ref, k_ref, v_ref, bits_ref, o_ref, qx_ref, m_ref, acc_ref, mb_ref,
                 corr_ref, *, tq, tk, nk):
    qi = pl.program_id(1)
    kt = pl.program_id(2)
    last = ((qi + 1) * tq - 1) // tk
    n_sub = tk // LANES
    lane = lax.broadcasted_iota(I32, (tq, LANES), 1)
    lo_half = lane < HEAD_DIM

    @pl.when(kt == 0)
    def _init():
        m_ref[...] = jnp.full(m_ref.shape, M_INIT, F32)
        acc_ref[...] = jnp.zeros(acc_ref.shape, F32)
        t_f = (qi * tq + lax.broadcasted_iota(I32, (tq, LANES), 0)).astype(F32)
        for h in range(N_HEADS):
            q2 = q_ref[0, :, (h // 2) * LANES:(h // 2 + 1) * LANES]
            zero = jnp.zeros_like(q2)
            rows = slice((h % 2) * tq, (h % 2 + 1) * tq)
            qx_ref[h // 2, rows, :LANES] = (jnp.where(lo_half, q2, zero) if h % 2 == 0
                                            else jnp.where(lo_half, zero, q2))
            qx_ref[h // 2, rows, LANES:] = _alibi_q_lanes(t_f, lane, _alibi_slope2(h))

    def step(diag):
        bits = bits_ref[0]
        sel = jnp.concatenate(
            [(bits >> ((kt * n_sub + i) % MASK_BITS)) & 1 for i in range(n_sub)], axis=1)
        mb_ref[...] = jnp.where(sel == 1, 0.0, MASK_BIAS)
        if diag:
            t = qi * tq + lax.broadcasted_iota(I32, (tq, 1), 0)
            j = kt * tk + lax.broadcasted_iota(I32, (1, tk), 1)
            corr_ref[...] = (2 * jnp.maximum(j - t, 0)).astype(F32)
        for h in range(N_HEADS):
            hp = h // 2
            if h % 2 == 0:
                s_pair = lax.dot_general(qx_ref[hp],
                                         k_ref[0, :, 2 * hp * LANES:(2 * hp + 2) * LANES],
                                         _NT, preferred_element_type=F32)
            s = s_pair[(h % 2) * tq:(h % 2 + 1) * tq]
            if diag:
                s = s + (mb_ref[...] - _alibi_slope2(h) * corr_ref[...])
            else:
                s = s + mb_ref[...]
            m_old = m_ref[h]
            m_new = jnp.maximum(m_old, jnp.max(s, axis=1, keepdims=True))
            p = jnp.exp2(s - jnp.tile(m_new, (1, n_sub)))
            m_ref[h] = m_new
            acc_ref[h] = (jnp.exp2(m_old - m_new) * acc_ref[h]
                          + jnp.dot(p.astype(BF16), v_ref[0, :, h * LANES:(h + 1) * LANES],
                                    preferred_element_type=F32))

    pl.when(kt < last)(functools.partial(step, False))
    pl.when(kt == last)(functools.partial(step, True))

    @pl.when(kt == nk - 1)
    def _finish():
        for hp in range(N_HEADS // 2):
            a0 = acc_ref[2 * hp]
            a1 = acc_ref[2 * hp + 1]
            o0 = a0 / pltpu.roll(a0, HEAD_DIM, axis=1)
            o1 = a1 / pltpu.roll(a1, HEAD_DIM, axis=1)
            o_ref[0, :, hp * LANES:(hp + 1) * LANES] = jnp.where(lo_half, o0, o1).astype(o_ref.dtype)


def _attention(q, k_ext, v_ext, bits, *, tq, tk):
    bsz, seq, d = q.shape
    nk = seq // tk
    n_sub = tk // LANES

    def last_kt(i):
        return ((i + 1) * tq - 1) // tk

    kv_map = lambda b, i, t: (b, jnp.minimum(t, last_kt(i)), 0)
    bits_map = lambda b, i, t: (b, i, (jnp.minimum(t, last_kt(i)) * n_sub) // MASK_BITS)
    return pl.pallas_call(
        functools.partial(_attn_kernel, tq=tq, tk=tk, nk=nk),
        grid=(bsz, seq // tq, nk),
        in_specs=[pl.BlockSpec((1, tq, d), lambda b, i, t: (b, i, 0)),
                  pl.BlockSpec((1, tk, 2 * d), kv_map),
                  pl.BlockSpec((1, tk, 2 * d), kv_map),
                  pl.BlockSpec((1, tq, LANES), bits_map)],
        out_specs=pl.BlockSpec((1, tq, d), lambda b, i, t: (b, i, 0)),
        out_shape=jax.ShapeDtypeStruct((bsz, seq, d), BF16),
        scratch_shapes=[pltpu.VMEM((N_HEADS // 2, 2 * tq, 2 * LANES), BF16),
                        pltpu.VMEM((N_HEADS, tq, LANES), F32),
                        pltpu.VMEM((N_HEADS, tq, LANES), F32),
                        pltpu.VMEM((tq, tk), F32),
                        pltpu.VMEM((tq, tk), F32)],
        compiler_params=_cparams(("arbitrary", "arbitrary", "arbitrary")),
        name="masked_attention",
    )(q, k_ext, v_ext, bits)


def _sigmoid(z):
    return 0.5 * jnp.tanh(0.5 * z) + 0.5


def _proj_residual_kernel(y_ref, w_ref, x_ref, gate_ref, g_ref, *rest, glu, final, d):
    r = jnp.dot(y_ref[0], w_ref[...], preferred_element_type=F32)
    if glu:
        r = r[:, :d] * _sigmoid(r[:, d:])
    x_new = x_ref[0] + gate_ref[0] * r
    if final:
        (o_ref,) = rest
        o_ref[0] = _rms(x_new, g_ref[...])
    else:
        sh_ref, sc_ref, o_ref, h_ref = rest
        o_ref[0] = x_new
        h_ref[0] = _norm_mod(x_new, g_ref[...], sh_ref[0], sc_ref[0]).astype(h_ref.dtype)


def _proj_residual(y, w, x, gate, g, sh=None, sc=None, *, tm, glu, name):
    bsz, seq, d = x.shape
    kdim, n = w.shape
    final = sh is None
    row = lambda b, i: (b, i, 0)
    per_b = lambda b, i: (b, 0, 0)
    const = lambda b, i: (0, 0)
    in_specs = [pl.BlockSpec((1, tm, kdim), row),
                pl.BlockSpec((kdim, n), const),
                pl.BlockSpec((1, tm, d), row),
                pl.BlockSpec((1, 1, d), per_b),
                pl.BlockSpec((1, d), const)]
    args = [y, w.astype(BF16), x, gate, g]
    out_specs = [pl.BlockSpec((1, tm, d), row)]
    out_shape = [jax.ShapeDtypeStruct((bsz, seq, d), F32)]
    if not final:
        in_specs += [pl.BlockSpec((1, 1, d), per_b), pl.BlockSpec((1, 1, d), per_b)]
        args += [sh, sc]
        out_specs.append(pl.BlockSpec((1, tm, d), row))
        out_shape.append(jax.ShapeDtypeStruct((bsz, seq, d), BF16))
    out = pl.pallas_call(
        functools.partial(_proj_residual_kernel, glu=glu, final=final, d=d),
        grid=(bsz, seq // tm),
        in_specs=in_specs,
        out_specs=out_specs,
        out_shape=out_shape,
        compiler_params=_cparams(("arbitrary", "arbitrary")),
        name=name,
    )(*args)
    return out[0] if final else out


def _ffn_up_kernel(h_ref, wv_ref, wg_ref, cwv_ref, cwg_ref,
                   cbv_ref, cbg_ref, a_ref, zv_ref, zg_ref, *, tm):
    i = pl.program_id(2)
    ng = tm // SUBLANES
    tn = a_ref.shape[2]

    @pl.when(i == 0)
    def _seq_start():
        zv_ref[...] = jnp.zeros(zv_ref.shape, F32)
        zg_ref[...] = jnp.zeros(zg_ref.shape, F32)

    h = h_ref[0]
    sub = lax.broadcasted_iota(I32, (ng, SUBLANES, tn), 1)

    def conv(w_ref, carry_ref, cw_ref, cb_ref):
        z = jnp.dot(h, w_ref[...], preferred_element_type=F32).reshape(ng, SUBLANES, tn)
        z_all = jnp.concatenate([carry_ref[...][None], z], axis=0)
        carry_ref[...] = z[ng - 1]
        out = cb_ref[...] + cw_ref[CONV_WIDTH - 1:CONV_WIDTH, :] * z
        for back in range(1, CONV_WIDTH):
            rot = pltpu.roll(z_all, back, axis=1)
            shifted = jnp.where(sub < back, rot[:ng], rot[1:])
            out = out + cw_ref[CONV_WIDTH - 1 - back:CONV_WIDTH - back, :] * shifted
        return out

    val = conv(wv_ref, zv_ref, cwv_ref, cbv_ref)
    gt = conv(wg_ref, zg_ref, cwg_ref, cbg_ref)
    hg = 0.5 * gt
    hv = hg * val
    a_ref[0] = (hv + hv * jnp.tanh(hg)).reshape(tm, tn).astype(a_ref.dtype)


def _ffn_up(h, w_up, conv_w, conv_b, *, tm, tn):
    bsz, seq, d = h.shape
    dff = w_up.shape[1] // 2
    nj = dff // tn
    wv = w_up[:, :dff].astype(BF16)
    wg = w_up[:, dff:].astype(BF16)
    cwv = jnp.pad(conv_w[:, :dff], ((0, SUBLANES - CONV_WIDTH), (0, 0)))
    cwg = jnp.pad(conv_w[:, dff:], ((0, SUBLANES - CONV_WIDTH), (0, 0)))
    cbv = conv_b[:dff].reshape(1, dff)
    cbg = conv_b[dff:].reshape(1, dff)
    col = lambda j, b, i: (0, j)
    return pl.pallas_call(
        functools.partial(_ffn_up_kernel, tm=tm),
        grid=(nj, bsz, seq // tm),
        in_specs=[pl.BlockSpec((1, tm, d), lambda j, b, i: (b, i, 0)),
                  pl.BlockSpec((d, tn), col),
                  pl.BlockSpec((d, tn), col),
                  pl.BlockSpec((SUBLANES, tn), col),
                  pl.BlockSpec((SUBLANES, tn), col),
                  pl.BlockSpec((1, tn), col),
                  pl.BlockSpec((1, tn), col)],
        out_specs=pl.BlockSpec((1, tm, tn), lambda j, b, i: (b, i, j)),
        out_shape=jax.ShapeDtypeStruct((bsz, seq, dff), BF16),
        scratch_shapes=[pltpu.VMEM((SUBLANES, tn), F32),
                        pltpu.VMEM((SUBLANES, tn), F32)],
        compiler_params=_cparams(("arbitrary", "arbitrary", "arbitrary")),
        name="ffn_up",
    )(h, wv, wg, cwv, cwg, cbv, cbg)


def _ssm_in_kernel(h_ref, w_ref, u_ref):
    u_ref[0] = jnp.dot(h_ref[0], w_ref[...], preferred_element_type=F32)


def _ssm_in(h, w, *, tm):
    bsz, seq, d = h.shape
    n = w.shape[1]
    row = lambda b, i: (b, i, 0)
    return pl.pallas_call(
        _ssm_in_kernel,
        grid=(bsz, seq // tm),
        in_specs=[pl.BlockSpec((1, tm, d), row),
                  pl.BlockSpec((d, n), lambda b, i: (0, 0))],
        out_specs=pl.BlockSpec((1, tm, n), row),
        out_shape=jax.ShapeDtypeStruct((bsz, seq, n), F32),
        compiler_params=_cparams(("arbitrary", "arbitrary")),
        name="ssm_in",
    )(h, w.astype(BF16))


def _ssm_scan_kernel(u_ref, bbt_ref, apow_ref, ppow_ref, cc_ref, dskip_ref, y_ref,
                     carry_ref, *, tt, ns, n_steps):
    i = pl.program_id(2)
    ng = tt // SUBLANES

    @pl.when(i == 0)
    def _seq_start():
        carry_ref[...] = jnp.zeros(carry_ref.shape, F32)

    def cmul_add(x_re, x_im, a_re, a_im, y_re, y_im):
        return x_re + a_re * y_re - a_im * y_im, x_im + a_re * y_im + a_im * y_re

    u = u_ref[0]
    bu = jnp.dot(u.astype(BF16), bbt_ref[0], preferred_element_type=F32)
    loc_re = bu[:, :ns].reshape(ng, SUBLANES, ns)
    loc_im = bu[:, ns:].reshape(ng, SUBLANES, ns)
    for kk in range(n_steps - 1):
        loc_re, loc_im = cmul_add(loc_re, loc_im, apow_ref[0, kk, 0], apow_ref[0, kk, 1],
                                  pltpu.roll(loc_re, 1 << kk, axis=1),
                                  pltpu.roll(loc_im, 1 << kk, axis=1))
    last_re = jnp.broadcast_to(loc_re[:, SUBLANES - 1:, :], (ng, SUBLANES, ns))
    last_im = jnp.broadcast_to(loc_im[:, SUBLANES - 1:, :], (ng, SUBLANES, ns))
    a8_re = apow_ref[0, n_steps - 1, 0]
    a8_im = apow_ref[0, n_steps - 1, 1]
    c_re, c_im = carry_ref[0], carry_ref[1]
    cin_re, cin_im = [], []
    for r in range(ng):
        cin_re.append(c_re)
        cin_im.append(c_im)
        c_re, c_im = cmul_add(last_re[r], last_im[r], a8_re, a8_im, c_re, c_im)
    carry_ref[0] = c_re
    carry_ref[1] = c_im
    hg_re, hg_im = cmul_add(loc_re, loc_im, ppow_ref[0, 0], ppow_ref[0, 1],
                            jnp.stack(cin_re), jnp.stack(cin_im))
    hcat = jnp.concatenate([hg_re.reshape(tt, ns), hg_im.reshape(tt, ns)], axis=1).astype(BF16)
    y = jnp.dot(hcat, cc_ref[0], preferred_element_type=F32) + dskip_ref[0] * u
    y_ref[0] = jax.nn.gelu(y).astype(y_ref.dtype)


def _ssm_scan(u, lam_re, lam_im, log_dt, b_re, b_im, c_re, c_im, d_skip, *, tt):
    bsz, seq, width = u.shape
    n_groups, n_state = lam_re.shape
    gs = SSM_GROUP
    gpb = LANES // gs
    n_gb = n_groups // gpb
    ns = gpb * n_state
    n_steps = int(math.log2(SUBLANES)) + 1
    dt = jnp.exp(log_dt)[:, None]
    mag = jnp.exp(lam_re * dt)
    a_re = mag * jnp.cos(lam_im * dt)
    a_im = mag * jnp.sin(lam_im * dt)
    den = lam_re * lam_re + lam_im * lam_im
    coef_re = ((a_re - 1.0) * lam_re + a_im * lam_im) / den
    coef_im = (a_im * lam_re - (a_re - 1.0) * lam_im) / den
    bb_re = coef_re[..., None] * b_re - coef_im[..., None] * b_im
    bb_im = coef_re[..., None] * b_im + coef_im[..., None] * b_re
    eye = jnp.eye(gpb, dtype=F32)

    def block_diag_in(bb):
        t = bb.reshape(n_gb, gpb, n_state, gs)
        return jnp.einsum('ngpc,gh->ngchp', t, eye).reshape(n_gb, gpb * gs, gpb * n_state)

    def block_diag_out(cm):
        t = cm.reshape(n_gb, gpb, gs, n_state)
        return jnp.einsum('ngcp,gh->ngphc', t, eye).reshape(n_gb, gpb * n_state, gpb * gs)

    bbt = jnp.concatenate([block_diag_in(bb_re), block_diag_in(bb_im)], axis=2).astype(BF16)
    cc = jnp.concatenate([block_diag_out(c_re), -block_diag_out(c_im)], axis=1).astype(BF16)

    def powers(e):
        ang = lam_im[None] * dt[None] * e[:, None, None]
        mg = jnp.exp(lam_re[None] * dt[None] * e[:, None, None])
        pw = jnp.stack([mg * jnp.cos(ang), mg * jnp.sin(ang)], axis=1)
        pw = pw.reshape(e.shape[0], 2, n_gb, ns)
        return jnp.transpose(pw, (2, 0, 1, 3))

    apow = powers(2.0 ** jnp.arange(n_steps, dtype=F32))
    shift = (2 ** jnp.arange(n_steps))[:, None]
    live = (jnp.arange(SUBLANES)[None, :] >= shift) | (shift >= SUBLANES)
    apow = apow[:, :, :, None, :] * live[None, :, None, :, None].astype(F32)
    ppow = jnp.transpose(powers(jnp.arange(1, SUBLANES + 1, dtype=F32)), (0, 2, 1, 3))
    dsk = d_skip.reshape(n_gb, 1, gpb * gs)
    return pl.pallas_call(
        functools.partial(_ssm_scan_kernel, tt=tt, ns=ns, n_steps=n_steps),
        grid=(bsz, n_gb, seq // tt),
        in_specs=[pl.BlockSpec((1, tt, LANES), lambda b, g, i: (b, i, g)),
                  pl.BlockSpec((1, LANES, 2 * ns), lambda b, g, i: (g, 0, 0)),
                  pl.BlockSpec((1, n_steps, 2, SUBLANES, ns), lambda b, g, i: (g, 0, 0, 0, 0)),
                  pl.BlockSpec((1, 2, SUBLANES, ns), lambda b, g, i: (g, 0, 0, 0)),
                  pl.BlockSpec((1, 2 * ns, LANES), lambda b, g, i: (g, 0, 0)),
                  pl.BlockSpec((1, 1, LANES), lambda b, g, i: (g, 0, 0))],
        out_specs=pl.BlockSpec((1, tt, LANES), lambda b, g, i: (b, i, g)),
        out_shape=jax.ShapeDtypeStruct((bsz, seq, width), BF16),
        scratch_shapes=[pltpu.VMEM((2, SUBLANES, ns), F32)],
        compiler_params=_cparams(("arbitrary", "arbitrary", "arbitrary")),
        name="ssm_scan",
    )(u, bbt, apow, ppow, cc, dsk)


def kernel(x, c, mod_w, mod_b, norm_mix_g, norm_ffn_g, attn_w_in, attn_w_out, ssm_w_in,
           ssm_lam_re, ssm_lam_im, ssm_log_dt, ssm_b_re, ssm_b_im, ssm_c_re, ssm_c_im,
           ssm_d, ssm_w_glu, ffn_w_up, ffn_conv_w, ffn_conv_b, ffn_w_down, final_g):
    bsz, seq, d = x.shape
    depth = mod_w.shape[0]
    n_keep = min(TOPK_MAX, seq // 4)
    tq = 256
    tk = min(512, seq)
    tm = min(512, seq)
    dff = ffn_w_down.shape[1]
    tn = dff // 2

    mod = _mod_vectors(c, mod_w, mod_b)
    mods = [[mod[i, :, None, k * d:(k + 1) * d] for k in range(6)] for i in range(depth)]
    h = None
    for i in range(depth):
        sh1, sc1, g1, sh2, sc2, g2 = mods[i]
        gm = norm_mix_g[i].reshape(1, d)
        gf = norm_ffn_g[i].reshape(1, d)
        j = i // 2
        if i % 2 == 0:
            q, k, v, qcat, kcat, wi = _attn_proj(x, gm, sh1, sc1, attn_w_in[j], tm=min(256, seq))
            bits = _topk_mask(qcat, kcat, wi, tq=tq, n_keep=n_keep)
            att = _attention(q, k, v, bits, tq=min(512, seq), tk=tk)
            x, h = _proj_residual(att, attn_w_out[j], x, g1, gf, sh2, sc2, tm=tm, glu=False,
                                  name="attn_out")
        else:
            u = _ssm_in(h, ssm_w_in[j], tm=tm)
            y = _ssm_scan(u, ssm_lam_re[j], ssm_lam_im[j], ssm_log_dt[j], ssm_b_re[j],
                          ssm_b_im[j], ssm_c_re[j], ssm_c_im[j], ssm_d[j], tt=min(256, seq))
            x, h = _proj_residual(y, ssm_w_glu[j], x, g1, gf, sh2, sc2, tm=tm, glu=True,
                                  name="ssm_glu")
        a = _ffn_up(h, ffn_w_up[i], ffn_conv_w[i], ffn_conv_b[i], tm=min(1024, seq), tn=tn)
        if i == depth - 1:
            x = _proj_residual(a, ffn_w_down[i], x, g2, final_g.reshape(1, d), tm=tm, glu=False,
                               name="ffn_down_final")
        else:
            sh_n, sc_n = mods[i + 1][0], mods[i + 1][1]
            x, h = _proj_residual(a, ffn_w_down[i], x, g2, norm_mix_g[i + 1].reshape(1, d),
                                  sh_n, sc_n, tm=tm, glu=False, name="ffn_down")
    return x
```

```python
import functools
import math

import jax
import jax.numpy as jnp
import numpy as np
from jax import lax
from jax.experimental import pallas as pl
from jax.experimental.pallas import tpu as pltpu

F32 = jnp.float32
BF16 = jnp.bfloat16
I32 = jnp.int32
I16 = jnp.int16

EPS = 1e-6
CHUNK = 64
N_HEADS = 16
HEAD_DIM = 64
IDX_HEADS = 8
IDX_DIM = 64
TOPK_MAX = 256
SSM_GROUP = 16
SSM_STATE = 64
CONV_WIDTH = 3

LANES = 128
SUBLANES = 8
MASK_BITS = 32
VMEM_LIMIT = 56 * 1024 * 1024

INT_MIN = -2 ** 31
KEY_NEG_INF = INT_MIN + 0x7FFFFF
MASK_BIAS = -1e30
M_INIT = -3e38
LOG2E = math.log2(math.e)


def _bf16_pieces(x, n=3):
    out, rest = [], np.float32(x)
    for _ in range(n):
        piece = np.float32(np.asarray(rest, np.float32).astype(BF16).astype(np.float32))
        out.append(float(piece))
        rest = np.float32(rest - piece)
    return out


def _alibi_slope2(h):
    return 2.0 ** (-8.0 * (h + 1) / N_HEADS) * LOG2E

_NT = (((1,), (1,)), ((), ()))


def _cparams(sem):
    return pltpu.CompilerParams(dimension_semantics=sem, vmem_limit_bytes=VMEM_LIMIT)


def _norm_mod(x, g, sh, sc):
    ms = jnp.mean(x * x, axis=-1, keepdims=True)
    y = x * lax.rsqrt(ms + EPS)
    return (y * g) * (1.0 + sc) + sh


def _rms(x, g):
    ms = jnp.mean(x * x, axis=-1, keepdims=True)
    return (x * lax.rsqrt(ms + EPS)) * g


def _mod_kernel(c_ref, w_ref, b_ref, o_ref):
    c = c_ref[...]
    cond = c * (1.0 / (1.0 + jnp.exp(-c)))
    o_ref[...] = jnp.dot(cond, w_ref[...], preferred_element_type=F32,
                         precision=lax.Precision.HIGHEST) + b_ref[...]


def _mod_vectors(c, mod_w, mod_b):
    depth, d, n6 = mod_w.shape
    bsz = c.shape[0]
    rows = SUBLANES
    c_pad = jnp.zeros((rows, d), F32).at[:bsz].set(c)
    out = pl.pallas_call(
        _mod_kernel,
        grid=(depth, n6 // d),
        in_specs=[pl.BlockSpec((rows, d), lambda i, j: (0, 0)),
                  pl.BlockSpec((None, d, d), lambda i, j: (i, 0, j)),
                  pl.BlockSpec((None, 1, d), lambda i, j: (i, 0, j))],
        out_specs=pl.BlockSpec((None, rows, d), lambda i, j: (i, 0, j)),
        out_shape=jax.ShapeDtypeStruct((depth, rows, n6), F32),
        compiler_params=_cparams(("arbitrary", "arbitrary")),
        name="mod_vectors",
    )(c_pad, mod_w, mod_b.reshape(depth, 1, n6))
    return out[:, :bsz, :]


def _attn_proj_kernel(x_ref, g_ref, sh_ref, sc_ref, wqkv_ref, wih_ref, wil_ref,
                      q_ref, k_ref, v_ref, qcat_ref, kcat_ref, wi_ref, *, d):
    h = _norm_mod(x_ref[0], g_ref[...], sh_ref[0], sc_ref[0])
    hb = h.astype(BF16)
    hl = (h - hb.astype(F32)).astype(BF16)
    qkv = jnp.dot(hb, wqkv_ref[...], preferred_element_type=F32)
    q_ref[0] = (qkv[:, :d] * (HEAD_DIM ** -0.5 * LOG2E)).astype(BF16)
    kb = qkv[:, d:2 * d].astype(BF16)
    vb = qkv[:, 2 * d:3 * d].astype(BF16)
    tm = kb.shape[0]
    j = pl.program_id(1) * tm + lax.broadcasted_iota(I32, (tm, LANES), 0)
    ln = lax.broadcasted_iota(I32, (tm, LANES), 1)
    j_hi = ((j // CHUNK) * CHUNK).astype(F32)
    j_lo = (j % CHUNK).astype(F32)
    kx = jnp.where(ln < 2, 1.0, jnp.where(ln < 5, j_hi, jnp.where(ln < 8, j_lo, 0.0))).astype(BF16)
    lo_half = ln < HEAD_DIM
    ones = jnp.ones((tm, LANES), BF16)
    for hp in range(N_HEADS // 2):
        k_ref[0, :, 2 * hp * LANES:(2 * hp + 1) * LANES] = kb[:, hp * LANES:(hp + 1) * LANES]
        k_ref[0, :, (2 * hp + 1) * LANES:(2 * hp + 2) * LANES] = kx
        v2 = vb[:, hp * LANES:(hp + 1) * LANES]
        v_ref[0, :, 2 * hp * LANES:(2 * hp + 1) * LANES] = jnp.where(lo_half, v2, ones)
        v_ref[0, :, (2 * hp + 1) * LANES:(2 * hp + 2) * LANES] = jnp.where(lo_half, ones, v2)
    wih = wih_ref[...]
    idx = (jnp.dot(hb, wih, preferred_element_type=F32)
           + jnp.dot(hl, wih, preferred_element_type=F32)
           + jnp.dot(hb, wil_ref[...], preferred_element_type=F32))
    nq = IDX_HEADS * IDX_DIM
    def split(tile):
        hi = tile.astype(BF16).astype(F32)
        return hi, tile - hi

    for pp in range(IDX_HEADS // 2):
        hi, lo = split(idx[:, pp * LANES:(pp + 1) * LANES] * (IDX_DIM ** -0.5))
        first = jnp.where(lo_half, hi, pltpu.roll(lo, IDX_DIM, axis=1)).astype(BF16)
        second = jnp.where(lo_half, pltpu.roll(hi, IDX_DIM, axis=1), lo).astype(BF16)
        for e, piece in enumerate((first, second)):
            hh = 2 * pp + e
            qcat_ref[0, :, 2 * hh * LANES:(2 * hh + 1) * LANES] = piece
            qcat_ref[0, :, (2 * hh + 1) * LANES:(2 * hh + 2) * LANES] = piece
    tail = idx[:, nq:nq + LANES]
    khi, klo = split(tail)
    kcat_ref[0, :, :LANES] = jnp.where(lo_half, khi, pltpu.roll(khi, IDX_DIM, axis=1)).astype(BF16)
    kcat_ref[0, :, LANES:] = jnp.where(lo_half, klo, pltpu.roll(klo, IDX_DIM, axis=1)).astype(BF16)
    wi_ref[0] = jnp.where(ln < IDX_HEADS, pltpu.roll(tail, IDX_DIM, axis=1) * (IDX_HEADS ** -0.5), 0.0)


def _attn_proj(x, g, sh, sc, w_in, *, tm):
    bsz, seq, d = x.shape
    nq = IDX_HEADS * IDX_DIM
    wqkv = w_in[:, :3 * d].astype(BF16)
    w_idx = w_in[:, 3 * d:]
    w_idx = jnp.pad(w_idx, ((0, 0), (0, nq + LANES - w_idx.shape[1])))
    w_hi = w_idx.astype(BF16)
    w_lo = (w_idx - w_hi.astype(F32)).astype(BF16)
    n_idx = w_idx.shape[1]
    row = lambda b, i: (b, i, 0)
    const = lambda b, i: (0, 0)
    per_b = lambda b, i: (b, 0, 0)
    return pl.pallas_call(
        functools.partial(_attn_proj_kernel, d=d),
        grid=(bsz, seq // tm),
        in_specs=[pl.BlockSpec((1, tm, d), row),
                  pl.BlockSpec((1, d), const),
                  pl.BlockSpec((1, 1, d), per_b),
                  pl.BlockSpec((1, 1, d), per_b),
                  pl.BlockSpec((d, 3 * d), const, pipeline_mode=pl.Buffered(1)),
                  pl.BlockSpec((d, n_idx), const, pipeline_mode=pl.Buffered(1)),
                  pl.BlockSpec((d, n_idx), const, pipeline_mode=pl.Buffered(1))],
        out_specs=[pl.BlockSpec((1, tm, d), row),
                   pl.BlockSpec((1, tm, 2 * d), row),
                   pl.BlockSpec((1, tm, 2 * d), row),
                   pl.BlockSpec((1, tm, 2 * IDX_HEADS * LANES), row),
                   pl.BlockSpec((1, tm, 2 * LANES), row),
                   pl.BlockSpec((1, tm, LANES), row)],
        out_shape=[jax.ShapeDtypeStruct((bsz, seq, d), BF16),
                   jax.ShapeDtypeStruct((bsz, seq, 2 * d), BF16),
                   jax.ShapeDtypeStruct((bsz, seq, 2 * d), BF16),
                   jax.ShapeDtypeStruct((bsz, seq, 2 * IDX_HEADS * LANES), BF16),
                   jax.ShapeDtypeStruct((bsz, seq, 2 * LANES), BF16),
                   jax.ShapeDtypeStruct((bsz, seq, LANES), F32)],
        compiler_params=_cparams(("arbitrary", "arbitrary")),
        name="attn_proj",
    )(x, g, sh, sc, wqkv, w_hi, w_lo)


def _topk_mask_kernel(qcat_ref, kcat_ref, wi_ref, bits_ref, hi_ref, lo_ref, pk_ref, gmax_ref,
                      *, tq, tkr, n_keep, cr, unroll, tie_bits):
    qi = pl.program_id(1)
    n_adm = (qi + 1) * tq
    t_chunk = (qi * tq + lax.broadcasted_iota(I32, (1, tq), 1)) // CHUNK
    w_t = wi_ref[0].T

    def score_body(kt, carry):
        r0 = pl.multiple_of(kt * tkr, tkr)
        kc = kcat_ref[0, pl.ds(r0, tkr), :]
        sc = jnp.zeros((tkr, tq), F32)
        for hh in range(IDX_HEADS):
            lg = lax.dot_general(kc, qcat_ref[0, :, 2 * hh * LANES:(2 * hh + 2) * LANES],
                                 _NT, preferred_element_type=F32)
            sc = sc + w_t[hh:hh + 1, :] * jnp.maximum(lg, 0.0)
        sc = sc + 0.0
        raw = lax.bitcast_convert_type(sc, I32)
        key = raw ^ ((raw >> 31) & 0x7FFFFFFF)
        j_chunk = (r0 + lax.broadcasted_iota(I32, (tkr, 1), 0)) // CHUNK
        key = jnp.where(j_chunk <= t_chunk, key, KEY_NEG_INF)
        hi16 = (key >> 16).astype(I16)
        hi_ref[pl.ds(r0, tkr), :] = hi16
        lo_ref[pl.ds(r0, tkr), :] = ((key & 0xFFFF) - 2 ** 15).astype(I16)
        g = gmax_ref[...]
        for part in range(tkr // tq):
            piece = hi16[part * tq:(part + 1) * tq]
            g = jnp.where(piece > g, piece, g)
        gmax_ref[...] = g
        return carry

    gmax_ref[...] = jnp.full(gmax_ref.shape, -2 ** 15, I16)
    lax.fori_loop(0, (n_adm + tkr - 1) // tkr, score_body, 0)

    n_ch = n_adm // (cr * unroll)
    one, zero = jnp.int16(1), jnp.int16(0)

    def count(pred):
        def body(c, acc):
            for uu in range(unroll):
                r0 = pl.multiple_of((c * unroll + uu) * cr, cr)
                acc = acc + jnp.where(pred(r0), one, zero)
            return acc
        acc = lax.fori_loop(0, n_ch, body, jnp.zeros((cr, tq), I16))
        return jnp.sum(acc.astype(I32), axis=0, keepdims=True)

    def hi_at(r0):
        return hi_ref[pl.ds(r0, cr), :]

    def lo_at(r0):
        return lo_ref[pl.ds(r0, cr), :]

    def bcast16(v):
        return jnp.broadcast_to(v, (cr, tq)).astype(I16)

    gmax = gmax_ref[...].astype(I32)
    lo0 = jnp.min(gmax, axis=0, keepdims=True)
    hi0 = jnp.max(gmax, axis=0, keepdims=True) + 1
    lo0_16 = bcast16(lo0)
    c_lo0 = count(lambda r0: hi_at(r0) >= lo0_16)
    span = (hi0 - lo0 - 1).astype(F32)
    n_bits = jnp.maximum((lax.bitcast_convert_type(span, I32) >> 23) - 126, 0)
    n_rounds = jnp.max(n_bits)

    def hi_round(i, carry):
        lo, hi, c_lo = carry
        mid = lo + ((hi - lo) >> 1)
        mid16 = bcast16(mid)
        cnt = count(lambda r0: hi_at(r0) >= mid16)
        ok = cnt >= n_keep
        return jnp.where(ok, mid, lo), jnp.where(ok, hi, mid), jnp.where(ok, cnt, c_lo)

    thr_hi, _, c_ge_hi = lax.fori_loop(0, n_rounds, hi_round, (lo0, hi0, c_lo0))
    thr_hi16 = bcast16(thr_hi)
    c_above = count(lambda r0: hi_at(r0) > thr_hi16)

    def bucket_body(c, carry):
        r0 = pl.multiple_of(c * cr, cr)
        lo_ref[pl.ds(r0, cr), :] = jnp.where(hi_at(r0) == thr_hi16, lo_at(r0),
                                             jnp.int16(-2 ** 15))
        return carry

    lax.fori_loop(0, n_adm // cr, bucket_body, 0)

    def lo_round(i, carry):
        thr, c_thr = carry
        cand = thr + jnp.left_shift(jnp.int32(1), 15 - i)
        cand16 = bcast16(cand)
        cnt = c_above + count(lambda r0: lo_at(r0) >= cand16)
        ok = cnt >= n_keep
        return jnp.where(ok, cand, thr), jnp.where(ok, cnt, c_thr)

    thr_lo, c_thr = lax.fori_loop(
        0, 16, lo_round, (jnp.full((1, tq), -2 ** 15, I32), c_ge_hi))
    thr_lo16 = bcast16(thr_lo)

    need_ties = jnp.max(c_thr) > n_keep
    n_tied_ok = n_keep - (c_above + count(lambda r0: lo_at(r0) > thr_lo16))

    def tied(r0):
        return (lo_at(r0) == thr_lo16) & (hi_at(r0) == thr_hi16)

    def tie_round(i, cut):
        cand = cut + jnp.left_shift(jnp.int32(1), tie_bits - 1 - i)
        cand16 = bcast16(cand)
        cnt = count(lambda r0: tied(r0)
                    & ((r0 + lax.broadcasted_iota(I32, (cr, tq), 0)).astype(I16) < cand16))
        return jnp.where(cnt <= n_tied_ok, cand, cut)

    cut = lax.fori_loop(0, jnp.where(need_ties, tie_bits, 0), tie_round,
                        jnp.zeros((1, tq), I32))
    cut = jnp.where(need_ties, cut, jnp.int32(2 ** 15 - 1))
    cut16 = bcast16(cut)

    pk_ref[...] = jnp.zeros(pk_ref.shape, I32)
    n_sub = LANES // cr

    def pack_body(c, carry):
        pieces = []
        for uu in range(n_sub):
            r0 = pl.multiple_of(c * LANES + uu * cr, cr)
            j16 = (r0 + lax.broadcasted_iota(I32, (cr, tq), 0)).astype(I16)
            sel = ((hi_at(r0) > thr_hi16) | (lo_at(r0) > thr_lo16)
                   | (tied(r0) & (j16 < cut16)))
            adm16 = bcast16(((r0 // CHUNK) <= t_chunk).astype(I32))
            pieces.append(jnp.where(sel, adm16, zero).astype(I32))
        sel32 = jnp.concatenate(pieces, axis=0)
        w0 = pl.multiple_of((c // MASK_BITS) * LANES, LANES)
        pk_ref[pl.ds(w0, LANES), :] = pk_ref[pl.ds(w0, LANES), :] | jnp.left_shift(
            sel32, c % MASK_BITS)
        return carry

    lax.fori_loop(0, n_adm // LANES, pack_body, 0)
    bits_ref[0] = pk_ref[...].T


def _topk_mask(qcat, kcat, wi, *, tq, n_keep):
    bsz, seq, _ = qcat.shape
    assert seq < 2 ** 15, "key indices are compared as int16"
    assert n_keep <= tq, "the threshold bracket needs at least n_keep row classes"
    n_words = -(-seq // (MASK_BITS * LANES))
    tie_bits = max(1, int(seq).bit_length())
    return pl.pallas_call(
        functools.partial(_topk_mask_kernel, tq=tq, tkr=min(2 * tq, seq), n_keep=n_keep,
                          cr=CHUNK, unroll=tq // CHUNK, tie_bits=tie_bits),
        grid=(bsz, seq // tq),
        in_specs=[pl.BlockSpec((1, tq, qcat.shape[2]), lambda b, i: (b, i, 0)),
                  pl.BlockSpec((1, seq, kcat.shape[2]), lambda b, i: (b, 0, 0)),
                  pl.BlockSpec((1, tq, LANES), lambda b, i: (b, i, 0))],
        out_specs=pl.BlockSpec((1, tq, n_words * LANES), lambda b, i: (b, i, 0)),
        out_shape=jax.ShapeDtypeStruct((bsz, seq, n_words * LANES), I32),
        scratch_shapes=[pltpu.VMEM((seq, tq), I16),
                        pltpu.VMEM((seq, tq), I16),
                        pltpu.VMEM((n_words * LANES, tq), I32),
                        pltpu.VMEM((tq, tq), I16)],
        compiler_params=_cparams(("arbitrary", "arbitrary")),
        name="topk_mask",
    )(qcat, kcat, wi)


def _alibi_q_lanes(t_f, lane, slope2):
    a = -slope2 * t_f
    a_hi = a.astype(BF16).astype(F32)
    s1, s2, s3 = _bf16_pieces(slope2)
    val = jnp.where(lane == 0, a_hi, jnp.where(lane == 1, a - a_hi, 0.0))
    val = jnp.where((lane == 2) | (lane == 5), s1, val)
    val = jnp.where((lane == 3) | (lane == 6), s2, val)
    val = jnp.where((lane == 4) | (lane == 7), s3, val)
    return val.astype(BF16)


def _attn_kernel(q_ref, k_ref, v_ref, bits_ref, o_ref, qx_ref, m_ref, acc_ref, mb_ref,
                 corr_ref, *, tq, tk, nk):
    qi = pl.program_id(1)
    kt = pl.program_id(2)
    last = ((qi + 1) * tq - 1) // tk
    n_sub = tk // LANES
    lane = lax.broadcasted_iota(I32, (tq, LANES), 1)
    lo_half = lane < HEAD_DIM

    @pl.when(kt == 0)
    def _init():
        m_ref[...] = jnp.full(m_ref.shape, M_INIT, F32)
        acc_ref[...] = jnp.zeros(acc_ref.shape, F32)
        t_f = (qi * tq + lax.broadcasted_iota(I32, (tq, LANES), 0)).astype(F32)
        for h in range(N_HEADS):
            q2 = q_ref[0, :, (h // 2) * LANES:(h // 2 + 1) * LANES]
            zero = jnp.zeros_like(q2)
            rows = slice((h % 2) * tq, (h % 2 + 1) * tq)
            qx_ref[h // 2, rows, :LANES] = (jnp.where(lo_half, q2, zero) if h % 2 == 0
                                            else jnp.where(lo_half, zero, q2))
            qx_ref[h // 2, rows, LANES:] = _alibi_q_lanes(t_f, lane, _alibi_slope2(h))

    def step(diag):
        bits = bits_ref[0]
        sel = jnp.concatenate(
            [(bits >> ((kt * n_sub + i) % MASK_BITS)) & 1 for i in range(n_sub)], axis=1)
        mb_ref[...] = jnp.where(sel == 1, 0.0, MASK_BIAS)
        if diag:
            t = qi * tq + lax.broadcasted_iota(I32, (tq, 1), 0)
            j = kt * tk + lax.broadcasted_iota(I32, (1, tk), 1)
            corr_ref[...] = (2 * jnp.maximum(j - t, 0)).astype(F32)
        for h in range(N_HEADS):
            hp = h // 2
            if h % 2 == 0:
                s_pair = lax.dot_general(qx_ref[hp],
                                         k_ref[0, :, 2 * hp * LANES:(2 * hp + 2) * LANES],
                                         _NT, preferred_element_type=F32)
            s = s_pair[(h % 2) * tq:(h % 2 + 1) * tq]
            if diag:
                s = s + (mb_ref[...] - _alibi_slope2(h) * corr_ref[...])
            else:
                s = s + mb_ref[...]
            m_old = m_ref[h]
            m_new = jnp.maximum(m_old, jnp.max(s, axis=1, keepdims=True))
            p = jnp.exp2(s - jnp.tile(m_new, (1, n_sub)))
            m_ref[h] = m_new
            acc_ref[h] = (jnp.exp2(m_old - m_new) * acc_ref[h]
                          + jnp.dot(p.astype(BF16), v_ref[0, :, h * LANES:(h + 1) * LANES],
                                    preferred_element_type=F32))

    pl.when(kt < last)(functools.partial(step, False))
    pl.when(kt == last)(functools.partial(step, True))

    @pl.when(kt == nk - 1)
    def _finish():
        for hp in range(N_HEADS // 2):
            a0 = acc_ref[2 * hp]
            a1 = acc_ref[2 * hp + 1]
            o0 = a0 / pltpu.roll(a0, HEAD_DIM, axis=1)
            o1 = a1 / pltpu.roll(a1, HEAD_DIM, axis=1)
            o_ref[0, :, hp * LANES:(hp + 1) * LANES] = jnp.where(lo_half, o0, o1).astype(o_ref.dtype)


def _attention(q, k_ext, v_ext, bits, *, tq, tk):
    bsz, seq, d = q.shape
    nk = seq // tk
    n_sub = tk // LANES

    def last_kt(i):
        return ((i + 1) * tq - 1) // tk

    kv_map = lambda b, i, t: (b, jnp.minimum(t, last_kt(i)), 0)
    bits_map = lambda b, i, t: (b, i, (jnp.minimum(t, last_kt(i)) * n_sub) // MASK_BITS)
    return pl.pallas_call(
        functools.partial(_attn_kernel, tq=tq, tk=tk, nk=nk),
        grid=(bsz, seq // tq, nk),
        in_specs=[pl.BlockSpec((1, tq, d), lambda b, i, t: (b, i, 0)),
                  pl.BlockSpec((1, tk, 2 * d), kv_map),
                  pl.BlockSpec((1, tk, 2 * d), kv_map),
                  pl.BlockSpec((1, tq, LANES), bits_map)],
        out_specs=pl.BlockSpec((1, tq, d), lambda b, i, t: (b, i, 0)),
        out_shape=jax.ShapeDtypeStruct((bsz, seq, d), BF16),
        scratch_shapes=[pltpu.VMEM((N_HEADS // 2, 2 * tq, 2 * LANES), BF16),
                        pltpu.VMEM((N_HEADS, tq, LANES), F32),
                        pltpu.VMEM((N_HEADS, tq, LANES), F32),
                        pltpu.VMEM((tq, tk), F32),
                        pltpu.VMEM((tq, tk), F32)],
        compiler_params=_cparams(("arbitrary", "arbitrary", "arbitrary")),
        name="masked_attention",
    )(q, k_ext, v_ext, bits)


def _sigmoid(z):
    return 0.5 * jnp.tanh(0.5 * z) + 0.5


def _proj_residual_kernel(y_ref, w_ref, x_ref, gate_ref, g_ref, *rest, glu, final, d):
    r = jnp.dot(y_ref[0], w_ref[...], preferred_element_type=F32)
    if glu:
        r = r[:, :d] * _sigmoid(r[:, d:])
    x_new = x_ref[0] + gate_ref[0] * r
    if final:
        (o_ref,) = rest
        o_ref[0] = _rms(x_new, g_ref[...])
    else:
        sh_ref, sc_ref, o_ref, h_ref = rest
        o_ref[0] = x_new
        h_ref[0] = _norm_mod(x_new, g_ref[...], sh_ref[0], sc_ref[0]).astype(h_ref.dtype)


def _proj_residual(y, w, x, gate, g, sh=None, sc=None, *, tm, glu, name):
    bsz, seq, d = x.shape
    kdim, n = w.shape
    final = sh is None
    row = lambda b, i: (b, i, 0)
    per_b = lambda b, i: (b, 0, 0)
    const = lambda b, i: (0, 0)
    in_specs = [pl.BlockSpec((1, tm, kdim), row),
                pl.BlockSpec((kdim, n), const, pipeline_mode=pl.Buffered(1)),
                pl.BlockSpec((1, tm, d), row),
                pl.BlockSpec((1, 1, d), per_b),
                pl.BlockSpec((1, d), const)]
    args = [y, w.astype(BF16), x, gate, g]
    out_specs = [pl.BlockSpec((1, tm, d), row)]
    out_shape = [jax.ShapeDtypeStruct((bsz, seq, d), F32)]
    if not final:
        in_specs += [pl.BlockSpec((1, 1, d), per_b), pl.BlockSpec((1, 1, d), per_b)]
        args += [sh, sc]
        out_specs.append(pl.BlockSpec((1, tm, d), row))
        out_shape.append(jax.ShapeDtypeStruct((bsz, seq, d), BF16))
    out = pl.pallas_call(
        functools.partial(_proj_residual_kernel, glu=glu, final=final, d=d),
        grid=(bsz, seq // tm),
        in_specs=in_specs,
        out_specs=out_specs,
        out_shape=out_shape,
        compiler_params=_cparams(("arbitrary", "arbitrary")),
        name=name,
    )(*args)
    return out[0] if final else out


def _ffn_up_kernel(h_ref, wv_ref, wg_ref, cwv_ref, cwg_ref,
                   cbv_ref, cbg_ref, a_ref, zv_ref, zg_ref, *, tm):
    i = pl.program_id(2)
    ng = tm // SUBLANES
    tn = a_ref.shape[2]

    @pl.when(i == 0)
    def _seq_start():
        zv_ref[...] = jnp.zeros(zv_ref.shape, F32)
        zg_ref[...] = jnp.zeros(zg_ref.shape, F32)

    h = h_ref[0]
    sub = lax.broadcasted_iota(I32, (ng, SUBLANES, tn), 1)

    def conv(w_ref, carry_ref, cw_ref, cb_ref):
        z = jnp.dot(h, w_ref[...], preferred_element_type=F32).reshape(ng, SUBLANES, tn)
        z_all = jnp.concatenate([carry_ref[...][None], z], axis=0)
        carry_ref[...] = z[ng - 1]
        out = cb_ref[...] + cw_ref[CONV_WIDTH - 1:CONV_WIDTH, :] * z
        for back in range(1, CONV_WIDTH):
            rot = pltpu.roll(z_all, back, axis=1)
            shifted = jnp.where(sub < back, rot[:ng], rot[1:])
            out = out + cw_ref[CONV_WIDTH - 1 - back:CONV_WIDTH - back, :] * shifted
        return out

    val = conv(wv_ref, zv_ref, cwv_ref, cbv_ref)
    gt = conv(wg_ref, zg_ref, cwg_ref, cbg_ref)
    hg = 0.5 * gt
    hv = hg * val
    a_ref[0] = (hv + hv * jnp.tanh(hg)).reshape(tm, tn).astype(a_ref.dtype)


def _ffn_up(h, w_up, conv_w, conv_b, *, tm, tn):
    bsz, seq, d = h.shape
    dff = w_up.shape[1] // 2
    nj = dff // tn
    wv = w_up[:, :dff].astype(BF16)
    wg = w_up[:, dff:].astype(BF16)
    cwv = jnp.pad(conv_w[:, :dff], ((0, SUBLANES - CONV_WIDTH), (0, 0)))
    cwg = jnp.pad(conv_w[:, dff:], ((0, SUBLANES - CONV_WIDTH), (0, 0)))
    cbv = conv_b[:dff].reshape(1, dff)
    cbg = conv_b[dff:].reshape(1, dff)
    col = lambda j, b, i: (0, j)
    return pl.pallas_call(
        functools.partial(_ffn_up_kernel, tm=tm),
        grid=(nj, bsz, seq // tm),
        in_specs=[pl.BlockSpec((1, tm, d), lambda j, b, i: (b, i, 0)),
                  pl.BlockSpec((d, tn), col),
                  pl.BlockSpec((d, tn), col),
                  pl.BlockSpec((SUBLANES, tn), col),
                  pl.BlockSpec((SUBLANES, tn), col),
                  pl.BlockSpec((1, tn), col),
                  pl.BlockSpec((1, tn), col)],
        out_specs=pl.BlockSpec((1, tm, tn), lambda j, b, i: (b, i, j)),
        out_shape=jax.ShapeDtypeStruct((bsz, seq, dff), BF16),
        scratch_shapes=[pltpu.VMEM((SUBLANES, tn), F32),
                        pltpu.VMEM((SUBLANES, tn), F32)],
        compiler_params=_cparams(("arbitrary", "arbitrary", "arbitrary")),
        name="ffn_up",
    )(h, wv, wg, cwv, cwg, cbv, cbg)


def _ssm_in_kernel(h_ref, w_ref, u_ref):
    u_ref[0] = jnp.dot(h_ref[0], w_ref[...], preferred_element_type=F32)


def _ssm_in(h, w, *, tm):
    bsz, seq, d = h.shape
    n = w.shape[1]
    row = lambda b, i: (b, i, 0)
    return pl.pallas_call(
        _ssm_in_kernel,
        grid=(bsz, seq // tm),
        in_specs=[pl.BlockSpec((1, tm, d), row),
                  pl.BlockSpec((d, n), lambda b, i: (0, 0))],
        out_specs=pl.BlockSpec((1, tm, n), row),
        out_shape=jax.ShapeDtypeStruct((bsz, seq, n), F32),
        compiler_params=_cparams(("arbitrary", "arbitrary")),
        name="ssm_in",
    )(h, w.astype(BF16))


def _ssm_scan_kernel(u_ref, bbt_ref, apow_ref, ppow_ref, cc_ref, dskip_ref, y_ref,
                     carry_ref, *, tt, ns, n_steps):
    i = pl.program_id(2)
    ng = tt // SUBLANES

    @pl.when(i == 0)
    def _seq_start():
        carry_ref[...] = jnp.zeros(carry_ref.shape, F32)

    def cmul_add(x_re, x_im, a_re, a_im, y_re, y_im):
        return x_re + a_re * y_re - a_im * y_im, x_im + a_re * y_im + a_im * y_re

    u = u_ref[0]
    bu = jnp.dot(u.astype(BF16), bbt_ref[0], preferred_element_type=F32)
    loc_re = bu[:, :ns].reshape(ng, SUBLANES, ns)
    loc_im = bu[:, ns:].reshape(ng, SUBLANES, ns)
    for kk in range(n_steps - 1):
        loc_re, loc_im = cmul_add(loc_re, loc_im, apow_ref[0, kk, 0], apow_ref[0, kk, 1],
                                  pltpu.roll(loc_re, 1 << kk, axis=1),
                                  pltpu.roll(loc_im, 1 << kk, axis=1))
    last_re = jnp.broadcast_to(loc_re[:, SUBLANES - 1:, :], (ng, SUBLANES, ns))
    last_im = jnp.broadcast_to(loc_im[:, SUBLANES - 1:, :], (ng, SUBLANES, ns))
    a8_re = apow_ref[0, n_steps - 1, 0]
    a8_im = apow_ref[0, n_steps - 1, 1]
    c_re, c_im = carry_ref[0], carry_ref[1]
    cin_re, cin_im = [], []
    for r in range(ng):
        cin_re.append(c_re)
        cin_im.append(c_im)
        c_re, c_im = cmul_add(last_re[r], last_im[r], a8_re, a8_im, c_re, c_im)
    carry_ref[0] = c_re
    carry_ref[1] = c_im
    hg_re, hg_im = cmul_add(loc_re, loc_im, ppow_ref[0, 0], ppow_ref[0, 1],
                            jnp.stack(cin_re), jnp.stack(cin_im))
    hcat = jnp.concatenate([hg_re.reshape(tt, ns), hg_im.reshape(tt, ns)], axis=1).astype(BF16)
    y = jnp.dot(hcat, cc_ref[0], preferred_element_type=F32) + dskip_ref[0] * u
    y_ref[0] = jax.nn.gelu(y).astype(y_ref.dtype)


def _ssm_scan(u, lam_re, lam_im, log_dt, b_re, b_im, c_re, c_im, d_skip, *, tt):
    bsz, seq, width = u.shape
    n_groups, n_state = lam_re.shape
    gs = SSM_GROUP
    gpb = LANES // gs
    n_gb = n_groups // gpb
    ns = gpb * n_state
    n_steps = int(math.log2(SUBLANES)) + 1
    dt = jnp.exp(log_dt)[:, None]
    mag = jnp.exp(lam_re * dt)
    a_re = mag * jnp.cos(lam_im * dt)
    a_im = mag * jnp.sin(lam_im * dt)
    den = lam_re * lam_re + lam_im * lam_im
    coef_re = ((a_re - 1.0) * lam_re + a_im * lam_im) / den
    coef_im = (a_im * lam_re - (a_re - 1.0) * lam_im) / den
    bb_re = coef_re[..., None] * b_re - coef_im[..., None] * b_im
    bb_im = coef_re[..., None] * b_im + coef_im[..., None] * b_re
    eye = jnp.eye(gpb, dtype=F32)

    def block_diag_in(bb):
        t = bb.reshape(n_gb, gpb, n_state, gs)
        return jnp.einsum('ngpc,gh->ngchp', t, eye).reshape(n_gb, gpb * gs, gpb * n_state)

    def block_diag_out(cm):
        t = cm.reshape(n_gb, gpb, gs, n_state)
        return jnp.einsum('ngcp,gh->ngphc', t, eye).reshape(n_gb, gpb * n_state, gpb * gs)

    bbt = jnp.concatenate([block_diag_in(bb_re), block_diag_in(bb_im)], axis=2).astype(BF16)
    cc = jnp.concatenate([block_diag_out(c_re), -block_diag_out(c_im)], axis=1).astype(BF16)

    def powers(e):
        ang = lam_im[None] * dt[None] * e[:, None, None]
        mg = jnp.exp(lam_re[None] * dt[None] * e[:, None, None])
        pw = jnp.stack([mg * jnp.cos(ang), mg * jnp.sin(ang)], axis=1)
        pw = pw.reshape(e.shape[0], 2, n_gb, ns)
        return jnp.transpose(pw, (2, 0, 1, 3))

    apow = powers(2.0 ** jnp.arange(n_steps, dtype=F32))
    shift = (2 ** jnp.arange(n_steps))[:, None]
    live = (jnp.arange(SUBLANES)[None, :] >= shift) | (shift >= SUBLANES)
    apow = apow[:, :, :, None, :] * live[None, :, None, :, None].astype(F32)
    ppow = jnp.transpose(powers(jnp.arange(1, SUBLANES + 1, dtype=F32)), (0, 2, 1, 3))
    dsk = d_skip.reshape(n_gb, 1, gpb * gs)
    return pl.pallas_call(
        functools.partial(_ssm_scan_kernel, tt=tt, ns=ns, n_steps=n_steps),
        grid=(bsz, n_gb, seq // tt),
        in_specs=[pl.BlockSpec((1, tt, LANES), lambda b, g, i: (b, i, g)),
                  pl.BlockSpec((1, LANES, 2 * ns), lambda b, g, i: (g, 0, 0)),
                  pl.BlockSpec((1, n_steps, 2, SUBLANES, ns), lambda b, g, i: (g, 0, 0, 0, 0)),
                  pl.BlockSpec((1, 2, SUBLANES, ns), lambda b, g, i: (g, 0, 0, 0)),
                  pl.BlockSpec((1, 2 * ns, LANES), lambda b, g, i: (g, 0, 0)),
                  pl.BlockSpec((1, 1, LANES), lambda b, g, i: (g, 0, 0))],
        out_specs=pl.BlockSpec((1, tt, LANES), lambda b, g, i: (b, i, g)),
        out_shape=jax.ShapeDtypeStruct((bsz, seq, width), BF16),
        scratch_shapes=[pltpu.VMEM((2, SUBLANES, ns), F32)],
        compiler_params=_cparams(("arbitrary", "arbitrary", "arbitrary")),
        name="ssm_scan",
    )(u, bbt, apow, ppow, cc, dsk)


def kernel(x, c, mod_w, mod_b, norm_mix_g, norm_ffn_g, attn_w_in, attn_w_out, ssm_w_in,
           ssm_lam_re, ssm_lam_im, ssm_log_dt, ssm_b_re, ssm_b_im, ssm_c_re, ssm_c_im,
           ssm_d, ssm_w_glu, ffn_w_up, ffn_conv_w, ffn_conv_b, ffn_w_down, final_g):
    bsz, seq, d = x.shape
    depth = mod_w.shape[0]
    n_keep = min(TOPK_MAX, seq // 4)
    tq = 256
    tk = min(512, seq)
    tm = min(1024, seq)
    dff = ffn_w_down.shape[1]
    tn = dff // 2

    mod = _mod_vectors(c, mod_w, mod_b)
    mods = [[mod[i, :, None, k * d:(k + 1) * d] for k in range(6)] for i in range(depth)]
    h = None
    for i in range(depth):
        sh1, sc1, g1, sh2, sc2, g2 = mods[i]
        gm = norm_mix_g[i].reshape(1, d)
        gf = norm_ffn_g[i].reshape(1, d)
        j = i // 2
        if i % 2 == 0:
            q, k, v, qcat, kcat, wi = _attn_proj(x, gm, sh1, sc1, attn_w_in[j], tm=min(512, seq))
            bits = _topk_mask(qcat, kcat, wi, tq=tq, n_keep=n_keep)
            att = _attention(q, k, v, bits, tq=min(512, seq), tk=tk)
            x, h = _proj_residual(att, attn_w_out[j], x, g1, gf, sh2, sc2, tm=tm, glu=False,
                                  name="attn_out")
        else:
            u = _ssm_in(h, ssm_w_in[j], tm=tm)
            y = _ssm_scan(u, ssm_lam_re[j], ssm_lam_im[j], ssm_log_dt[j], ssm_b_re[j],
                          ssm_b_im[j], ssm_c_re[j], ssm_c_im[j], ssm_d[j], tt=min(256, seq))
            x, h = _proj_residual(y, ssm_w_glu[j], x, g1, gf, sh2, sc2, tm=tm, glu=True,
                                  name="ssm_glu")
        a = _ffn_up(h, ffn_w_up[i], ffn_conv_w[i], ffn_conv_b[i], tm=min(1024, seq), tn=tn)
        if i == depth - 1:
            x = _proj_residual(a, ffn_w_down[i], x, g2, final_g.reshape(1, d), tm=tm, glu=False,
                               name="ffn_down_final")
        else:
            sh_n, sc_n = mods[i + 1][0], mods[i + 1][1]
            x, h = _proj_residual(a, ffn_w_down[i], x, g2, norm_mix_g[i + 1].reshape(1, d),
                                  sh_n, sc_n, tm=tm, glu=False, name="ffn_down")
    return x
```

```python
import functools
import math

import jax
import jax.numpy as jnp
import numpy as np
from jax import lax
from jax.experimental import pallas as pl
from jax.experimental.pallas import tpu as pltpu

F32 = jnp.float32
BF16 = jnp.bfloat16
I32 = jnp.int32
I16 = jnp.int16

EPS = 1e-6
CHUNK = 64
N_HEADS = 16
HEAD_DIM = 64
IDX_HEADS = 8
IDX_DIM = 64
TOPK_MAX = 256
SSM_GROUP = 16
SSM_STATE = 64
CONV_WIDTH = 3

LANES = 128
SUBLANES = 8
MASK_BITS = 32
VMEM_LIMIT = 56 * 1024 * 1024

INT_MIN = -2 ** 31
KEY_NEG_INF = INT_MIN + 0x7FFFFF
MASK_BIAS = -1e30
M_INIT = -3e38
LOG2E = math.log2(math.e)


def _bf16_pieces(x, n=3):
    out, rest = [], np.float32(x)
    for _ in range(n):
        piece = np.float32(np.asarray(rest, np.float32).astype(BF16).astype(np.float32))
        out.append(float(piece))
        rest = np.float32(rest - piece)
    return out


def _alibi_slope2(h):
    return 2.0 ** (-8.0 * (h + 1) / N_HEADS) * LOG2E

_NT = (((1,), (1,)), ((), ()))


def _cparams(sem):
    return pltpu.CompilerParams(dimension_semantics=sem, vmem_limit_bytes=VMEM_LIMIT)


def _norm_mod(x, g, sh, sc):
    ms = jnp.mean(x * x, axis=-1, keepdims=True)
    y = x * lax.rsqrt(ms + EPS)
    return (y * g) * (1.0 + sc) + sh


def _rms(x, g):
    ms = jnp.mean(x * x, axis=-1, keepdims=True)
    return (x * lax.rsqrt(ms + EPS)) * g


def _mod_kernel(c_ref, w_ref, b_ref, o_ref):
    c = c_ref[...]
    cond = c * (1.0 / (1.0 + jnp.exp(-c)))
    o_ref[...] = jnp.dot(cond, w_ref[...], preferred_element_type=F32,
                         precision=lax.Precision.HIGHEST) + b_ref[...]


def _mod_vectors(c, mod_w, mod_b):
    depth, d, n6 = mod_w.shape
    bsz = c.shape[0]
    rows = SUBLANES
    c_pad = jnp.zeros((rows, d), F32).at[:bsz].set(c)
    out = pl.pallas_call(
        _mod_kernel,
        grid=(depth, n6 // d),
        in_specs=[pl.BlockSpec((rows, d), lambda i, j: (0, 0)),
                  pl.BlockSpec((None, d, d), lambda i, j: (i, 0, j)),
                  pl.BlockSpec((None, 1, d), lambda i, j: (i, 0, j))],
        out_specs=pl.BlockSpec((None, rows, d), lambda i, j: (i, 0, j)),
        out_shape=jax.ShapeDtypeStruct((depth, rows, n6), F32),
        compiler_params=_cparams(("arbitrary", "arbitrary")),
        name="mod_vectors",
    )(c_pad, mod_w, mod_b.reshape(depth, 1, n6))
    return out[:, :bsz, :]


def _attn_proj_kernel(x_ref, g_ref, sh_ref, sc_ref, wqkv_ref, wih_ref, wil_ref,
                      q_ref, k_ref, v_ref, qcat_ref, kcat_ref, wi_ref, *, d):
    h = _norm_mod(x_ref[0], g_ref[...], sh_ref[0], sc_ref[0])
    hb = h.astype(BF16)
    hl = (h - hb.astype(F32)).astype(BF16)
    qkv = jnp.dot(hb, wqkv_ref[...], preferred_element_type=F32)
    q_ref[0] = (qkv[:, :d] * (HEAD_DIM ** -0.5 * LOG2E)).astype(BF16)
    kb = qkv[:, d:2 * d].astype(BF16)
    vb = qkv[:, 2 * d:3 * d].astype(BF16)
    tm = kb.shape[0]
    j = pl.program_id(1) * tm + lax.broadcasted_iota(I32, (tm, LANES), 0)
    ln = lax.broadcasted_iota(I32, (tm, LANES), 1)
    j_hi = ((j // CHUNK) * CHUNK).astype(F32)
    j_lo = (j % CHUNK).astype(F32)
    kx = jnp.where(ln < 2, 1.0, jnp.where(ln < 5, j_hi, jnp.where(ln < 8, j_lo, 0.0))).astype(BF16)
    lo_half = ln < HEAD_DIM
    ones = jnp.ones((tm, LANES), BF16)
    for hp in range(N_HEADS // 2):
        k_ref[0, :, 2 * hp * LANES:(2 * hp + 1) * LANES] = kb[:, hp * LANES:(hp + 1) * LANES]
        k_ref[0, :, (2 * hp + 1) * LANES:(2 * hp + 2) * LANES] = kx
        v2 = vb[:, hp * LANES:(hp + 1) * LANES]
        v_ref[0, :, 2 * hp * LANES:(2 * hp + 1) * LANES] = jnp.where(lo_half, v2, ones)
        v_ref[0, :, (2 * hp + 1) * LANES:(2 * hp + 2) * LANES] = jnp.where(lo_half, ones, v2)
    wih = wih_ref[...]
    idx = (jnp.dot(hb, wih, preferred_element_type=F32)
           + jnp.dot(hl, wih, preferred_element_type=F32)
           + jnp.dot(hb, wil_ref[...], preferred_element_type=F32))
    nq = IDX_HEADS * IDX_DIM
    def split(tile):
        hi = tile.astype(BF16).astype(F32)
        return hi, tile - hi

    for pp in range(IDX_HEADS // 2):
        hi, lo = split(idx[:, pp * LANES:(pp + 1) * LANES] * (IDX_DIM ** -0.5))
        first = jnp.where(lo_half, hi, pltpu.roll(lo, IDX_DIM, axis=1)).astype(BF16)
        second = jnp.where(lo_half, pltpu.roll(hi, IDX_DIM, axis=1), lo).astype(BF16)
        for e, piece in enumerate((first, second)):
            hh = 2 * pp + e
            qcat_ref[0, :, 2 * hh * LANES:(2 * hh + 1) * LANES] = piece
            qcat_ref[0, :, (2 * hh + 1) * LANES:(2 * hh + 2) * LANES] = piece
    tail = idx[:, nq:nq + LANES]
    khi, klo = split(tail)
    kcat_ref[0, :, :LANES] = jnp.where(lo_half, khi, pltpu.roll(khi, IDX_DIM, axis=1)).astype(BF16)
    kcat_ref[0, :, LANES:] = jnp.where(lo_half, klo, pltpu.roll(klo, IDX_DIM, axis=1)).astype(BF16)
    wi_ref[0] = jnp.where(ln < IDX_HEADS, pltpu.roll(tail, IDX_DIM, axis=1) * (IDX_HEADS ** -0.5), 0.0)


def _attn_proj(x, g, sh, sc, w_in, *, tm):
    bsz, seq, d = x.shape
    nq = IDX_HEADS * IDX_DIM
    wqkv = w_in[:, :3 * d].astype(BF16)
    w_idx = w_in[:, 3 * d:]
    w_idx = jnp.pad(w_idx, ((0, 0), (0, nq + LANES - w_idx.shape[1])))
    w_hi = w_idx.astype(BF16)
    w_lo = (w_idx - w_hi.astype(F32)).astype(BF16)
    n_idx = w_idx.shape[1]
    row = lambda b, i: (b, i, 0)
    const = lambda b, i: (0, 0)
    per_b = lambda b, i: (b, 0, 0)
    return pl.pallas_call(
        functools.partial(_attn_proj_kernel, d=d),
        grid=(bsz, seq // tm),
        in_specs=[pl.BlockSpec((1, tm, d), row),
                  pl.BlockSpec((1, d), const),
                  pl.BlockSpec((1, 1, d), per_b),
                  pl.BlockSpec((1, 1, d), per_b),
                  pl.BlockSpec((d, 3 * d), const, pipeline_mode=pl.Buffered(1)),
                  pl.BlockSpec((d, n_idx), const, pipeline_mode=pl.Buffered(1)),
                  pl.BlockSpec((d, n_idx), const, pipeline_mode=pl.Buffered(1))],
        out_specs=[pl.BlockSpec((1, tm, d), row),
                   pl.BlockSpec((1, tm, 2 * d), row),
                   pl.BlockSpec((1, tm, 2 * d), row),
                   pl.BlockSpec((1, tm, 2 * IDX_HEADS * LANES), row),
                   pl.BlockSpec((1, tm, 2 * LANES), row),
                   pl.BlockSpec((1, tm, LANES), row)],
        out_shape=[jax.ShapeDtypeStruct((bsz, seq, d), BF16),
                   jax.ShapeDtypeStruct((bsz, seq, 2 * d), BF16),
                   jax.ShapeDtypeStruct((bsz, seq, 2 * d), BF16),
                   jax.ShapeDtypeStruct((bsz, seq, 2 * IDX_HEADS * LANES), BF16),
                   jax.ShapeDtypeStruct((bsz, seq, 2 * LANES), BF16),
                   jax.ShapeDtypeStruct((bsz, seq, LANES), F32)],
        compiler_params=_cparams(("arbitrary", "arbitrary")),
        name="attn_proj",
    )(x, g, sh, sc, wqkv, w_hi, w_lo)


def _topk_mask_kernel(qcat_ref, kcat_ref, wi_ref, bits_ref, hi_ref, lo_ref, pk_ref,
                      *, tq, tkr, n_keep, cr, unroll, tie_bits):
    qi = pl.program_id(1)
    n_adm = (qi + 1) * tq
    t_chunk = (qi * tq + lax.broadcasted_iota(I32, (1, tq), 1)) // CHUNK
    w_t = wi_ref[0].T

    def score_body(kt, carry):
        r0 = pl.multiple_of(kt * tkr, tkr)
        kc = kcat_ref[0, pl.ds(r0, tkr), :]
        sc = jnp.zeros((tkr, tq), F32)
        for hh in range(IDX_HEADS):
            lg = lax.dot_general(kc, qcat_ref[0, :, 2 * hh * LANES:(2 * hh + 2) * LANES],
                                 _NT, preferred_element_type=F32)
            sc = sc + w_t[hh:hh + 1, :] * jnp.maximum(lg, 0.0)
        sc = sc + 0.0
        raw = lax.bitcast_convert_type(sc, I32)
        key = raw ^ ((raw >> 31) & 0x7FFFFFFF)
        j_chunk = (r0 + lax.broadcasted_iota(I32, (tkr, 1), 0)) // CHUNK
        key = jnp.where(j_chunk <= t_chunk, key, KEY_NEG_INF)
        hi_ref[pl.ds(r0, tkr), :] = (key >> 16).astype(I16)
        lo_ref[pl.ds(r0, tkr), :] = ((key & 0xFFFF) - 2 ** 15).astype(I16)
        return carry

    lax.fori_loop(0, (n_adm + tkr - 1) // tkr, score_body, 0)

    n_ch = n_adm // (cr * unroll)
    one, zero = jnp.int16(1), jnp.int16(0)

    def count(pred):
        def body(c, acc):
            for uu in range(unroll):
                r0 = pl.multiple_of((c * unroll + uu) * cr, cr)
                acc = acc + jnp.where(pred(r0), one, zero)
            return acc
        acc = lax.fori_loop(0, n_ch, body, jnp.zeros((cr, tq), I16))
        return jnp.sum(acc.astype(I32), axis=0, keepdims=True)

    def hi_at(r0):
        return hi_ref[pl.ds(r0, cr), :]

    def lo_at(r0):
        return lo_ref[pl.ds(r0, cr), :]

    def bcast16(v):
        return jnp.broadcast_to(v, (cr, tq)).astype(I16)

    def hi_round(i, carry):
        thr, c_thr = carry
        cand = thr + jnp.left_shift(jnp.int32(1), 15 - i)
        cand16 = bcast16(cand)
        cnt = count(lambda r0: hi_at(r0) >= cand16)
        ok = cnt >= n_keep
        return jnp.where(ok, cand, thr), jnp.where(ok, cnt, c_thr)

    thr_hi, c_ge_hi = lax.fori_loop(
        0, 16, hi_round,
        (jnp.full((1, tq), -2 ** 15, I32), jnp.full((1, tq), n_adm, I32)))
    thr_hi16 = bcast16(thr_hi)
    c_above = count(lambda r0: hi_at(r0) > thr_hi16)

    def bucket_body(c, carry):
        r0 = pl.multiple_of(c * cr, cr)
        lo_ref[pl.ds(r0, cr), :] = jnp.where(hi_at(r0) == thr_hi16, lo_at(r0),
                                             jnp.int16(-2 ** 15))
        return carry

    lax.fori_loop(0, n_adm // cr, bucket_body, 0)

    def lo_round(i, carry):
        thr, c_thr = carry
        cand = thr + jnp.left_shift(jnp.int32(1), 15 - i)
        cand16 = bcast16(cand)
        cnt = c_above + count(lambda r0: lo_at(r0) >= cand16)
        ok = cnt >= n_keep
        return jnp.where(ok, cand, thr), jnp.where(ok, cnt, c_thr)

    thr_lo, c_thr = lax.fori_loop(
        0, 16, lo_round, (jnp.full((1, tq), -2 ** 15, I32), c_ge_hi))
    thr_lo16 = bcast16(thr_lo)

    need_ties = jnp.max(c_thr) > n_keep
    n_tied_ok = n_keep - (c_above + count(lambda r0: lo_at(r0) > thr_lo16))

    def tied(r0):
        return (lo_at(r0) == thr_lo16) & (hi_at(r0) == thr_hi16)

    def tie_round(i, cut):
        cand = cut + jnp.left_shift(jnp.int32(1), tie_bits - 1 - i)
        cand16 = bcast16(cand)
        cnt = count(lambda r0: tied(r0)
                    & ((r0 + lax.broadcasted_iota(I32, (cr, tq), 0)).astype(I16) < cand16))
        return jnp.where(cnt <= n_tied_ok, cand, cut)

    cut = lax.fori_loop(0, jnp.where(need_ties, tie_bits, 0), tie_round,
                        jnp.zeros((1, tq), I32))
    cut = jnp.where(need_ties, cut, jnp.int32(2 ** 15 - 1))
    cut16 = bcast16(cut)

    pk_ref[...] = jnp.zeros(pk_ref.shape, I32)
    n_sub = LANES // cr

    def pack_body(c, carry):
        pieces = []
        for uu in range(n_sub):
            r0 = pl.multiple_of(c * LANES + uu * cr, cr)
            j16 = (r0 + lax.broadcasted_iota(I32, (cr, tq), 0)).astype(I16)
            sel = ((hi_at(r0) > thr_hi16) | (lo_at(r0) > thr_lo16)
                   | (tied(r0) & (j16 < cut16)))
            adm16 = bcast16(((r0 // CHUNK) <= t_chunk).astype(I32))
            pieces.append(jnp.where(sel, adm16, zero).astype(I32))
        sel32 = jnp.concatenate(pieces, axis=0)
        w0 = pl.multiple_of((c // MASK_BITS) * LANES, LANES)
        pk_ref[pl.ds(w0, LANES), :] = pk_ref[pl.ds(w0, LANES), :] | jnp.left_shift(
            sel32, c % MASK_BITS)
        return carry

    lax.fori_loop(0, n_adm // LANES, pack_body, 0)
    bits_ref[0] = pk_ref[...].T


def _topk_mask(qcat, kcat, wi, *, tq, n_keep):
    bsz, seq, _ = qcat.shape
    assert seq < 2 ** 15, "key indices are compared as int16"
    n_words = -(-seq // (MASK_BITS * LANES))
    tie_bits = max(1, int(seq).bit_length())
    return pl.pallas_call(
        functools.partial(_topk_mask_kernel, tq=tq, tkr=min(2 * tq, seq), n_keep=n_keep,
                          cr=CHUNK, unroll=tq // CHUNK, tie_bits=tie_bits),
        grid=(bsz, seq // tq),
        in_specs=[pl.BlockSpec((1, tq, qcat.shape[2]), lambda b, i: (b, i, 0)),
                  pl.BlockSpec((1, seq, kcat.shape[2]), lambda b, i: (b, 0, 0)),
                  pl.BlockSpec((1, tq, LANES), lambda b, i: (b, i, 0))],
        out_specs=pl.BlockSpec((1, tq, n_words * LANES), lambda b, i: (b, i, 0)),
        out_shape=jax.ShapeDtypeStruct((bsz, seq, n_words * LANES), I32),
        scratch_shapes=[pltpu.VMEM((seq, tq), I16),
                        pltpu.VMEM((seq, tq), I16),
                        pltpu.VMEM((n_words * LANES, tq), I32)],
        compiler_params=_cparams(("arbitrary", "arbitrary")),
        name="topk_mask",
    )(qcat, kcat, wi)


def _alibi_q_lanes(t_f, lane, slope2):
    a = -slope2 * t_f
    a_hi = a.astype(BF16).astype(F32)
    s1, s2, s3 = _bf16_pieces(slope2)
    val = jnp.where(lane == 0, a_hi, jnp.where(lane == 1, a - a_hi, 0.0))
    val = jnp.where((lane == 2) | (lane == 5), s1, val)
    val = jnp.where((lane == 3) | (lane == 6), s2, val)
    val = jnp.where((lane == 4) | (lane == 7), s3, val)
    return val.astype(BF16)


def _attn_kernel(q_ref, k_ref, v_ref, bits_ref, o_ref, qx_ref, m_ref, acc_ref, mb_ref,
                 corr_ref, *, tq, tk, nk):
    qi = pl.program_id(1)
    kt = pl.program_id(2)
    last = ((qi + 1) * tq - 1) // tk
    n_sub = tk // LANES
    lane = lax.broadcasted_iota(I32, (tq, LANES), 1)
    lo_half = lane < HEAD_DIM

    @pl.when(kt == 0)
    def _init():
        m_ref[...] = jnp.full(m_ref.shape, M_INIT, F32)
        acc_ref[...] = jnp.zeros(acc_ref.shape, F32)
        t_f = (qi * tq + lax.broadcasted_iota(I32, (tq, LANES), 0)).astype(F32)
        for h in range(N_HEADS):
            q2 = q_ref[0, :, (h // 2) * LANES:(h // 2 + 1) * LANES]
            zero = jnp.zeros_like(q2)
            rows = slice((h % 2) * tq, (h % 2 + 1) * tq)
            qx_ref[h // 2, rows, :LANES] = (jnp.where(lo_half, q2, zero) if h % 2 == 0
                                            else jnp.where(lo_half, zero, q2))
            qx_ref[h // 2, rows, LANES:] = _alibi_q_lanes(t_f, lane, _alibi_slope2(h))

    def step(diag):
        bits = bits_ref[0]
        sel = jnp.concatenate(
            [(bits >> ((kt * n_sub + i) % MASK_BITS)) & 1 for i in range(n_sub)], axis=1)
        mb_ref[...] = jnp.where(sel == 1, 0.0, MASK_BIAS)
        if diag:
            t = qi * tq + lax.broadcasted_iota(I32, (tq, 1), 0)
            j = kt * tk + lax.broadcasted_iota(I32, (1, tk), 1)
            corr_ref[...] = (2 * jnp.maximum(j - t, 0)).astype(F32)
        for h in range(N_HEADS):
            hp = h // 2
            if h % 2 == 0:
                s_pair = lax.dot_general(qx_ref[hp],
                                         k_ref[0, :, 2 * hp * LANES:(2 * hp + 2) * LANES],
                                         _NT, preferred_element_type=F32)
            s = s_pair[(h % 2) * tq:(h % 2 + 1) * tq]
            if diag:
                s = s + (mb_ref[...] - _alibi_slope2(h) * corr_ref[...])
            else:
                s = s + mb_ref[...]
            m_old = m_ref[h]
            m_new = jnp.maximum(m_old, jnp.max(s, axis=1, keepdims=True))
            p = jnp.exp2(s - jnp.tile(m_new, (1, n_sub)))
            m_ref[h] = m_new
            acc_ref[h] = (jnp.exp2(m_old - m_new) * acc_ref[h]
                          + jnp.dot(p.astype(BF16), v_ref[0, :, h * LANES:(h + 1) * LANES],
                                    preferred_element_type=F32))

    pl.when(kt < last)(functools.partial(step, False))
    pl.when(kt == last)(functools.partial(step, True))

    @pl.when(kt == nk - 1)
    def _finish():
        for hp in range(N_HEADS // 2):
            a0 = acc_ref[2 * hp]
            a1 = acc_ref[2 * hp + 1]
            o0 = a0 / pltpu.roll(a0, HEAD_DIM, axis=1)
            o1 = a1 / pltpu.roll(a1, HEAD_DIM, axis=1)
            o_ref[0, :, hp * LANES:(hp + 1) * LANES] = jnp.where(lo_half, o0, o1).astype(o_ref.dtype)


def _attention(q, k_ext, v_ext, bits, *, tq, tk):
    bsz, seq, d = q.shape
    nk = seq // tk
    n_sub = tk // LANES

    def last_kt(i):
        return ((i + 1) * tq - 1) // tk

    kv_map = lambda b, i, t: (b, jnp.minimum(t, last_kt(i)), 0)
    bits_map = lambda b, i, t: (b, i, (jnp.minimum(t, last_kt(i)) * n_sub) // MASK_BITS)
    return pl.pallas_call(
        functools.partial(_attn_kernel, tq=tq, tk=tk, nk=nk),
        grid=(bsz, seq // tq, nk),
        in_specs=[pl.BlockSpec((1, tq, d), lambda b, i, t: (b, i, 0)),
                  pl.BlockSpec((1, tk, 2 * d), kv_map),
                  pl.BlockSpec((1, tk, 2 * d), kv_map),
                  pl.BlockSpec((1, tq, LANES), bits_map)],
        out_specs=pl.BlockSpec((1, tq, d), lambda b, i, t: (b, i, 0)),
        out_shape=jax.ShapeDtypeStruct((bsz, seq, d), BF16),
        scratch_shapes=[pltpu.VMEM((N_HEADS // 2, 2 * tq, 2 * LANES), BF16),
                        pltpu.VMEM((N_HEADS, tq, LANES), F32),
                        pltpu.VMEM((N_HEADS, tq, LANES), F32),
                        pltpu.VMEM((tq, tk), F32),
                        pltpu.VMEM((tq, tk), F32)],
        compiler_params=_cparams(("arbitrary", "arbitrary", "arbitrary")),
        name="masked_attention",
    )(q, k_ext, v_ext, bits)


def _sigmoid(z):
    return 0.5 * jnp.tanh(0.5 * z) + 0.5


def _proj_residual_kernel(y_ref, w_ref, x_ref, gate_ref, g_ref, *rest, glu, final, d):
    r = jnp.dot(y_ref[0], w_ref[...], preferred_element_type=F32)
    if glu:
        r = r[:, :d] * _sigmoid(r[:, d:])
    x_new = x_ref[0] + gate_ref[0] * r
    if final:
        (o_ref,) = rest
        o_ref[0] = _rms(x_new, g_ref[...])
    else:
        sh_ref, sc_ref, o_ref, h_ref = rest
        o_ref[0] = x_new
        h_ref[0] = _norm_mod(x_new, g_ref[...], sh_ref[0], sc_ref[0]).astype(h_ref.dtype)


def _proj_residual(y, w, x, gate, g, sh=None, sc=None, *, tm, glu, name):
    bsz, seq, d = x.shape
    kdim, n = w.shape
    final = sh is None
    row = lambda b, i: (b, i, 0)
    per_b = lambda b, i: (b, 0, 0)
    const = lambda b, i: (0, 0)
    in_specs = [pl.BlockSpec((1, tm, kdim), row),
                pl.BlockSpec((kdim, n), const, pipeline_mode=pl.Buffered(1)),
                pl.BlockSpec((1, tm, d), row),
                pl.BlockSpec((1, 1, d), per_b),
                pl.BlockSpec((1, d), const)]
    args = [y, w.astype(BF16), x, gate, g]
    out_specs = [pl.BlockSpec((1, tm, d), row)]
    out_shape = [jax.ShapeDtypeStruct((bsz, seq, d), F32)]
    if not final:
        in_specs += [pl.BlockSpec((1, 1, d), per_b), pl.BlockSpec((1, 1, d), per_b)]
        args += [sh, sc]
        out_specs.append(pl.BlockSpec((1, tm, d), row))
        out_shape.append(jax.ShapeDtypeStruct((bsz, seq, d), BF16))
    out = pl.pallas_call(
        functools.partial(_proj_residual_kernel, glu=glu, final=final, d=d),
        grid=(bsz, seq // tm),
        in_specs=in_specs,
        out_specs=out_specs,
        out_shape=out_shape,
        compiler_params=_cparams(("arbitrary", "arbitrary")),
        name=name,
    )(*args)
    return out[0] if final else out


def _ffn_up_kernel(h_ref, wv_ref, wg_ref, cwv_ref, cwg_ref,
                   cbv_ref, cbg_ref, a_ref, zv_ref, zg_ref, *, tm):
    i = pl.program_id(2)
    ng = tm // SUBLANES
    tn = a_ref.shape[2]

    @pl.when(i == 0)
    def _seq_start():
        zv_ref[...] = jnp.zeros(zv_ref.shape, F32)
        zg_ref[...] = jnp.zeros(zg_ref.shape, F32)

    h = h_ref[0]
    sub = lax.broadcasted_iota(I32, (ng, SUBLANES, tn), 1)

    def conv(w_ref, carry_ref, cw_ref, cb_ref):
        z = jnp.dot(h, w_ref[...], preferred_element_type=F32).reshape(ng, SUBLANES, tn)
        z_all = jnp.concatenate([carry_ref[...][None], z], axis=0)
        carry_ref[...] = z[ng - 1]
        out = cb_ref[...] + cw_ref[CONV_WIDTH - 1:CONV_WIDTH, :] * z
        for back in range(1, CONV_WIDTH):
            rot = pltpu.roll(z_all, back, axis=1)
            shifted = jnp.where(sub < back, rot[:ng], rot[1:])
            out = out + cw_ref[CONV_WIDTH - 1 - back:CONV_WIDTH - back, :] * shifted
        return out

    val = conv(wv_ref, zv_ref, cwv_ref, cbv_ref)
    gt = conv(wg_ref, zg_ref, cwg_ref, cbg_ref)
    hg = 0.5 * gt
    hv = hg * val
    a_ref[0] = (hv + hv * jnp.tanh(hg)).reshape(tm, tn).astype(a_ref.dtype)


def _ffn_up(h, w_up, conv_w, conv_b, *, tm, tn):
    bsz, seq, d = h.shape
    dff = w_up.shape[1] // 2
    nj = dff // tn
    wv = w_up[:, :dff].astype(BF16)
    wg = w_up[:, dff:].astype(BF16)
    cwv = jnp.pad(conv_w[:, :dff], ((0, SUBLANES - CONV_WIDTH), (0, 0)))
    cwg = jnp.pad(conv_w[:, dff:], ((0, SUBLANES - CONV_WIDTH), (0, 0)))
    cbv = conv_b[:dff].reshape(1, dff)
    cbg = conv_b[dff:].reshape(1, dff)
    col = lambda j, b, i: (0, j)
    return pl.pallas_call(
        functools.partial(_ffn_up_kernel, tm=tm),
        grid=(nj, bsz, seq // tm),
        in_specs=[pl.BlockSpec((1, tm, d), lambda j, b, i: (b, i, 0)),
                  pl.BlockSpec((d, tn), col),
                  pl.BlockSpec((d, tn), col),
                  pl.BlockSpec((SUBLANES, tn), col),
                  pl.BlockSpec((SUBLANES, tn), col),
                  pl.BlockSpec((1, tn), col),
                  pl.BlockSpec((1, tn), col)],
        out_specs=pl.BlockSpec((1, tm, tn), lambda j, b, i: (b, i, j)),
        out_shape=jax.ShapeDtypeStruct((bsz, seq, dff), BF16),
        scratch_shapes=[pltpu.VMEM((SUBLANES, tn), F32),
                        pltpu.VMEM((SUBLANES, tn), F32)],
        compiler_params=_cparams(("arbitrary", "arbitrary", "arbitrary")),
        name="ffn_up",
    )(h, wv, wg, cwv, cwg, cbv, cbg)


def _ssm_in_kernel(h_ref, w_ref, u_ref):
    u_ref[0] = jnp.dot(h_ref[0], w_ref[...], preferred_element_type=F32)


def _ssm_in(h, w, *, tm):
    bsz, seq, d = h.shape
    n = w.shape[1]
    row = lambda b, i: (b, i, 0)
    return pl.pallas_call(
        _ssm_in_kernel,
        grid=(bsz, seq // tm),
        in_specs=[pl.BlockSpec((1, tm, d), row),
                  pl.BlockSpec((d, n), lambda b, i: (0, 0))],
        out_specs=pl.BlockSpec((1, tm, n), row),
        out_shape=jax.ShapeDtypeStruct((bsz, seq, n), F32),
        compiler_params=_cparams(("arbitrary", "arbitrary")),
        name="ssm_in",
    )(h, w.astype(BF16))


def _ssm_scan_kernel(u_ref, perm_ref, permt_ref, bbt_ref, tab_ref, pt_ref, cc_ref, dskip_ref,
                     y_ref, carry_ref, *, tt, ns):
    i = pl.program_id(1)
    nt = tt // SUBLANES

    @pl.when(i == 0)
    def _seq_start():
        carry_ref[...] = jnp.zeros(carry_ref.shape, F32)

    def cmul_add(x_re, x_im, a_re, a_im, y_re, y_im):
        return x_re + a_re * y_re - a_im * y_im, x_im + a_re * y_im + a_im * y_re

    nb = u_ref.shape[0]
    u = u_ref[...]
    lanes = lambda rows: jnp.concatenate([rows[b] for b in range(nb)], axis=1)
    unlanes = lambda wide: jnp.concatenate(
        [wide[:, b * LANES:(b + 1) * LANES] for b in range(nb)], axis=0)
    up = jnp.dot(perm_ref[...], lanes(u).astype(BF16), preferred_element_type=F32).astype(BF16)
    bu = jnp.dot(unlanes(up), bbt_ref[0], preferred_element_type=F32)
    bu_re = bu[:, :ns].reshape(nb, nt, SUBLANES, ns)
    bu_im = bu[:, ns:].reshape(nb, nt, SUBLANES, ns)
    a_re, a_im = tab_ref[0, 0, 0], tab_ref[0, 0, 1]
    h_re = jnp.zeros((nb, SUBLANES, ns), F32)
    h_im = jnp.zeros((nb, SUBLANES, ns), F32)
    loc_re, loc_im = [], []
    for g in range(nt):
        h_re, h_im = cmul_add(bu_re[:, g], bu_im[:, g], a_re, a_im, h_re, h_im)
        loc_re.append(h_re)
        loc_im.append(h_im)
    s_re, s_im = h_re, h_im
    for kk in range(3):
        s_re, s_im = cmul_add(s_re, s_im, tab_ref[0, 1 + kk, 0], tab_ref[0, 1 + kk, 1],
                              pltpu.roll(s_re, 1 << kk, axis=1),
                              pltpu.roll(s_im, 1 << kk, axis=1))
    c_re, c_im = carry_ref[:, 0], carry_ref[:, 1]
    first = lax.broadcasted_iota(I32, (nb, SUBLANES, ns), 1) == 0
    hin_re, hin_im = cmul_add(jnp.where(first, 0.0, pltpu.roll(s_re, 1, axis=1)),
                              jnp.where(first, 0.0, pltpu.roll(s_im, 1, axis=1)),
                              tab_ref[0, 4, 0], tab_ref[0, 4, 1], c_re, c_im)
    c_re, c_im = cmul_add(jnp.broadcast_to(s_re[:, SUBLANES - 1:], (nb, SUBLANES, ns)),
                          jnp.broadcast_to(s_im[:, SUBLANES - 1:], (nb, SUBLANES, ns)),
                          tab_ref[0, 5, 0], tab_ref[0, 5, 1], c_re, c_im)
    carry_ref[:, 0] = c_re
    carry_ref[:, 1] = c_im
    hg_re, hg_im = cmul_add(jnp.stack(loc_re, axis=1), jnp.stack(loc_im, axis=1),
                            pt_ref[0, 0], pt_ref[0, 1], hin_re[:, None], hin_im[:, None])
    hcat = jnp.concatenate([hg_re.reshape(nb * tt, ns), hg_im.reshape(nb * tt, ns)],
                           axis=1).astype(BF16)
    yp = jnp.dot(hcat, cc_ref[0], preferred_element_type=F32)
    yp = lanes(yp.reshape(nb, tt, LANES))
    yp_hi = yp.astype(BF16)
    yp_lo = (yp - yp_hi.astype(F32)).astype(BF16)
    y = (jnp.dot(permt_ref[...], yp_hi, preferred_element_type=F32)
         + jnp.dot(permt_ref[...], yp_lo, preferred_element_type=F32))
    y = unlanes(y).reshape(nb, tt, LANES) + dskip_ref[0] * u
    y_ref[...] = jax.nn.gelu(y).astype(y_ref.dtype)


def _ssm_scan(u, lam_re, lam_im, log_dt, b_re, b_im, c_re, c_im, d_skip, *, tt):
    bsz, seq, width = u.shape
    n_groups, n_state = lam_re.shape
    gs = SSM_GROUP
    gpb = LANES // gs
    n_gb = n_groups // gpb
    ns = gpb * n_state
    dt = jnp.exp(log_dt)[:, None]
    mag = jnp.exp(lam_re * dt)
    a_re = mag * jnp.cos(lam_im * dt)
    a_im = mag * jnp.sin(lam_im * dt)
    den = lam_re * lam_re + lam_im * lam_im
    coef_re = ((a_re - 1.0) * lam_re + a_im * lam_im) / den
    coef_im = (a_im * lam_re - (a_re - 1.0) * lam_im) / den
    bb_re = coef_re[..., None] * b_re - coef_im[..., None] * b_im
    bb_im = coef_re[..., None] * b_im + coef_im[..., None] * b_re
    eye = jnp.eye(gpb, dtype=F32)

    def block_diag_in(bb):
        t = bb.reshape(n_gb, gpb, n_state, gs)
        return jnp.einsum('ngpc,gh->ngchp', t, eye).reshape(n_gb, gpb * gs, gpb * n_state)

    def block_diag_out(cm):
        t = cm.reshape(n_gb, gpb, gs, n_state)
        return jnp.einsum('ngcp,gh->ngphc', t, eye).reshape(n_gb, gpb * n_state, gpb * gs)

    bbt = jnp.concatenate([block_diag_in(bb_re), block_diag_in(bb_im)], axis=2).astype(BF16)
    cc = jnp.concatenate([block_diag_out(c_re), -block_diag_out(c_im)], axis=1).astype(BF16)

    def powers(e):
        ang = lam_im[None] * dt[None] * e[:, None, None]
        mg = jnp.exp(lam_re[None] * dt[None] * e[:, None, None])
        pw = jnp.stack([mg * jnp.cos(ang), mg * jnp.sin(ang)], axis=1)
        pw = pw.reshape(e.shape[0], 2, n_gb, ns)
        return jnp.transpose(pw, (2, 0, 1, 3))

    nt = tt // SUBLANES
    rows = jnp.arange(SUBLANES, dtype=F32)
    rep = lambda p: jnp.broadcast_to(p[:, :, :, None, :], p.shape[:3] + (SUBLANES, ns))
    seg_steps = rep(powers(nt * 2.0 ** jnp.arange(3, dtype=F32)))
    live = rows[None, :] >= (2.0 ** jnp.arange(3, dtype=F32))[:, None]
    seg_steps = seg_steps * live[None, :, None, :, None].astype(F32)
    seg_rows = jnp.transpose(powers(nt * rows), (0, 2, 1, 3))[:, None]
    tab = jnp.concatenate([rep(powers(jnp.ones((1,), F32))), seg_steps, seg_rows,
                           rep(powers(jnp.full((1,), nt * SUBLANES, F32)))], axis=1)
    pt = jnp.transpose(rep(powers(jnp.arange(1, nt + 1, dtype=F32))), (0, 2, 1, 3, 4))
    new_row = jnp.arange(tt)
    old_row = nt * (new_row % SUBLANES) + new_row // SUBLANES
    perm = (old_row[:, None] == jnp.arange(tt)[None, :]).astype(BF16)
    dsk = d_skip.reshape(n_gb, 1, gpb * gs)
    const2 = lambda g, i: (0, 0)
    return pl.pallas_call(
        functools.partial(_ssm_scan_kernel, tt=tt, ns=ns),
        grid=(n_gb, seq // tt),
        in_specs=[pl.BlockSpec((bsz, tt, LANES), lambda g, i: (0, i, g)),
                  pl.BlockSpec((tt, tt), const2),
                  pl.BlockSpec((tt, tt), const2),
                  pl.BlockSpec((1, LANES, 2 * ns), lambda g, i: (g, 0, 0)),
                  pl.BlockSpec((1, 6, 2, SUBLANES, ns), lambda g, i: (g, 0, 0, 0, 0)),
                  pl.BlockSpec((1, 2, nt, SUBLANES, ns), lambda g, i: (g, 0, 0, 0, 0)),
                  pl.BlockSpec((1, 2 * ns, LANES), lambda g, i: (g, 0, 0)),
                  pl.BlockSpec((1, 1, LANES), lambda g, i: (g, 0, 0))],
        out_specs=pl.BlockSpec((bsz, tt, LANES), lambda g, i: (0, i, g)),
        out_shape=jax.ShapeDtypeStruct((bsz, seq, width), BF16),
        scratch_shapes=[pltpu.VMEM((bsz, 2, SUBLANES, ns), F32)],
        compiler_params=_cparams(("arbitrary", "arbitrary")),
        name="ssm_scan",
    )(u, perm, perm.T, bbt, tab, pt, cc, dsk)


def kernel(x, c, mod_w, mod_b, norm_mix_g, norm_ffn_g, attn_w_in, attn_w_out, ssm_w_in,
           ssm_lam_re, ssm_lam_im, ssm_log_dt, ssm_b_re, ssm_b_im, ssm_c_re, ssm_c_im,
           ssm_d, ssm_w_glu, ffn_w_up, ffn_conv_w, ffn_conv_b, ffn_w_down, final_g):
    bsz, seq, d = x.shape
    depth = mod_w.shape[0]
    n_keep = min(TOPK_MAX, seq // 4)
    tq = 256
    tk = min(512, seq)
    tm = min(1024, seq)
    dff = ffn_w_down.shape[1]
    tn = dff // 2

    mod = _mod_vectors(c, mod_w, mod_b)
    mods = [[mod[i, :, None, k * d:(k + 1) * d] for k in range(6)] for i in range(depth)]
    h = None
    for i in range(depth):
        sh1, sc1, g1, sh2, sc2, g2 = mods[i]
        gm = norm_mix_g[i].reshape(1, d)
        gf = norm_ffn_g[i].reshape(1, d)
        j = i // 2
        if i % 2 == 0:
            q, k, v, qcat, kcat, wi = _attn_proj(x, gm, sh1, sc1, attn_w_in[j], tm=min(512, seq))
            bits = _topk_mask(qcat, kcat, wi, tq=tq, n_keep=n_keep)
            att = _attention(q, k, v, bits, tq=min(512, seq), tk=tk)
            x, h = _proj_residual(att, attn_w_out[j], x, g1, gf, sh2, sc2, tm=tm, glu=False,
                                  name="attn_out")
        else:
            u = _ssm_in(h, ssm_w_in[j], tm=tm)
            y = _ssm_scan(u, ssm_lam_re[j], ssm_lam_im[j], ssm_log_dt[j], ssm_b_re[j],
                          ssm_b_im[j], ssm_c_re[j], ssm_c_im[j], ssm_d[j], tt=min(512, seq))
            x, h = _proj_residual(y, ssm_w_glu[j], x, g1, gf, sh2, sc2, tm=tm, glu=True,
                                  name="ssm_glu")
        a = _ffn_up(h, ffn_w_up[i], ffn_conv_w[i], ffn_conv_b[i], tm=min(1024, seq), tn=tn)
        if i == depth - 1:
            x = _proj_residual(a, ffn_w_down[i], x, g2, final_g.reshape(1, d), tm=tm, glu=False,
                               name="ffn_down_final")
        else:
            sh_n, sc_n = mods[i + 1][0], mods[i + 1][1]
            x, h = _proj_residual(a, ffn_w_down[i], x, g2, norm_mix_g[i + 1].reshape(1, d),
                                  sh_n, sc_n, tm=tm, glu=False, name="ffn_down")
    return x
```

```python
import functools
import math
from typing import NamedTuple

import jax
import jax.numpy as jnp
import numpy as np
from jax import lax
from jax.experimental import pallas as pl
from jax.experimental.pallas import tpu as pltpu

F32 = jnp.float32
BF16 = jnp.bfloat16
I32 = jnp.int32
I16 = jnp.int16

EPS = 1e-6
CHUNK = 64
N_HEADS = 16
HEAD_DIM = 64
IDX_HEADS = 8
IDX_DIM = 64
TOPK_MAX = 256
SSM_GROUP = 16
SSM_STATE = 64
CONV_WIDTH = 3

LANES = 128
SUBLANES = 8
MASK_BITS = 32
VMEM_LIMIT = 56 * 1024 * 1024

INT_MIN = -2 ** 31
KEY_NEG_INF = INT_MIN + 0x7FFFFF
MASK_BIAS = -1e30
M_INIT = -3e38
LOG2E = math.log2(math.e)


def _bf16_pieces(x, n=3):
    out, rest = [], np.float32(x)
    for _ in range(n):
        piece = np.float32(np.asarray(rest, np.float32).astype(BF16).astype(np.float32))
        out.append(float(piece))
        rest = np.float32(rest - piece)
    return out


def _alibi_slope2(h):
    return 2.0 ** (-8.0 * (h + 1) / N_HEADS) * LOG2E

_NT = (((1,), (1,)), ((), ()))


def _cparams(sem):
    return pltpu.CompilerParams(dimension_semantics=sem, vmem_limit_bytes=VMEM_LIMIT)


def _norm_mod(x, g, sh, sc):
    ms = jnp.mean(x * x, axis=-1, keepdims=True)
    y = x * lax.rsqrt(ms + EPS)
    return (y * g) * (1.0 + sc) + sh


def _rms(x, g):
    ms = jnp.mean(x * x, axis=-1, keepdims=True)
    return (x * lax.rsqrt(ms + EPS)) * g


def _mod_kernel(c_ref, w_ref, b_ref, o_ref):
    c = c_ref[...]
    cond = c * (1.0 / (1.0 + jnp.exp(-c)))
    o_ref[...] = jnp.dot(cond, w_ref[...], preferred_element_type=F32,
                         precision=lax.Precision.HIGHEST) + b_ref[...]


def _mod_vectors(c, mod_w, mod_b):
    depth, d, n6 = mod_w.shape
    bsz = c.shape[0]
    rows = SUBLANES
    c_pad = jnp.zeros((rows, d), F32).at[:bsz].set(c)
    out = pl.pallas_call(
        _mod_kernel,
        grid=(depth, n6 // d),
        in_specs=[pl.BlockSpec((rows, d), lambda i, j: (0, 0)),
                  pl.BlockSpec((None, d, d), lambda i, j: (i, 0, j)),
                  pl.BlockSpec((None, 1, d), lambda i, j: (i, 0, j))],
        out_specs=pl.BlockSpec((None, rows, d), lambda i, j: (i, 0, j)),
        out_shape=jax.ShapeDtypeStruct((depth, rows, n6), F32),
        compiler_params=_cparams(("arbitrary", "arbitrary")),
        name="mod_vectors",
    )(c_pad, mod_w, mod_b.reshape(depth, 1, n6))
    return out[:, :bsz, :]


def _attn_proj_kernel(x_ref, g_ref, sh_ref, sc_ref, wqkv_ref, wih_ref, wil_ref,
                      q_ref, k_ref, v_ref, qcat_ref, kcat_ref, wi_ref, *, d):
    h = _norm_mod(x_ref[0], g_ref[...], sh_ref[0], sc_ref[0])
    hb = h.astype(BF16)
    hl = (h - hb.astype(F32)).astype(BF16)
    qkv = jnp.dot(hb, wqkv_ref[...], preferred_element_type=F32)
    q_ref[0] = (qkv[:, :d] * (HEAD_DIM ** -0.5 * LOG2E)).astype(BF16)
    kb = qkv[:, d:2 * d].astype(BF16)
    vb = qkv[:, 2 * d:3 * d].astype(BF16)
    tm = kb.shape[0]
    j = pl.program_id(1) * tm + lax.broadcasted_iota(I32, (tm, LANES), 0)
    ln = lax.broadcasted_iota(I32, (tm, LANES), 1)
    j_hi = ((j // CHUNK) * CHUNK).astype(F32)
    j_lo = (j % CHUNK).astype(F32)
    kx = jnp.where(ln < 2, 1.0, jnp.where(ln < 5, j_hi, jnp.where(ln < 8, j_lo, 0.0))).astype(BF16)
    lo_half = ln < HEAD_DIM
    ones = jnp.ones((tm, LANES), BF16)
    for hp in range(N_HEADS // 2):
        k_ref[0, :, 2 * hp * LANES:(2 * hp + 1) * LANES] = kb[:, hp * LANES:(hp + 1) * LANES]
        k_ref[0, :, (2 * hp + 1) * LANES:(2 * hp + 2) * LANES] = kx
        v2 = vb[:, hp * LANES:(hp + 1) * LANES]
        v_ref[0, :, 2 * hp * LANES:(2 * hp + 1) * LANES] = jnp.where(lo_half, v2, ones)
        v_ref[0, :, (2 * hp + 1) * LANES:(2 * hp + 2) * LANES] = jnp.where(lo_half, ones, v2)
    wih = wih_ref[...]
    idx = (jnp.dot(hb, wih, preferred_element_type=F32)
           + jnp.dot(hl, wih, preferred_element_type=F32)
           + jnp.dot(hb, wil_ref[...], preferred_element_type=F32))
    nq = IDX_HEADS * IDX_DIM
    def split(tile):
        hi = tile.astype(BF16).astype(F32)
        return hi, tile - hi

    for pp in range(IDX_HEADS // 2):
        hi, lo = split(idx[:, pp * LANES:(pp + 1) * LANES] * (IDX_DIM ** -0.5))
        first = jnp.where(lo_half, hi, pltpu.roll(lo, IDX_DIM, axis=1)).astype(BF16)
        second = jnp.where(lo_half, pltpu.roll(hi, IDX_DIM, axis=1), lo).astype(BF16)
        for e, piece in enumerate((first, second)):
            hh = 2 * pp + e
            qcat_ref[0, :, 2 * hh * LANES:(2 * hh + 1) * LANES] = piece
            qcat_ref[0, :, (2 * hh + 1) * LANES:(2 * hh + 2) * LANES] = piece
    tail = idx[:, nq:nq + LANES]
    khi, klo = split(tail)
    kcat_ref[0, :, :LANES] = jnp.where(lo_half, khi, pltpu.roll(khi, IDX_DIM, axis=1)).astype(BF16)
    kcat_ref[0, :, LANES:] = jnp.where(lo_half, klo, pltpu.roll(klo, IDX_DIM, axis=1)).astype(BF16)
    wi_ref[0] = jnp.where(ln < IDX_HEADS, pltpu.roll(tail, IDX_DIM, axis=1) * (IDX_HEADS ** -0.5), 0.0)


def _attn_proj(x, g, sh, sc, w_in, *, tm):
    bsz, seq, d = x.shape
    nq = IDX_HEADS * IDX_DIM
    wqkv = w_in[:, :3 * d].astype(BF16)
    w_idx = w_in[:, 3 * d:]
    w_idx = jnp.pad(w_idx, ((0, 0), (0, nq + LANES - w_idx.shape[1])))
    w_hi = w_idx.astype(BF16)
    w_lo = (w_idx - w_hi.astype(F32)).astype(BF16)
    n_idx = w_idx.shape[1]
    row = lambda b, i: (b, i, 0)
    const = lambda b, i: (0, 0)
    per_b = lambda b, i: (b, 0, 0)
    return pl.pallas_call(
        functools.partial(_attn_proj_kernel, d=d),
        grid=(bsz, seq // tm),
        in_specs=[pl.BlockSpec((1, tm, d), row),
                  pl.BlockSpec((1, d), const),
                  pl.BlockSpec((1, 1, d), per_b),
                  pl.BlockSpec((1, 1, d), per_b),
                  pl.BlockSpec((d, 3 * d), const, pipeline_mode=pl.Buffered(1)),
                  pl.BlockSpec((d, n_idx), const, pipeline_mode=pl.Buffered(1)),
                  pl.BlockSpec((d, n_idx), const, pipeline_mode=pl.Buffered(1))],
        out_specs=[pl.BlockSpec((1, tm, d), row),
                   pl.BlockSpec((1, tm, 2 * d), row),
                   pl.BlockSpec((1, tm, 2 * d), row),
                   pl.BlockSpec((1, tm, 2 * IDX_HEADS * LANES), row),
                   pl.BlockSpec((1, tm, 2 * LANES), row),
                   pl.BlockSpec((1, tm, LANES), row)],
        out_shape=[jax.ShapeDtypeStruct((bsz, seq, d), BF16),
                   jax.ShapeDtypeStruct((bsz, seq, 2 * d), BF16),
                   jax.ShapeDtypeStruct((bsz, seq, 2 * d), BF16),
                   jax.ShapeDtypeStruct((bsz, seq, 2 * IDX_HEADS * LANES), BF16),
                   jax.ShapeDtypeStruct((bsz, seq, 2 * LANES), BF16),
                   jax.ShapeDtypeStruct((bsz, seq, LANES), F32)],
        compiler_params=_cparams(("arbitrary", "arbitrary")),
        name="attn_proj",
    )(x, g, sh, sc, wqkv, w_hi, w_lo)


def _topk_mask_kernel(qcat_ref, kcat_ref, wi_ref, bits_ref, hi_ref, lo_ref, pk_ref,
                      *, tq, tkr, n_keep, cr, unroll, tie_bits):
    qi = pl.program_id(1)
    n_adm = (qi + 1) * tq
    t_chunk = (qi * tq + lax.broadcasted_iota(I32, (1, tq), 1)) // CHUNK
    w_t = wi_ref[0].T

    def score_body(kt, carry):
        r0 = pl.multiple_of(kt * tkr, tkr)
        kc = kcat_ref[0, pl.ds(r0, tkr), :]
        sc = jnp.zeros((tkr, tq), F32)
        for hh in range(IDX_HEADS):
            lg = lax.dot_general(kc, qcat_ref[0, :, 2 * hh * LANES:(2 * hh + 2) * LANES],
                                 _NT, preferred_element_type=F32)
            sc = sc + w_t[hh:hh + 1, :] * jnp.maximum(lg, 0.0)
        sc = sc + 0.0
        raw = lax.bitcast_convert_type(sc, I32)
        key = raw ^ ((raw >> 31) & 0x7FFFFFFF)
        j_chunk = (r0 + lax.broadcasted_iota(I32, (tkr, 1), 0)) // CHUNK
        key = jnp.where(j_chunk <= t_chunk, key, KEY_NEG_INF)
        hi_ref[pl.ds(r0, tkr), :] = (key >> 16).astype(I16)
        lo_ref[pl.ds(r0, tkr), :] = ((key & 0xFFFF) - 2 ** 15).astype(I16)
        return carry

    lax.fori_loop(0, (n_adm + tkr - 1) // tkr, score_body, 0)

    n_ch = n_adm // (cr * unroll)
    one, zero = jnp.int16(1), jnp.int16(0)

    def count(pred):
        def body(c, acc):
            for uu in range(unroll):
                r0 = pl.multiple_of((c * unroll + uu) * cr, cr)
                acc = acc + jnp.where(pred(r0), one, zero)
            return acc
        acc = lax.fori_loop(0, n_ch, body, jnp.zeros((cr, tq), I16))
        return jnp.sum(acc.astype(I32), axis=0, keepdims=True)

    def hi_at(r0):
        return hi_ref[pl.ds(r0, cr), :]

    def lo_at(r0):
        return lo_ref[pl.ds(r0, cr), :]

    def bcast16(v):
        return jnp.broadcast_to(v, (cr, tq)).astype(I16)

    def hi_round(i, carry):
        thr, c_thr, c_gt = carry
        cand = thr + jnp.left_shift(jnp.int32(1), 15 - i)
        cand16 = bcast16(cand)
        cnt = count(lambda r0: hi_at(r0) >= cand16)
        ok = cnt >= n_keep
        return jnp.where(ok, cand, thr), jnp.where(ok, cnt, c_thr), jnp.where(ok, c_gt, cnt)

    thr_hi, c_ge_hi, c_above = lax.fori_loop(
        0, 16, hi_round,
        (jnp.full((1, tq), -2 ** 15, I32), jnp.full((1, tq), n_adm, I32),
         jnp.zeros((1, tq), I32)))
    thr_hi16 = bcast16(thr_hi)

    def bucket_body(c, carry):
        r0 = pl.multiple_of(c * cr, cr)
        lo_ref[pl.ds(r0, cr), :] = jnp.where(hi_at(r0) == thr_hi16, lo_at(r0),
                                             jnp.int16(-2 ** 15))
        return carry

    lax.fori_loop(0, n_adm // cr, bucket_body, 0)

    def lo_round(i, carry):
        thr, c_thr, c_gt = carry
        cand = thr + jnp.left_shift(jnp.int32(1), 15 - i)
        cand16 = bcast16(cand)
        cnt = c_above + count(lambda r0: lo_at(r0) >= cand16)
        ok = cnt >= n_keep
        return jnp.where(ok, cand, thr), jnp.where(ok, cnt, c_thr), jnp.where(ok, c_gt, cnt)

    thr_lo, c_thr, c_gt = lax.fori_loop(
        0, 16, lo_round, (jnp.full((1, tq), -2 ** 15, I32), c_ge_hi, c_above))
    thr_lo16 = bcast16(thr_lo)

    need_ties = jnp.max(c_thr) > n_keep
    n_tied_ok = n_keep - c_gt

    def tied(r0):
        return (lo_at(r0) == thr_lo16) & (hi_at(r0) == thr_hi16)

    def tie_round(i, cut):
        cand = cut + jnp.left_shift(jnp.int32(1), tie_bits - 1 - i)
        cand16 = bcast16(cand)
        cnt = count(lambda r0: tied(r0)
                    & ((r0 + lax.broadcasted_iota(I32, (cr, tq), 0)).astype(I16) < cand16))
        return jnp.where(cnt <= n_tied_ok, cand, cut)

    cut = lax.fori_loop(0, jnp.where(need_ties, tie_bits, 0), tie_round,
                        jnp.zeros((1, tq), I32))
    cut = jnp.where(need_ties, cut, jnp.int32(2 ** 15 - 1))
    cut16 = bcast16(cut)

    pk_ref[...] = jnp.zeros(pk_ref.shape, I32)
    n_sub = LANES // cr

    def pack_body(c, carry):
        pieces = []
        for uu in range(n_sub):
            r0 = pl.multiple_of(c * LANES + uu * cr, cr)
            j16 = (r0 + lax.broadcasted_iota(I32, (cr, tq), 0)).astype(I16)
            sel = ((hi_at(r0) > thr_hi16) | (lo_at(r0) > thr_lo16)
                   | (tied(r0) & (j16 < cut16)))
            adm16 = bcast16(((r0 // CHUNK) <= t_chunk).astype(I32))
            pieces.append(jnp.where(sel, adm16, zero).astype(I32))
        sel32 = jnp.concatenate(pieces, axis=0)
        w0 = pl.multiple_of((c // MASK_BITS) * LANES, LANES)
        pk_ref[pl.ds(w0, LANES), :] = pk_ref[pl.ds(w0, LANES), :] | jnp.left_shift(
            sel32, c % MASK_BITS)
        return carry

    lax.fori_loop(0, n_adm // LANES, pack_body, 0)
    bits_ref[0] = pk_ref[...].T


def _topk_mask(qcat, kcat, wi, *, tq, n_keep):
    bsz, seq, _ = qcat.shape
    assert seq < 2 ** 15, "key indices are compared as int16"
    n_words = -(-seq // (MASK_BITS * LANES))
    tie_bits = max(1, int(seq).bit_length())
    return pl.pallas_call(
        functools.partial(_topk_mask_kernel, tq=tq, tkr=min(2 * tq, seq), n_keep=n_keep,
                          cr=CHUNK, unroll=tq // CHUNK, tie_bits=tie_bits),
        grid=(bsz, seq // tq),
        in_specs=[pl.BlockSpec((1, tq, qcat.shape[2]), lambda b, i: (b, i, 0)),
                  pl.BlockSpec((1, seq, kcat.shape[2]), lambda b, i: (b, 0, 0)),
                  pl.BlockSpec((1, tq, LANES), lambda b, i: (b, i, 0))],
        out_specs=pl.BlockSpec((1, tq, n_words * LANES), lambda b, i: (b, i, 0)),
        out_shape=jax.ShapeDtypeStruct((bsz, seq, n_words * LANES), I32),
        scratch_shapes=[pltpu.VMEM((seq, tq), I16),
                        pltpu.VMEM((seq, tq), I16),
                        pltpu.VMEM((n_words * LANES, tq), I32)],
        compiler_params=_cparams(("arbitrary", "arbitrary")),
        name="topk_mask",
    )(qcat, kcat, wi)


def _alibi_q_lanes(t_f, is_lane0, is_lane1, slope2):
    a = -slope2 * t_f
    a_hi = a.astype(BF16).astype(F32)
    lane = lax.broadcasted_iota(I32, (1, LANES), 1)
    const = jnp.zeros((1, LANES), F32)
    for n, piece in enumerate(_bf16_pieces(slope2)):
        const = jnp.where((lane == 2 + n) | (lane == 5 + n), piece, const)
    return jnp.where(is_lane0, a_hi, jnp.where(is_lane1, a - a_hi, const)).astype(BF16)


def _attn_kernel(q_ref, k_ref, v_ref, bits_ref, o_ref, qx_ref, m_ref, acc_ref, mb_ref,
                 corr_ref, *, tq, tk, nk):
    qi = pl.program_id(1)
    kt = pl.program_id(2)
    last = ((qi + 1) * tq - 1) // tk
    n_sub = tk // LANES
    lane = lax.broadcasted_iota(I32, (tq, LANES), 1)
    lo_half = lane < HEAD_DIM

    @pl.when(kt == 0)
    def _init():
        m_ref[...] = jnp.full(m_ref.shape, M_INIT, F32)
        acc_ref[...] = jnp.zeros(acc_ref.shape, F32)
        t_f = (qi * tq + lax.broadcasted_iota(I32, (tq, LANES), 0)).astype(F32)
        for h in range(N_HEADS):
            q2 = q_ref[0, :, (h // 2) * LANES:(h // 2 + 1) * LANES]
            zero = jnp.zeros_like(q2)
            rows = slice((h % 2) * tq, (h % 2 + 1) * tq)
            qx_ref[h // 2, rows, :LANES] = (jnp.where(lo_half, q2, zero) if h % 2 == 0
                                            else jnp.where(lo_half, zero, q2))
            qx_ref[h // 2, rows, LANES:] = _alibi_q_lanes(t_f, lane == 0, lane == 1,
                                                          _alibi_slope2(h))

    def step(diag):
        bits = bits_ref[0]
        sel = jnp.concatenate(
            [(bits >> ((kt * n_sub + i) % MASK_BITS)) & 1 for i in range(n_sub)], axis=1)
        mb_ref[...] = jnp.where(sel == 1, 0.0, MASK_BIAS)
        if diag:
            t = qi * tq + lax.broadcasted_iota(I32, (tq, 1), 0)
            j = kt * tk + lax.broadcasted_iota(I32, (1, tk), 1)
            corr_ref[...] = (2 * jnp.maximum(j - t, 0)).astype(F32)
        for h in range(N_HEADS):
            hp = h // 2
            if h % 2 == 0:
                s_pair = lax.dot_general(qx_ref[hp],
                                         k_ref[0, :, 2 * hp * LANES:(2 * hp + 2) * LANES],
                                         _NT, preferred_element_type=F32)
            s = s_pair[(h % 2) * tq:(h % 2 + 1) * tq]
            if diag:
                s = s + (mb_ref[...] - _alibi_slope2(h) * corr_ref[...])
            else:
                s = s + mb_ref[...]
            m_old = m_ref[h]
            m_new = jnp.maximum(m_old, jnp.max(s, axis=1, keepdims=True))
            p = jnp.exp2(s - jnp.tile(m_new, (1, n_sub)))
            m_ref[h] = m_new
            acc_ref[h] = (jnp.exp2(m_old - m_new) * acc_ref[h]
                          + jnp.dot(p.astype(BF16), v_ref[0, :, h * LANES:(h + 1) * LANES],
                                    preferred_element_type=F32))

    pl.when(kt < last)(functools.partial(step, False))
    pl.when(kt == last)(functools.partial(step, True))

    @pl.when(kt == nk - 1)
    def _finish():
        for hp in range(N_HEADS // 2):
            a0 = acc_ref[2 * hp]
            a1 = acc_ref[2 * hp + 1]
            o0 = a0 / pltpu.roll(a0, HEAD_DIM, axis=1)
            o1 = a1 / pltpu.roll(a1, HEAD_DIM, axis=1)
            o_ref[0, :, hp * LANES:(hp + 1) * LANES] = jnp.where(lo_half, o0, o1).astype(o_ref.dtype)


def _attention(q, k_ext, v_ext, bits, *, tq, tk):
    bsz, seq, d = q.shape
    nk = seq // tk
    n_sub = tk // LANES

    def last_kt(i):
        return ((i + 1) * tq - 1) // tk

    kv_map = lambda b, i, t: (b, jnp.minimum(t, last_kt(i)), 0)
    bits_map = lambda b, i, t: (b, i, (jnp.minimum(t, last_kt(i)) * n_sub) // MASK_BITS)
    return pl.pallas_call(
        functools.partial(_attn_kernel, tq=tq, tk=tk, nk=nk),
        grid=(bsz, seq // tq, nk),
        in_specs=[pl.BlockSpec((1, tq, d), lambda b, i, t: (b, i, 0)),
                  pl.BlockSpec((1, tk, 2 * d), kv_map),
                  pl.BlockSpec((1, tk, 2 * d), kv_map),
                  pl.BlockSpec((1, tq, LANES), bits_map)],
        out_specs=pl.BlockSpec((1, tq, d), lambda b, i, t: (b, i, 0)),
        out_shape=jax.ShapeDtypeStruct((bsz, seq, d), BF16),
        scratch_shapes=[pltpu.VMEM((N_HEADS // 2, 2 * tq, 2 * LANES), BF16),
                        pltpu.VMEM((N_HEADS, tq, LANES), F32),
                        pltpu.VMEM((N_HEADS, tq, LANES), F32),
                        pltpu.VMEM((tq, tk), F32),
                        pltpu.VMEM((tq, tk), F32)],
        compiler_params=_cparams(("arbitrary", "arbitrary", "arbitrary")),
        name="masked_attention",
    )(q, k_ext, v_ext, bits)


def _sigmoid(z):
    return 0.5 * jnp.tanh(0.5 * z) + 0.5


def _proj_residual_kernel(y_ref, w_ref, x_ref, gate_ref, g_ref, *rest, glu, final, d):
    r = jnp.dot(y_ref[0], w_ref[...], preferred_element_type=F32)
    if glu:
        r = r[:, :d] * _sigmoid(r[:, d:])
    x_new = x_ref[0] + gate_ref[0] * r
    if final:
        (o_ref,) = rest
        o_ref[0] = _rms(x_new, g_ref[...])
    else:
        sh_ref, sc_ref, o_ref, h_ref = rest
        o_ref[0] = x_new
        h_ref[0] = _norm_mod(x_new, g_ref[...], sh_ref[0], sc_ref[0]).astype(h_ref.dtype)


def _proj_residual(y, w, x, gate, g, sh=None, sc=None, *, tm, glu, name):
    bsz, seq, d = x.shape
    kdim, n = w.shape
    final = sh is None
    row = lambda b, i: (b, i, 0)
    per_b = lambda b, i: (b, 0, 0)
    const = lambda b, i: (0, 0)
    in_specs = [pl.BlockSpec((1, tm, kdim), row),
                pl.BlockSpec((kdim, n), const, pipeline_mode=pl.Buffered(1)),
                pl.BlockSpec((1, tm, d), row),
                pl.BlockSpec((1, 1, d), per_b),
                pl.BlockSpec((1, d), const)]
    args = [y, w.astype(BF16), x, gate, g]
    out_specs = [pl.BlockSpec((1, tm, d), row)]
    out_shape = [jax.ShapeDtypeStruct((bsz, seq, d), F32)]
    if not final:
        in_specs += [pl.BlockSpec((1, 1, d), per_b), pl.BlockSpec((1, 1, d), per_b)]
        args += [sh, sc]
        out_specs.append(pl.BlockSpec((1, tm, d), row))
        out_shape.append(jax.ShapeDtypeStruct((bsz, seq, d), BF16))
    out = pl.pallas_call(
        functools.partial(_proj_residual_kernel, glu=glu, final=final, d=d),
        grid=(bsz, seq // tm),
        in_specs=in_specs,
        out_specs=out_specs,
        out_shape=out_shape,
        compiler_params=_cparams(("arbitrary", "arbitrary")),
        name=name,
    )(*args)
    return out[0] if final else out


def _ffn_up_kernel(h_ref, wv_ref, wg_ref, cwv_ref, cwg_ref,
                   cbv_ref, cbg_ref, a_ref, zv_ref, zg_ref, *, tm):
    i = pl.program_id(2)
    ng = tm // SUBLANES
    tn = a_ref.shape[2]

    @pl.when(i == 0)
    def _seq_start():
        zv_ref[...] = jnp.zeros(zv_ref.shape, F32)
        zg_ref[...] = jnp.zeros(zg_ref.shape, F32)

    h = h_ref[0]
    sub = lax.broadcasted_iota(I32, (ng, SUBLANES, tn), 1)

    def conv(w_ref, carry_ref, cw_ref, cb_ref):
        z = jnp.dot(h, w_ref[...], preferred_element_type=F32).reshape(ng, SUBLANES, tn)
        z_all = jnp.concatenate([carry_ref[...][None], z], axis=0)
        carry_ref[...] = z[ng - 1]
        out = cb_ref[...] + cw_ref[CONV_WIDTH - 1:CONV_WIDTH, :] * z
        for back in range(1, CONV_WIDTH):
            rot = pltpu.roll(z_all, back, axis=1)
            shifted = jnp.where(sub < back, rot[:ng], rot[1:])
            out = out + cw_ref[CONV_WIDTH - 1 - back:CONV_WIDTH - back, :] * shifted
        return out

    val = conv(wv_ref, zv_ref, cwv_ref, cbv_ref)
    gt = conv(wg_ref, zg_ref, cwg_ref, cbg_ref)
    hg = 0.5 * gt
    hv = hg * val
    a_ref[0] = (hv + hv * jnp.tanh(hg)).reshape(tm, tn).astype(a_ref.dtype)


def _ffn_up(h, w_up, conv_w, conv_b, *, tm, tn):
    bsz, seq, d = h.shape
    dff = w_up.shape[1] // 2
    nj = dff // tn
    wv = w_up[:, :dff].astype(BF16)
    wg = w_up[:, dff:].astype(BF16)
    cwv = jnp.pad(conv_w[:, :dff], ((0, SUBLANES - CONV_WIDTH), (0, 0)))
    cwg = jnp.pad(conv_w[:, dff:], ((0, SUBLANES - CONV_WIDTH), (0, 0)))
    cbv = conv_b[:dff].reshape(1, dff)
    cbg = conv_b[dff:].reshape(1, dff)
    col = lambda j, b, i: (0, j)
    return pl.pallas_call(
        functools.partial(_ffn_up_kernel, tm=tm),
        grid=(nj, bsz, seq // tm),
        in_specs=[pl.BlockSpec((1, tm, d), lambda j, b, i: (b, i, 0)),
                  pl.BlockSpec((d, tn), col),
                  pl.BlockSpec((d, tn), col),
                  pl.BlockSpec((SUBLANES, tn), col),
                  pl.BlockSpec((SUBLANES, tn), col),
                  pl.BlockSpec((1, tn), col),
                  pl.BlockSpec((1, tn), col)],
        out_specs=pl.BlockSpec((1, tm, tn), lambda j, b, i: (b, i, j)),
        out_shape=jax.ShapeDtypeStruct((bsz, seq, dff), BF16),
        scratch_shapes=[pltpu.VMEM((SUBLANES, tn), F32),
                        pltpu.VMEM((SUBLANES, tn), F32)],
        compiler_params=_cparams(("arbitrary", "arbitrary", "arbitrary")),
        name="ffn_up",
    )(h, wv, wg, cwv, cwg, cbv, cbg)


def _ssm_in_kernel(h_ref, w_ref, u_ref):
    u_ref[0] = jnp.dot(h_ref[0], w_ref[...], preferred_element_type=F32)


def _ssm_in(h, w, *, tm):
    bsz, seq, d = h.shape
    n = w.shape[1]
    row = lambda b, i: (b, i, 0)
    return pl.pallas_call(
        _ssm_in_kernel,
        grid=(bsz, seq // tm),
        in_specs=[pl.BlockSpec((1, tm, d), row),
                  pl.BlockSpec((d, n), lambda b, i: (0, 0))],
        out_specs=pl.BlockSpec((1, tm, n), row),
        out_shape=jax.ShapeDtypeStruct((bsz, seq, n), F32),
        compiler_params=_cparams(("arbitrary", "arbitrary")),
        name="ssm_in",
    )(h, w.astype(BF16))


def _ssm_scan_kernel(u_ref, perm_ref, permt_ref, bbt_ref, tab_ref, pt_ref, cc_ref, dskip_ref,
                     y_ref, carry_ref, *, tt, ns):
    i = pl.program_id(1)
    nt = tt // SUBLANES

    @pl.when(i == 0)
    def _seq_start():
        carry_ref[...] = jnp.zeros(carry_ref.shape, F32)

    def cmul_add(x_re, x_im, a_re, a_im, y_re, y_im):
        return x_re + a_re * y_re - a_im * y_im, x_im + a_re * y_im + a_im * y_re

    nb = u_ref.shape[0]
    u = u_ref[...]
    lanes = lambda rows: jnp.concatenate([rows[b] for b in range(nb)], axis=1)
    unlanes = lambda wide: jnp.concatenate(
        [wide[:, b * LANES:(b + 1) * LANES] for b in range(nb)], axis=0)
    up = jnp.dot(perm_ref[...], lanes(u).astype(BF16), preferred_element_type=F32).astype(BF16)
    bu = jnp.dot(unlanes(up), bbt_ref[0], preferred_element_type=F32)
    bu_re = bu[:, :ns].reshape(nb, nt, SUBLANES, ns)
    bu_im = bu[:, ns:].reshape(nb, nt, SUBLANES, ns)
    a_re, a_im = tab_ref[0, 0, 0], tab_ref[0, 0, 1]
    h_re = jnp.zeros((nb, SUBLANES, ns), F32)
    h_im = jnp.zeros((nb, SUBLANES, ns), F32)
    loc_re, loc_im = [], []
    for g in range(nt):
        h_re, h_im = cmul_add(bu_re[:, g], bu_im[:, g], a_re, a_im, h_re, h_im)
        loc_re.append(h_re)
        loc_im.append(h_im)
    s_re, s_im = h_re, h_im
    for kk in range(3):
        s_re, s_im = cmul_add(s_re, s_im, tab_ref[0, 1 + kk, 0], tab_ref[0, 1 + kk, 1],
                              pltpu.roll(s_re, 1 << kk, axis=1),
                              pltpu.roll(s_im, 1 << kk, axis=1))
    c_re, c_im = carry_ref[:, 0], carry_ref[:, 1]
    first = lax.broadcasted_iota(I32, (nb, SUBLANES, ns), 1) == 0
    hin_re, hin_im = cmul_add(jnp.where(first, 0.0, pltpu.roll(s_re, 1, axis=1)),
                              jnp.where(first, 0.0, pltpu.roll(s_im, 1, axis=1)),
                              tab_ref[0, 4, 0], tab_ref[0, 4, 1], c_re, c_im)
    c_re, c_im = cmul_add(jnp.broadcast_to(s_re[:, SUBLANES - 1:], (nb, SUBLANES, ns)),
                          jnp.broadcast_to(s_im[:, SUBLANES - 1:], (nb, SUBLANES, ns)),
                          tab_ref[0, 5, 0], tab_ref[0, 5, 1], c_re, c_im)
    carry_ref[:, 0] = c_re
    carry_ref[:, 1] = c_im
    hg_re, hg_im = cmul_add(jnp.stack(loc_re, axis=1), jnp.stack(loc_im, axis=1),
                            pt_ref[0, 0], pt_ref[0, 1], hin_re[:, None], hin_im[:, None])
    hcat = jnp.concatenate([hg_re.reshape(nb * tt, ns), hg_im.reshape(nb * tt, ns)],
                           axis=1).astype(BF16)
    yp = jnp.dot(hcat, cc_ref[0], preferred_element_type=F32)
    yp = lanes(yp.reshape(nb, tt, LANES))
    yp_hi = yp.astype(BF16)
    yp_lo = (yp - yp_hi.astype(F32)).astype(BF16)
    y = (jnp.dot(permt_ref[...], yp_hi, preferred_element_type=F32)
         + jnp.dot(permt_ref[...], yp_lo, preferred_element_type=F32))
    y = unlanes(y).reshape(nb, tt, LANES) + dskip_ref[0] * u
    y_ref[...] = jax.nn.gelu(y).astype(y_ref.dtype)


def _ssm_scan(u, lam_re, lam_im, log_dt, b_re, b_im, c_re, c_im, d_skip, *, tt):
    bsz, seq, width = u.shape
    n_groups, n_state = lam_re.shape
    gs = SSM_GROUP
    gpb = LANES // gs
    n_gb = n_groups // gpb
    ns = gpb * n_state
    dt = jnp.exp(log_dt)[:, None]
    mag = jnp.exp(lam_re * dt)
    a_re = mag * jnp.cos(lam_im * dt)
    a_im = mag * jnp.sin(lam_im * dt)
    den = lam_re * lam_re + lam_im * lam_im
    coef_re = ((a_re - 1.0) * lam_re + a_im * lam_im) / den
    coef_im = (a_im * lam_re - (a_re - 1.0) * lam_im) / den
    bb_re = coef_re[..., None] * b_re - coef_im[..., None] * b_im
    bb_im = coef_re[..., None] * b_im + coef_im[..., None] * b_re
    eye = jnp.eye(gpb, dtype=F32)

    def block_diag_in(bb):
        t = bb.reshape(n_gb, gpb, n_state, gs)
        return jnp.einsum('ngpc,gh->ngchp', t, eye).reshape(n_gb, gpb * gs, gpb * n_state)

    def block_diag_out(cm):
        t = cm.reshape(n_gb, gpb, gs, n_state)
        return jnp.einsum('ngcp,gh->ngphc', t, eye).reshape(n_gb, gpb * n_state, gpb * gs)

    bbt = jnp.concatenate([block_diag_in(bb_re), block_diag_in(bb_im)], axis=2).astype(BF16)
    cc = jnp.concatenate([block_diag_out(c_re), -block_diag_out(c_im)], axis=1).astype(BF16)

    def powers(e):
        ang = lam_im[None] * dt[None] * e[:, None, None]
        mg = jnp.exp(lam_re[None] * dt[None] * e[:, None, None])
        pw = jnp.stack([mg * jnp.cos(ang), mg * jnp.sin(ang)], axis=1)
        pw = pw.reshape(e.shape[0], 2, n_gb, ns)
        return jnp.transpose(pw, (2, 0, 1, 3))

    nt = tt // SUBLANES
    rows = jnp.arange(SUBLANES, dtype=F32)
    rep = lambda p: jnp.broadcast_to(p[:, :, :, None, :], p.shape[:3] + (SUBLANES, ns))
    seg_steps = rep(powers(nt * 2.0 ** jnp.arange(3, dtype=F32)))
    live = rows[None, :] >= (2.0 ** jnp.arange(3, dtype=F32))[:, None]
    seg_steps = seg_steps * live[None, :, None, :, None].astype(F32)
    seg_rows = jnp.transpose(powers(nt * rows), (0, 2, 1, 3))[:, None]
    tab = jnp.concatenate([rep(powers(jnp.ones((1,), F32))), seg_steps, seg_rows,
                           rep(powers(jnp.full((1,), nt * SUBLANES, F32)))], axis=1)
    pt = jnp.transpose(rep(powers(jnp.arange(1, nt + 1, dtype=F32))), (0, 2, 1, 3, 4))
    new_row = jnp.arange(tt)
    old_row = nt * (new_row % SUBLANES) + new_row // SUBLANES
    perm = (old_row[:, None] == jnp.arange(tt)[None, :]).astype(BF16)
    dsk = d_skip.reshape(n_gb, 1, gpb * gs)
    const2 = lambda g, i: (0, 0)
    return pl.pallas_call(
        functools.partial(_ssm_scan_kernel, tt=tt, ns=ns),
        grid=(n_gb, seq // tt),
        in_specs=[pl.BlockSpec((bsz, tt, LANES), lambda g, i: (0, i, g)),
                  pl.BlockSpec((tt, tt), const2),
                  pl.BlockSpec((tt, tt), const2),
                  pl.BlockSpec((1, LANES, 2 * ns), lambda g, i: (g, 0, 0)),
                  pl.BlockSpec((1, 6, 2, SUBLANES, ns), lambda g, i: (g, 0, 0, 0, 0)),
                  pl.BlockSpec((1, 2, nt, SUBLANES, ns), lambda g, i: (g, 0, 0, 0, 0)),
                  pl.BlockSpec((1, 2 * ns, LANES), lambda g, i: (g, 0, 0)),
                  pl.BlockSpec((1, 1, LANES), lambda g, i: (g, 0, 0))],
        out_specs=pl.BlockSpec((bsz, tt, LANES), lambda g, i: (0, i, g)),
        out_shape=jax.ShapeDtypeStruct((bsz, seq, width), BF16),
        scratch_shapes=[pltpu.VMEM((bsz, 2, SUBLANES, ns), F32)],
        compiler_params=_cparams(("arbitrary", "arbitrary")),
        name="ssm_scan",
    )(u, perm, perm.T, bbt, tab, pt, cc, dsk)


class _Tiles(NamedTuple):
    proj_rows: int
    topk_q: int
    attn_q: int
    attn_k: int
    rows: int
    ffn_cols: int
    scan_rows: int


def _tiles(seq, dff):
    return _Tiles(proj_rows=min(512, seq), topk_q=2 * LANES, attn_q=min(512, seq),
                  attn_k=min(512, seq), rows=min(1024, seq), ffn_cols=dff // 2,
                  scan_rows=min(512, seq))


def kernel(x, c, mod_w, mod_b, norm_mix_g, norm_ffn_g, attn_w_in, attn_w_out, ssm_w_in,
           ssm_lam_re, ssm_lam_im, ssm_log_dt, ssm_b_re, ssm_b_im, ssm_c_re, ssm_c_im,
           ssm_d, ssm_w_glu, ffn_w_up, ffn_conv_w, ffn_conv_b, ffn_w_down, final_g):
    bsz, seq, d = x.shape
    depth = mod_w.shape[0]
    n_keep = min(TOPK_MAX, seq // 4)
    t = _tiles(seq, ffn_w_down.shape[1])

    mod = _mod_vectors(c, mod_w, mod_b)
    mods = [[mod[i, :, None, k * d:(k + 1) * d] for k in range(6)] for i in range(depth)]
    h = None
    for i in range(depth):
        sh1, sc1, g1, sh2, sc2, g2 = mods[i]
        gm = norm_mix_g[i].reshape(1, d)
        gf = norm_ffn_g[i].reshape(1, d)
        j = i // 2
        if i % 2 == 0:
            q, k, v, qcat, kcat, wi = _attn_proj(x, gm, sh1, sc1, attn_w_in[j], tm=t.proj_rows)
            bits = _topk_mask(qcat, kcat, wi, tq=t.topk_q, n_keep=n_keep)
            att = _attention(q, k, v, bits, tq=t.attn_q, tk=t.attn_k)
            x, h = _proj_residual(att, attn_w_out[j], x, g1, gf, sh2, sc2, tm=t.rows, glu=False,
                                  name="attn_out")
        else:
            u = _ssm_in(h, ssm_w_in[j], tm=t.rows)
            y = _ssm_scan(u, ssm_lam_re[j], ssm_lam_im[j], ssm_log_dt[j], ssm_b_re[j],
                          ssm_b_im[j], ssm_c_re[j], ssm_c_im[j], ssm_d[j], tt=t.scan_rows)
            x, h = _proj_residual(y, ssm_w_glu[j], x, g1, gf, sh2, sc2, tm=t.rows, glu=True,
                                  name="ssm_glu")
        a = _ffn_up(h, ffn_w_up[i], ffn_conv_w[i], ffn_conv_b[i], tm=t.rows, tn=t.ffn_cols)
        if i == depth - 1:
            x = _proj_residual(a, ffn_w_down[i], x, g2, final_g.reshape(1, d), tm=t.rows,
                               glu=False, name="ffn_down_final")
        else:
            sh_n, sc_n = mods[i + 1][0], mods[i + 1][1]
            x, h = _proj_residual(a, ffn_w_down[i], x, g2, norm_mix_g[i + 1].reshape(1, d),
                                  sh_n, sc_n, tm=t.rows, glu=False, name="ffn_down")
    return x
```

```python
import functools
import math
from typing import NamedTuple

import jax
import jax.numpy as jnp
import numpy as np
from jax import lax
from jax.experimental import pallas as pl
from jax.experimental.pallas import tpu as pltpu

F32 = jnp.float32
BF16 = jnp.bfloat16
I32 = jnp.int32
I16 = jnp.int16

EPS = 1e-6
CHUNK = 64
N_HEADS = 16
HEAD_DIM = 64
IDX_HEADS = 8
IDX_DIM = 64
TOPK_MAX = 256
SSM_GROUP = 16
SSM_STATE = 64
CONV_WIDTH = 3

LANES = 128
SUBLANES = 8
MASK_BITS = 32
VMEM_LIMIT = 56 * 1024 * 1024

INT_MIN = -2 ** 31
KEY_NEG_INF = INT_MIN + 0x7FFFFF
MASK_BIAS = -1e30
M_INIT = -3e38
LOG2E = math.log2(math.e)


def _bf16_pieces(x, n=3):
    out, rest = [], np.float32(x)
    for _ in range(n):
        piece = np.float32(np.asarray(rest, np.float32).astype(BF16).astype(np.float32))
        out.append(float(piece))
        rest = np.float32(rest - piece)
    return out


def _alibi_slope2(h):
    return 2.0 ** (-8.0 * (h + 1) / N_HEADS) * LOG2E

_NT = (((1,), (1,)), ((), ()))


def _cparams(sem):
    return pltpu.CompilerParams(dimension_semantics=sem, vmem_limit_bytes=VMEM_LIMIT)


def _norm_mod(x, g, sh, sc):
    ms = jnp.mean(x * x, axis=-1, keepdims=True)
    y = x * lax.rsqrt(ms + EPS)
    return (y * g) * (1.0 + sc) + sh


def _rms(x, g):
    ms = jnp.mean(x * x, axis=-1, keepdims=True)
    return (x * lax.rsqrt(ms + EPS)) * g


def _mod_kernel(c_ref, w_ref, b_ref, o_ref):
    c = c_ref[...]
    cond = c * (1.0 / (1.0 + jnp.exp(-c)))
    o_ref[...] = jnp.dot(cond, w_ref[...], preferred_element_type=F32,
                         precision=lax.Precision.HIGHEST) + b_ref[...]


def _mod_vectors(c, mod_w, mod_b):
    depth, d, n6 = mod_w.shape
    bsz = c.shape[0]
    rows = SUBLANES
    c_pad = jnp.zeros((rows, d), F32).at[:bsz].set(c)
    out = pl.pallas_call(
        _mod_kernel,
        grid=(depth, n6 // d),
        in_specs=[pl.BlockSpec((rows, d), lambda i, j: (0, 0)),
                  pl.BlockSpec((None, d, d), lambda i, j: (i, 0, j)),
                  pl.BlockSpec((None, 1, d), lambda i, j: (i, 0, j))],
        out_specs=pl.BlockSpec((None, rows, d), lambda i, j: (i, 0, j)),
        out_shape=jax.ShapeDtypeStruct((depth, rows, n6), F32),
        compiler_params=_cparams(("arbitrary", "arbitrary")),
        name="mod_vectors",
    )(c_pad, mod_w, mod_b.reshape(depth, 1, n6))
    return out[:, :bsz, :]


def _attn_proj_kernel(x_ref, g_ref, sh_ref, sc_ref, wqkv_ref, wih_ref, wil_ref,
                      q_ref, k_ref, v_ref, qcat_ref, kcat_ref, wi_ref, *, d):
    h = _norm_mod(x_ref[0], g_ref[...], sh_ref[0], sc_ref[0])
    hb = h.astype(BF16)
    hl = (h - hb.astype(F32)).astype(BF16)
    qkv = jnp.dot(hb, wqkv_ref[...], preferred_element_type=F32)
    q_ref[0] = (qkv[:, :d] * (HEAD_DIM ** -0.5 * LOG2E)).astype(BF16)
    kb = qkv[:, d:2 * d].astype(BF16)
    vb = qkv[:, 2 * d:3 * d].astype(BF16)
    tm = kb.shape[0]
    j = pl.program_id(1) * tm + lax.broadcasted_iota(I32, (tm, LANES), 0)
    ln = lax.broadcasted_iota(I32, (tm, LANES), 1)
    j_hi = ((j // CHUNK) * CHUNK).astype(F32)
    j_lo = (j % CHUNK).astype(F32)
    kx = jnp.where(ln < 2, 1.0, jnp.where(ln < 5, j_hi, jnp.where(ln < 8, j_lo, 0.0))).astype(BF16)
    lo_half = ln < HEAD_DIM
    ones = jnp.ones((tm, LANES), BF16)
    for hp in range(N_HEADS // 2):
        k_ref[0, :, 2 * hp * LANES:(2 * hp + 1) * LANES] = kb[:, hp * LANES:(hp + 1) * LANES]
        k_ref[0, :, (2 * hp + 1) * LANES:(2 * hp + 2) * LANES] = kx
        v2 = vb[:, hp * LANES:(hp + 1) * LANES]
        v_ref[0, :, 2 * hp * LANES:(2 * hp + 1) * LANES] = jnp.where(lo_half, v2, ones)
        v_ref[0, :, (2 * hp + 1) * LANES:(2 * hp + 2) * LANES] = jnp.where(lo_half, ones, v2)
    wih = wih_ref[...]
    idx = (jnp.dot(hb, wih, preferred_element_type=F32)
           + jnp.dot(hl, wih, preferred_element_type=F32)
           + jnp.dot(hb, wil_ref[...], preferred_element_type=F32))
    nq = IDX_HEADS * IDX_DIM
    def split(tile):
        hi = tile.astype(BF16).astype(F32)
        return hi, tile - hi

    for pp in range(IDX_HEADS // 2):
        hi, lo = split(idx[:, pp * LANES:(pp + 1) * LANES] * (IDX_DIM ** -0.5))
        first = jnp.where(lo_half, hi, pltpu.roll(lo, IDX_DIM, axis=1)).astype(BF16)
        second = jnp.where(lo_half, pltpu.roll(hi, IDX_DIM, axis=1), lo).astype(BF16)
        for e, piece in enumerate((first, second)):
            hh = 2 * pp + e
            qcat_ref[0, :, 2 * hh * LANES:(2 * hh + 1) * LANES] = piece
            qcat_ref[0, :, (2 * hh + 1) * LANES:(2 * hh + 2) * LANES] = piece
    tail = idx[:, nq:nq + LANES]
    khi, klo = split(tail)
    kcat_ref[0, :, :LANES] = jnp.where(lo_half, khi, pltpu.roll(khi, IDX_DIM, axis=1)).astype(BF16)
    kcat_ref[0, :, LANES:] = jnp.where(lo_half, klo, pltpu.roll(klo, IDX_DIM, axis=1)).astype(BF16)
    wi_ref[0] = jnp.where(ln < IDX_HEADS, pltpu.roll(tail, IDX_DIM, axis=1) * (IDX_HEADS ** -0.5), 0.0)


def _attn_proj(x, g, sh, sc, w_in, *, tm):
    bsz, seq, d = x.shape
    nq = IDX_HEADS * IDX_DIM
    wqkv = w_in[:, :3 * d].astype(BF16)
    w_idx = w_in[:, 3 * d:]
    w_idx = jnp.pad(w_idx, ((0, 0), (0, nq + LANES - w_idx.shape[1])))
    w_hi = w_idx.astype(BF16)
    w_lo = (w_idx - w_hi.astype(F32)).astype(BF16)
    n_idx = w_idx.shape[1]
    row = lambda b, i: (b, i, 0)
    const = lambda b, i: (0, 0)
    per_b = lambda b, i: (b, 0, 0)
    return pl.pallas_call(
        functools.partial(_attn_proj_kernel, d=d),
        grid=(bsz, seq // tm),
        in_specs=[pl.BlockSpec((1, tm, d), row),
                  pl.BlockSpec((1, d), const),
                  pl.BlockSpec((1, 1, d), per_b),
                  pl.BlockSpec((1, 1, d), per_b),
                  pl.BlockSpec((d, 3 * d), const, pipeline_mode=pl.Buffered(1)),
                  pl.BlockSpec((d, n_idx), const, pipeline_mode=pl.Buffered(1)),
                  pl.BlockSpec((d, n_idx), const, pipeline_mode=pl.Buffered(1))],
        out_specs=[pl.BlockSpec((1, tm, d), row),
                   pl.BlockSpec((1, tm, 2 * d), row),
                   pl.BlockSpec((1, tm, 2 * d), row),
                   pl.BlockSpec((1, tm, 2 * IDX_HEADS * LANES), row),
                   pl.BlockSpec((1, tm, 2 * LANES), row),
                   pl.BlockSpec((1, tm, LANES), row)],
        out_shape=[jax.ShapeDtypeStruct((bsz, seq, d), BF16),
                   jax.ShapeDtypeStruct((bsz, seq, 2 * d), BF16),
                   jax.ShapeDtypeStruct((bsz, seq, 2 * d), BF16),
                   jax.ShapeDtypeStruct((bsz, seq, 2 * IDX_HEADS * LANES), BF16),
                   jax.ShapeDtypeStruct((bsz, seq, 2 * LANES), BF16),
                   jax.ShapeDtypeStruct((bsz, seq, LANES), F32)],
        compiler_params=_cparams(("arbitrary", "arbitrary")),
        name="attn_proj",
    )(x, g, sh, sc, wqkv, w_hi, w_lo)


def _topk_mask_kernel(qcat_ref, kcat_ref, wi_ref, bits_ref, hi_ref, lo_ref, pk_ref,
                      *, tq, tkr, n_keep, cr, unroll, tie_bits):
    qi = pl.program_id(1)
    n_adm = (qi + 1) * tq
    t_chunk = (qi * tq + lax.broadcasted_iota(I32, (1, tq), 1)) // CHUNK
    w_t = wi_ref[0].T

    def score_body(kt, carry):
        r0 = pl.multiple_of(kt * tkr, tkr)
        kc = kcat_ref[0, pl.ds(r0, tkr), :]
        sc = jnp.zeros((tkr, tq), F32)
        for hh in range(IDX_HEADS):
            lg = lax.dot_general(kc, qcat_ref[0, :, 2 * hh * LANES:(2 * hh + 2) * LANES],
                                 _NT, preferred_element_type=F32)
            sc = sc + w_t[hh:hh + 1, :] * jnp.maximum(lg, 0.0)
        sc = sc + 0.0
        raw = lax.bitcast_convert_type(sc, I32)
        key = raw ^ ((raw >> 31) & 0x7FFFFFFF)
        j_chunk = (r0 + lax.broadcasted_iota(I32, (tkr, 1), 0)) // CHUNK
        key = jnp.where(j_chunk <= t_chunk, key, KEY_NEG_INF)
        hi_ref[pl.ds(r0, tkr), :] = (key >> 16).astype(I16)
        lo_ref[pl.ds(r0, tkr), :] = ((key & 0xFFFF) - 2 ** 15).astype(I16)
        return carry

    lax.fori_loop(0, (n_adm + tkr - 1) // tkr, score_body, 0)

    n_ch = n_adm // (cr * unroll)
    one, zero = jnp.int16(1), jnp.int16(0)

    def count(pred):
        def body(c, acc):
            for uu in range(unroll):
                r0 = pl.multiple_of((c * unroll + uu) * cr, cr)
                acc = acc + jnp.where(pred(r0), one, zero)
            return acc
        acc = lax.fori_loop(0, n_ch, body, jnp.zeros((cr, tq), I16))
        return jnp.sum(acc.astype(I32), axis=0, keepdims=True)

    def hi_at(r0):
        return hi_ref[pl.ds(r0, cr), :]

    def lo_at(r0):
        return lo_ref[pl.ds(r0, cr), :]

    def bcast16(v):
        return jnp.broadcast_to(v, (cr, tq)).astype(I16)

    def hi_round(i, carry):
        thr, c_thr, c_gt = carry
        cand = thr + jnp.left_shift(jnp.int32(1), 15 - i)
        cand16 = bcast16(cand)
        cnt = count(lambda r0: hi_at(r0) >= cand16)
        ok = cnt >= n_keep
        return jnp.where(ok, cand, thr), jnp.where(ok, cnt, c_thr), jnp.where(ok, c_gt, cnt)

    thr_hi, c_ge_hi, c_above = lax.fori_loop(
        0, 16, hi_round,
        (jnp.full((1, tq), -2 ** 15, I32), jnp.full((1, tq), n_adm, I32),
         jnp.zeros((1, tq), I32)))
    thr_hi16 = bcast16(thr_hi)

    def bucket_body(c, carry):
        r0 = pl.multiple_of(c * cr, cr)
        lo_ref[pl.ds(r0, cr), :] = jnp.where(hi_at(r0) == thr_hi16, lo_at(r0),
                                             jnp.int16(-2 ** 15))
        return carry

    lax.fori_loop(0, n_adm // cr, bucket_body, 0)

    def lo_round(i, carry):
        thr, c_thr, c_gt = carry
        cand = thr + jnp.left_shift(jnp.int32(1), 15 - i)
        cand16 = bcast16(cand)
        cnt = c_above + count(lambda r0: lo_at(r0) >= cand16)
        ok = cnt >= n_keep
        return jnp.where(ok, cand, thr), jnp.where(ok, cnt, c_thr), jnp.where(ok, c_gt, cnt)

    thr_lo, c_thr, c_gt = lax.fori_loop(
        0, 16, lo_round, (jnp.full((1, tq), -2 ** 15, I32), c_ge_hi, c_above))
    thr_lo16 = bcast16(thr_lo)

    need_ties = jnp.max(c_thr) > n_keep
    n_tied_ok = n_keep - c_gt

    def tied(r0):
        return (lo_at(r0) == thr_lo16) & (hi_at(r0) == thr_hi16)

    def tie_round(i, cut):
        cand = cut + jnp.left_shift(jnp.int32(1), tie_bits - 1 - i)
        cand16 = bcast16(cand)
        cnt = count(lambda r0: tied(r0)
                    & ((r0 + lax.broadcasted_iota(I32, (cr, tq), 0)).astype(I16) < cand16))
        return jnp.where(cnt <= n_tied_ok, cand, cut)

    cut = lax.fori_loop(0, jnp.where(need_ties, tie_bits, 0), tie_round,
                        jnp.zeros((1, tq), I32))
    cut = jnp.where(need_ties, cut, jnp.int32(2 ** 15 - 1))
    cut16 = bcast16(cut)

    pk_ref[...] = jnp.zeros(pk_ref.shape, I32)
    n_sub = LANES // cr

    def pack_body(c, carry):
        pieces = []
        for uu in range(n_sub):
            r0 = pl.multiple_of(c * LANES + uu * cr, cr)
            j16 = (r0 + lax.broadcasted_iota(I32, (cr, tq), 0)).astype(I16)
            sel = ((hi_at(r0) > thr_hi16) | (lo_at(r0) > thr_lo16)
                   | (tied(r0) & (j16 < cut16)))
            adm16 = bcast16(((r0 // CHUNK) <= t_chunk).astype(I32))
            pieces.append(jnp.where(sel, adm16, zero).astype(I32))
        sel32 = jnp.concatenate(pieces, axis=0)
        w0 = pl.multiple_of((c // MASK_BITS) * LANES, LANES)
        pk_ref[pl.ds(w0, LANES), :] = pk_ref[pl.ds(w0, LANES), :] | jnp.left_shift(
            sel32, c % MASK_BITS)
        return carry

    lax.fori_loop(0, n_adm // LANES, pack_body, 0)
    bits_ref[0] = pk_ref[...].T


def _topk_mask(qcat, kcat, wi, *, tq, n_keep):
    bsz, seq, _ = qcat.shape
    assert seq < 2 ** 15, "key indices are compared as int16"
    n_words = -(-seq // (MASK_BITS * LANES))
    tie_bits = max(1, int(seq).bit_length())
    cr = CHUNK // 2
    assert tq % (2 * LANES) == 0 and seq % tq == 0
    return pl.pallas_call(
        functools.partial(_topk_mask_kernel, tq=tq, tkr=tq, n_keep=n_keep,
                          cr=cr, unroll=2 * LANES // cr, tie_bits=tie_bits),
        grid=(bsz, seq // tq),
        in_specs=[pl.BlockSpec((1, tq, qcat.shape[2]), lambda b, i: (b, i, 0)),
                  pl.BlockSpec((1, seq, kcat.shape[2]), lambda b, i: (b, 0, 0)),
                  pl.BlockSpec((1, tq, LANES), lambda b, i: (b, i, 0))],
        out_specs=pl.BlockSpec((1, tq, n_words * LANES), lambda b, i: (b, i, 0)),
        out_shape=jax.ShapeDtypeStruct((bsz, seq, n_words * LANES), I32),
        scratch_shapes=[pltpu.VMEM((seq, tq), I16),
                        pltpu.VMEM((seq, tq), I16),
                        pltpu.VMEM((n_words * LANES, tq), I32)],
        compiler_params=_cparams(("arbitrary", "arbitrary")),
        name="topk_mask",
    )(qcat, kcat, wi)


def _alibi_q_lanes(t_f, is_lane0, is_lane1, slope2):
    a = -slope2 * t_f
    a_hi = a.astype(BF16).astype(F32)
    lane = lax.broadcasted_iota(I32, (1, LANES), 1)
    const = jnp.zeros((1, LANES), F32)
    for n, piece in enumerate(_bf16_pieces(slope2)):
        const = jnp.where((lane == 2 + n) | (lane == 5 + n), piece, const)
    return jnp.where(is_lane0, a_hi, jnp.where(is_lane1, a - a_hi, const)).astype(BF16)


def _attn_kernel(q_ref, k_ref, v_ref, bits_ref, o_ref, qx_ref, m_ref, acc_ref, mb_ref,
                 corr_ref, *, tq, tk, nk):
    qi = pl.program_id(1)
    kt = pl.program_id(2)
    last = ((qi + 1) * tq - 1) // tk
    n_sub = tk // LANES
    lane = lax.broadcasted_iota(I32, (tq, LANES), 1)
    lo_half = lane < HEAD_DIM

    @pl.when(kt == 0)
    def _init():
        m_ref[...] = jnp.full(m_ref.shape, M_INIT, F32)
        acc_ref[...] = jnp.zeros(acc_ref.shape, F32)
        t_f = (qi * tq + lax.broadcasted_iota(I32, (tq, LANES), 0)).astype(F32)
        for h in range(N_HEADS):
            q2 = q_ref[0, :, (h // 2) * LANES:(h // 2 + 1) * LANES]
            zero = jnp.zeros_like(q2)
            rows = slice((h % 2) * tq, (h % 2 + 1) * tq)
            qx_ref[h // 2, rows, :LANES] = (jnp.where(lo_half, q2, zero) if h % 2 == 0
                                            else jnp.where(lo_half, zero, q2))
            qx_ref[h // 2, rows, LANES:] = _alibi_q_lanes(t_f, lane == 0, lane == 1,
                                                          _alibi_slope2(h))

    def step(diag):
        bits = bits_ref[0]
        sel = jnp.concatenate(
            [(bits >> ((kt * n_sub + i) % MASK_BITS)) & 1 for i in range(n_sub)], axis=1)
        mb_ref[...] = jnp.where(sel == 1, 0.0, MASK_BIAS)
        if diag:
            t = qi * tq + lax.broadcasted_iota(I32, (tq, 1), 0)
            j = kt * tk + lax.broadcasted_iota(I32, (1, tk), 1)
            corr_ref[...] = (2 * jnp.maximum(j - t, 0)).astype(F32)
        for h in range(N_HEADS):
            hp = h // 2
            if h % 2 == 0:
                s_pair = lax.dot_general(qx_ref[hp],
                                         k_ref[0, :, 2 * hp * LANES:(2 * hp + 2) * LANES],
                                         _NT, preferred_element_type=F32)
            s = s_pair[(h % 2) * tq:(h % 2 + 1) * tq]
            if diag:
                s = s + (mb_ref[...] - _alibi_slope2(h) * corr_ref[...])
            else:
                s = s + mb_ref[...]
            m_old = m_ref[h]
            m_new = jnp.maximum(m_old, jnp.max(s, axis=1, keepdims=True))
            p = jnp.exp2(s - jnp.tile(m_new, (1, n_sub)))
            m_ref[h] = m_new
            acc_ref[h] = (jnp.exp2(m_old - m_new) * acc_ref[h]
                          + jnp.dot(p.astype(BF16), v_ref[0, :, h * LANES:(h + 1) * LANES],
                                    preferred_element_type=F32))

    pl.when(kt < last)(functools.partial(step, False))
    pl.when(kt == last)(functools.partial(step, True))

    @pl.when(kt == nk - 1)
    def _finish():
        for hp in range(N_HEADS // 2):
            a0 = acc_ref[2 * hp]
            a1 = acc_ref[2 * hp + 1]
            o0 = a0 / pltpu.roll(a0, HEAD_DIM, axis=1)
            o1 = a1 / pltpu.roll(a1, HEAD_DIM, axis=1)
            o_ref[0, :, hp * LANES:(hp + 1) * LANES] = jnp.where(lo_half, o0, o1).astype(o_ref.dtype)


def _attention(q, k_ext, v_ext, bits, *, tq, tk):
    bsz, seq, d = q.shape
    nk = seq // tk
    n_sub = tk // LANES

    def last_kt(i):
        return ((i + 1) * tq - 1) // tk

    kv_map = lambda b, i, t: (b, jnp.minimum(t, last_kt(i)), 0)
    bits_map = lambda b, i, t: (b, i, (jnp.minimum(t, last_kt(i)) * n_sub) // MASK_BITS)
    return pl.pallas_call(
        functools.partial(_attn_kernel, tq=tq, tk=tk, nk=nk),
        grid=(bsz, seq // tq, nk),
        in_specs=[pl.BlockSpec((1, tq, d), lambda b, i, t: (b, i, 0)),
                  pl.BlockSpec((1, tk, 2 * d), kv_map),
                  pl.BlockSpec((1, tk, 2 * d), kv_map),
                  pl.BlockSpec((1, tq, LANES), bits_map)],
        out_specs=pl.BlockSpec((1, tq, d), lambda b, i, t: (b, i, 0)),
        out_shape=jax.ShapeDtypeStruct((bsz, seq, d), BF16),
        scratch_shapes=[pltpu.VMEM((N_HEADS // 2, 2 * tq, 2 * LANES), BF16),
                        pltpu.VMEM((N_HEADS, tq, LANES), F32),
                        pltpu.VMEM((N_HEADS, tq, LANES), F32),
                        pltpu.VMEM((tq, tk), F32),
                        pltpu.VMEM((tq, tk), F32)],
        compiler_params=_cparams(("arbitrary", "arbitrary", "arbitrary")),
        name="masked_attention",
    )(q, k_ext, v_ext, bits)


def _sigmoid(z):
    return 0.5 * jnp.tanh(0.5 * z) + 0.5


def _proj_residual_kernel(y_ref, w_ref, x_ref, gate_ref, g_ref, *rest, glu, final, d):
    r = jnp.dot(y_ref[0], w_ref[...], preferred_element_type=F32)
    if glu:
        r = r[:, :d] * _sigmoid(r[:, d:])
    x_new = x_ref[0] + gate_ref[0] * r
    if final:
        (o_ref,) = rest
        o_ref[0] = _rms(x_new, g_ref[...])
    else:
        sh_ref, sc_ref, o_ref, h_ref = rest
        o_ref[0] = x_new
        h_ref[0] = _norm_mod(x_new, g_ref[...], sh_ref[0], sc_ref[0]).astype(h_ref.dtype)


def _proj_residual(y, w, x, gate, g, sh=None, sc=None, *, tm, glu, name):
    bsz, seq, d = x.shape
    kdim, n = w.shape
    final = sh is None
    row = lambda b, i: (b, i, 0)
    per_b = lambda b, i: (b, 0, 0)
    const = lambda b, i: (0, 0)
    in_specs = [pl.BlockSpec((1, tm, kdim), row),
                pl.BlockSpec((kdim, n), const, pipeline_mode=pl.Buffered(1)),
                pl.BlockSpec((1, tm, d), row),
                pl.BlockSpec((1, 1, d), per_b),
                pl.BlockSpec((1, d), const)]
    args = [y, w.astype(BF16), x, gate, g]
    out_specs = [pl.BlockSpec((1, tm, d), row)]
    out_shape = [jax.ShapeDtypeStruct((bsz, seq, d), F32)]
    if not final:
        in_specs += [pl.BlockSpec((1, 1, d), per_b), pl.BlockSpec((1, 1, d), per_b)]
        args += [sh, sc]
        out_specs.append(pl.BlockSpec((1, tm, d), row))
        out_shape.append(jax.ShapeDtypeStruct((bsz, seq, d), BF16))
    out = pl.pallas_call(
        functools.partial(_proj_residual_kernel, glu=glu, final=final, d=d),
        grid=(bsz, seq // tm),
        in_specs=in_specs,
        out_specs=out_specs,
        out_shape=out_shape,
        compiler_params=_cparams(("arbitrary", "arbitrary")),
        name=name,
    )(*args)
    return out[0] if final else out


def _ffn_up_kernel(h_ref, wv_ref, wg_ref, cwv_ref, cwg_ref,
                   cbv_ref, cbg_ref, a_ref, zv_ref, zg_ref, *, tm):
    i = pl.program_id(2)
    ng = tm // SUBLANES
    tn = a_ref.shape[2]

    @pl.when(i == 0)
    def _seq_start():
        zv_ref[...] = jnp.zeros(zv_ref.shape, F32)
        zg_ref[...] = jnp.zeros(zg_ref.shape, F32)

    h = h_ref[0]
    sub = lax.broadcasted_iota(I32, (ng, SUBLANES, tn), 1)

    def conv(w_ref, carry_ref, cw_ref, cb_ref):
        z = jnp.dot(h, w_ref[...], preferred_element_type=F32).reshape(ng, SUBLANES, tn)
        z_all = jnp.concatenate([carry_ref[...][None], z], axis=0)
        carry_ref[...] = z[ng - 1]
        out = cb_ref[...] + cw_ref[CONV_WIDTH - 1:CONV_WIDTH, :] * z
        for back in range(1, CONV_WIDTH):
            rot = pltpu.roll(z_all, back, axis=1)
            shifted = jnp.where(sub < back, rot[:ng], rot[1:])
            out = out + cw_ref[CONV_WIDTH - 1 - back:CONV_WIDTH - back, :] * shifted
        return out

    val = conv(wv_ref, zv_ref, cwv_ref, cbv_ref)
    gt = conv(wg_ref, zg_ref, cwg_ref, cbg_ref)
    hg = 0.5 * gt
    hv = hg * val
    a_ref[0] = (hv + hv * jnp.tanh(hg)).reshape(tm, tn).astype(a_ref.dtype)


def _ffn_up(h, w_up, conv_w, conv_b, *, tm, tn):
    bsz, seq, d = h.shape
    dff = w_up.shape[1] // 2
    nj = dff // tn
    wv = w_up[:, :dff].astype(BF16)
    wg = w_up[:, dff:].astype(BF16)
    cwv = jnp.pad(conv_w[:, :dff], ((0, SUBLANES - CONV_WIDTH), (0, 0)))
    cwg = jnp.pad(conv_w[:, dff:], ((0, SUBLANES - CONV_WIDTH), (0, 0)))
    cbv = conv_b[:dff].reshape(1, dff)
    cbg = conv_b[dff:].reshape(1, dff)
    col = lambda j, b, i: (0, j)
    return pl.pallas_call(
        functools.partial(_ffn_up_kernel, tm=tm),
        grid=(nj, bsz, seq // tm),
        in_specs=[pl.BlockSpec((1, tm, d), lambda j, b, i: (b, i, 0)),
                  pl.BlockSpec((d, tn), col),
                  pl.BlockSpec((d, tn), col),
                  pl.BlockSpec((SUBLANES, tn), col),
                  pl.BlockSpec((SUBLANES, tn), col),
                  pl.BlockSpec((1, tn), col),
                  pl.BlockSpec((1, tn), col)],
        out_specs=pl.BlockSpec((1, tm, tn), lambda j, b, i: (b, i, j)),
        out_shape=jax.ShapeDtypeStruct((bsz, seq, dff), BF16),
        scratch_shapes=[pltpu.VMEM((SUBLANES, tn), F32),
                        pltpu.VMEM((SUBLANES, tn), F32)],
        compiler_params=_cparams(("arbitrary", "arbitrary", "arbitrary")),
        name="ffn_up",
    )(h, wv, wg, cwv, cwg, cbv, cbg)


def _ssm_in_kernel(h_ref, w_ref, u_ref):
    u_ref[0] = jnp.dot(h_ref[0], w_ref[...], preferred_element_type=F32)


def _ssm_in(h, w, *, tm):
    bsz, seq, d = h.shape
    n = w.shape[1]
    row = lambda b, i: (b, i, 0)
    return pl.pallas_call(
        _ssm_in_kernel,
        grid=(bsz, seq // tm),
        in_specs=[pl.BlockSpec((1, tm, d), row),
                  pl.BlockSpec((d, n), lambda b, i: (0, 0))],
        out_specs=pl.BlockSpec((1, tm, n), row),
        out_shape=jax.ShapeDtypeStruct((bsz, seq, n), F32),
        compiler_params=_cparams(("arbitrary", "arbitrary")),
        name="ssm_in",
    )(h, w.astype(BF16))


def _ssm_scan_kernel(u_ref, perm_ref, permt_ref, bbt_ref, tab_ref, pt_ref, cc_ref, dskip_ref,
                     y_ref, carry_ref, *, tt, ns):
    i = pl.program_id(1)
    nt = tt // SUBLANES

    @pl.when(i == 0)
    def _seq_start():
        carry_ref[...] = jnp.zeros(carry_ref.shape, F32)

    def cmul_add(x_re, x_im, a_re, a_im, y_re, y_im):
        return x_re + a_re * y_re - a_im * y_im, x_im + a_re * y_im + a_im * y_re

    nb = u_ref.shape[0]
    u = u_ref[...]
    lanes = lambda rows: jnp.concatenate([rows[b] for b in range(nb)], axis=1)
    unlanes = lambda wide: jnp.concatenate(
        [wide[:, b * LANES:(b + 1) * LANES] for b in range(nb)], axis=0)
    up = jnp.dot(perm_ref[...], lanes(u).astype(BF16), preferred_element_type=F32).astype(BF16)
    bu = jnp.dot(unlanes(up), bbt_ref[0], preferred_element_type=F32)
    bu_re = bu[:, :ns].reshape(nb, nt, SUBLANES, ns)
    bu_im = bu[:, ns:].reshape(nb, nt, SUBLANES, ns)
    a_re, a_im = tab_ref[0, 0, 0], tab_ref[0, 0, 1]
    h_re = jnp.zeros((nb, SUBLANES, ns), F32)
    h_im = jnp.zeros((nb, SUBLANES, ns), F32)
    loc_re, loc_im = [], []
    for g in range(nt):
        h_re, h_im = cmul_add(bu_re[:, g], bu_im[:, g], a_re, a_im, h_re, h_im)
        loc_re.append(h_re)
        loc_im.append(h_im)
    s_re, s_im = h_re, h_im
    for kk in range(3):
        s_re, s_im = cmul_add(s_re, s_im, tab_ref[0, 1 + kk, 0], tab_ref[0, 1 + kk, 1],
                              pltpu.roll(s_re, 1 << kk, axis=1),
                              pltpu.roll(s_im, 1 << kk, axis=1))
    c_re, c_im = carry_ref[:, 0], carry_ref[:, 1]
    first = lax.broadcasted_iota(I32, (nb, SUBLANES, ns), 1) == 0
    hin_re, hin_im = cmul_add(jnp.where(first, 0.0, pltpu.roll(s_re, 1, axis=1)),
                              jnp.where(first, 0.0, pltpu.roll(s_im, 1, axis=1)),
                              tab_ref[0, 4, 0], tab_ref[0, 4, 1], c_re, c_im)
    c_re, c_im = cmul_add(jnp.broadcast_to(s_re[:, SUBLANES - 1:], (nb, SUBLANES, ns)),
                          jnp.broadcast_to(s_im[:, SUBLANES - 1:], (nb, SUBLANES, ns)),
                          tab_ref[0, 5, 0], tab_ref[0, 5, 1], c_re, c_im)
    carry_ref[:, 0] = c_re
    carry_ref[:, 1] = c_im
    hg_re, hg_im = cmul_add(jnp.stack(loc_re, axis=1), jnp.stack(loc_im, axis=1),
                            pt_ref[0, 0], pt_ref[0, 1], hin_re[:, None], hin_im[:, None])
    hcat = jnp.concatenate([hg_re.reshape(nb * tt, ns), hg_im.reshape(nb * tt, ns)],
                           axis=1).astype(BF16)
    yp = jnp.dot(hcat, cc_ref[0], preferred_element_type=F32)
    yp = lanes(yp.reshape(nb, tt, LANES))
    yp_hi = yp.astype(BF16)
    yp_lo = (yp - yp_hi.astype(F32)).astype(BF16)
    y = (jnp.dot(permt_ref[...], yp_hi, preferred_element_type=F32)
         + jnp.dot(permt_ref[...], yp_lo, preferred_element_type=F32))
    y = unlanes(y).reshape(nb, tt, LANES) + dskip_ref[0] * u
    y_ref[...] = jax.nn.gelu(y).astype(y_ref.dtype)


def _ssm_scan(u, lam_re, lam_im, log_dt, b_re, b_im, c_re, c_im, d_skip, *, tt):
    bsz, seq, width = u.shape
    n_groups, n_state = lam_re.shape
    gs = SSM_GROUP
    gpb = LANES // gs
    n_gb = n_groups // gpb
    ns = gpb * n_state
    dt = jnp.exp(log_dt)[:, None]
    mag = jnp.exp(lam_re * dt)
    a_re = mag * jnp.cos(lam_im * dt)
    a_im = mag * jnp.sin(lam_im * dt)
    den = lam_re * lam_re + lam_im * lam_im
    coef_re = ((a_re - 1.0) * lam_re + a_im * lam_im) / den
    coef_im = (a_im * lam_re - (a_re - 1.0) * lam_im) / den
    bb_re = coef_re[..., None] * b_re - coef_im[..., None] * b_im
    bb_im = coef_re[..., None] * b_im + coef_im[..., None] * b_re
    eye = jnp.eye(gpb, dtype=F32)

    def block_diag_in(bb):
        t = bb.reshape(n_gb, gpb, n_state, gs)
        return jnp.einsum('ngpc,gh->ngchp', t, eye).reshape(n_gb, gpb * gs, gpb * n_state)

    def block_diag_out(cm):
        t = cm.reshape(n_gb, gpb, gs, n_state)
        return jnp.einsum('ngcp,gh->ngphc', t, eye).reshape(n_gb, gpb * n_state, gpb * gs)

    bbt = jnp.concatenate([block_diag_in(bb_re), block_diag_in(bb_im)], axis=2).astype(BF16)
    cc = jnp.concatenate([block_diag_out(c_re), -block_diag_out(c_im)], axis=1).astype(BF16)

    def powers(e):
        ang = lam_im[None] * dt[None] * e[:, None, None]
        mg = jnp.exp(lam_re[None] * dt[None] * e[:, None, None])
        pw = jnp.stack([mg * jnp.cos(ang), mg * jnp.sin(ang)], axis=1)
        pw = pw.reshape(e.shape[0], 2, n_gb, ns)
        return jnp.transpose(pw, (2, 0, 1, 3))

    nt = tt // SUBLANES
    rows = jnp.arange(SUBLANES, dtype=F32)
    rep = lambda p: jnp.broadcast_to(p[:, :, :, None, :], p.shape[:3] + (SUBLANES, ns))
    seg_steps = rep(powers(nt * 2.0 ** jnp.arange(3, dtype=F32)))
    live = rows[None, :] >= (2.0 ** jnp.arange(3, dtype=F32))[:, None]
    seg_steps = seg_steps * live[None, :, None, :, None].astype(F32)
    seg_rows = jnp.transpose(powers(nt * rows), (0, 2, 1, 3))[:, None]
    tab = jnp.concatenate([rep(powers(jnp.ones((1,), F32))), seg_steps, seg_rows,
                           rep(powers(jnp.full((1,), nt * SUBLANES, F32)))], axis=1)
    pt = jnp.transpose(rep(powers(jnp.arange(1, nt + 1, dtype=F32))), (0, 2, 1, 3, 4))
    new_row = jnp.arange(tt)
    old_row = nt * (new_row % SUBLANES) + new_row // SUBLANES
    perm = (old_row[:, None] == jnp.arange(tt)[None, :]).astype(BF16)
    dsk = d_skip.reshape(n_gb, 1, gpb * gs)
    const2 = lambda g, i: (0, 0)
    return pl.pallas_call(
        functools.partial(_ssm_scan_kernel, tt=tt, ns=ns),
        grid=(n_gb, seq // tt),
        in_specs=[pl.BlockSpec((bsz, tt, LANES), lambda g, i: (0, i, g)),
                  pl.BlockSpec((tt, tt), const2),
                  pl.BlockSpec((tt, tt), const2),
                  pl.BlockSpec((1, LANES, 2 * ns), lambda g, i: (g, 0, 0)),
                  pl.BlockSpec((1, 6, 2, SUBLANES, ns), lambda g, i: (g, 0, 0, 0, 0)),
                  pl.BlockSpec((1, 2, nt, SUBLANES, ns), lambda g, i: (g, 0, 0, 0, 0)),
                  pl.BlockSpec((1, 2 * ns, LANES), lambda g, i: (g, 0, 0)),
                  pl.BlockSpec((1, 1, LANES), lambda g, i: (g, 0, 0))],
        out_specs=pl.BlockSpec((bsz, tt, LANES), lambda g, i: (0, i, g)),
        out_shape=jax.ShapeDtypeStruct((bsz, seq, width), BF16),
        scratch_shapes=[pltpu.VMEM((bsz, 2, SUBLANES, ns), F32)],
        compiler_params=_cparams(("arbitrary", "arbitrary")),
        name="ssm_scan",
    )(u, perm, perm.T, bbt, tab, pt, cc, dsk)


class _Tiles(NamedTuple):
    proj_rows: int
    topk_q: int
    attn_q: int
    attn_k: int
    rows: int
    ffn_cols: int
    scan_rows: int


def _tiles(seq, dff):
    return _Tiles(proj_rows=min(512, seq), topk_q=min(4 * LANES, seq), attn_q=min(512, seq),
                  attn_k=min(512, seq), rows=min(1024, seq), ffn_cols=dff // 2,
                  scan_rows=min(512, seq))


def kernel(x, c, mod_w, mod_b, norm_mix_g, norm_ffn_g, attn_w_in, attn_w_out, ssm_w_in,
           ssm_lam_re, ssm_lam_im, ssm_log_dt, ssm_b_re, ssm_b_im, ssm_c_re, ssm_c_im,
           ssm_d, ssm_w_glu, ffn_w_up, ffn_conv_w, ffn_conv_b, ffn_w_down, final_g):
    bsz, seq, d = x.shape
    depth = mod_w.shape[0]
    n_keep = min(TOPK_MAX, seq // 4)
    t = _tiles(seq, ffn_w_down.shape[1])

    mod = _mod_vectors(c, mod_w, mod_b)
    mods = [[mod[i, :, None, k * d:(k + 1) * d] for k in range(6)] for i in range(depth)]
    h = None
    for i in range(depth):
        sh1, sc1, g1, sh2, sc2, g2 = mods[i]
        gm = norm_mix_g[i].reshape(1, d)
        gf = norm_ffn_g[i].reshape(1, d)
        j = i // 2
        if i % 2 == 0:
            q, k, v, qcat, kcat, wi = _attn_proj(x, gm, sh1, sc1, attn_w_in[j], tm=t.proj_rows)
            bits = _topk_mask(qcat, kcat, wi, tq=t.topk_q, n_keep=n_keep)
            att = _attention(q, k, v, bits, tq=t.attn_q, tk=t.attn_k)
            x, h = _proj_residual(att, attn_w_out[j], x, g1, gf, sh2, sc2, tm=t.rows, glu=False,
                                  name="attn_out")
        else:
            u = _ssm_in(h, ssm_w_in[j], tm=t.rows)
            y = _ssm_scan(u, ssm_lam_re[j], ssm_lam_im[j], ssm_log_dt[j], ssm_b_re[j],
                          ssm_b_im[j], ssm_c_re[j], ssm_c_im[j], ssm_d[j], tt=t.scan_rows)
            x, h = _proj_residual(y, ssm_w_glu[j], x, g1, gf, sh2, sc2, tm=t.rows, glu=True,
                                  name="ssm_glu")
        a = _ffn_up(h, ffn_w_up[i], ffn_conv_w[i], ffn_conv_b[i], tm=t.rows, tn=t.ffn_cols)
        if i == depth - 1:
            x = _proj_residual(a, ffn_w_down[i], x, g2, final_g.reshape(1, d), tm=t.rows,
                               glu=False, name="ffn_down_final")
        else:
            sh_n, sc_n = mods[i + 1][0], mods[i + 1][1]
            x, h = _proj_residual(a, ffn_w_down[i], x, g2, norm_mix_g[i + 1].reshape(1, d),
                                  sh_n, sc_n, tm=t.rows, glu=False, name="ffn_down")
    return x
```

```python
import functools
import math
from typing import NamedTuple

import jax
import jax.numpy as jnp
import numpy as np
from jax import lax
from jax.experimental import pallas as pl
from jax.experimental.pallas import tpu as pltpu

F32 = jnp.float32
BF16 = jnp.bfloat16
I32 = jnp.int32
I16 = jnp.int16

EPS = 1e-6
CHUNK = 64
N_HEADS = 16
HEAD_DIM = 64
IDX_HEADS = 8
IDX_DIM = 64
TOPK_MAX = 256
SSM_GROUP = 16
SSM_STATE = 64
CONV_WIDTH = 3

LANES = 128
SUBLANES = 8
MASK_BITS = 32
VMEM_LIMIT = 56 * 1024 * 1024

INT_MIN = -2 ** 31
KEY_NEG_INF = INT_MIN + 0x7FFFFF
MASK_BIAS = -1e30
M_INIT = -3e38
LOG2E = math.log2(math.e)


def _bf16_pieces(x, n=3):
    out, rest = [], np.float32(x)
    for _ in range(n):
        piece = np.float32(np.asarray(rest, np.float32).astype(BF16).astype(np.float32))
        out.append(float(piece))
        rest = np.float32(rest - piece)
    return out


def _alibi_slope2(h):
    return 2.0 ** (-8.0 * (h + 1) / N_HEADS) * LOG2E

_NT = (((1,), (1,)), ((), ()))


def _cparams(sem):
    return pltpu.CompilerParams(dimension_semantics=sem, vmem_limit_bytes=VMEM_LIMIT)


def _norm_mod(x, g, sh, sc):
    ms = jnp.mean(x * x, axis=-1, keepdims=True)
    y = x * lax.rsqrt(ms + EPS)
    return (y * g) * (1.0 + sc) + sh


def _rms(x, g):
    ms = jnp.mean(x * x, axis=-1, keepdims=True)
    return (x * lax.rsqrt(ms + EPS)) * g


def _mod_kernel(c_ref, w_ref, b_ref, o_ref):
    c = c_ref[...]
    cond = c * (1.0 / (1.0 + jnp.exp(-c)))
    o_ref[...] = jnp.dot(cond, w_ref[...], preferred_element_type=F32,
                         precision=lax.Precision.HIGHEST) + b_ref[...]


def _mod_vectors(c, mod_w, mod_b):
    depth, d, n6 = mod_w.shape
    bsz = c.shape[0]
    rows = SUBLANES
    c_pad = jnp.zeros((rows, d), F32).at[:bsz].set(c)
    out = pl.pallas_call(
        _mod_kernel,
        grid=(depth, n6 // d),
        in_specs=[pl.BlockSpec((rows, d), lambda i, j: (0, 0)),
                  pl.BlockSpec((None, d, d), lambda i, j: (i, 0, j)),
                  pl.BlockSpec((None, 1, d), lambda i, j: (i, 0, j))],
        out_specs=pl.BlockSpec((None, rows, d), lambda i, j: (i, 0, j)),
        out_shape=jax.ShapeDtypeStruct((depth, rows, n6), F32),
        compiler_params=_cparams(("arbitrary", "arbitrary")),
        name="mod_vectors",
    )(c_pad, mod_w, mod_b.reshape(depth, 1, n6))
    return out[:, :bsz, :]


def _attn_proj_kernel(x_ref, g_ref, sh_ref, sc_ref, wqkv_ref, wih_ref, wil_ref,
                      q_ref, k_ref, v_ref, qcat_ref, kcat_ref, wi_ref, *, d):
    h = _norm_mod(x_ref[0], g_ref[...], sh_ref[0], sc_ref[0])
    hb = h.astype(BF16)
    hl = (h - hb.astype(F32)).astype(BF16)
    qkv = jnp.dot(hb, wqkv_ref[...], preferred_element_type=F32)
    q_ref[0] = (qkv[:, :d] * (HEAD_DIM ** -0.5 * LOG2E)).astype(BF16)
    kb = qkv[:, d:2 * d].astype(BF16)
    vb = qkv[:, 2 * d:3 * d].astype(BF16)
    tm = kb.shape[0]
    j = pl.program_id(1) * tm + lax.broadcasted_iota(I32, (tm, LANES), 0)
    ln = lax.broadcasted_iota(I32, (tm, LANES), 1)
    j_hi = ((j // CHUNK) * CHUNK).astype(F32)
    j_lo = (j % CHUNK).astype(F32)
    kx = jnp.where(ln < 2, 1.0, jnp.where(ln < 5, j_hi, jnp.where(ln < 8, j_lo, 0.0))).astype(BF16)
    lo_half = ln < HEAD_DIM
    ones = jnp.ones((tm, LANES), BF16)
    for hp in range(N_HEADS // 2):
        k_ref[0, :, 2 * hp * LANES:(2 * hp + 1) * LANES] = kb[:, hp * LANES:(hp + 1) * LANES]
        k_ref[0, :, (2 * hp + 1) * LANES:(2 * hp + 2) * LANES] = kx
        v2 = vb[:, hp * LANES:(hp + 1) * LANES]
        v_ref[0, :, 2 * hp * LANES:(2 * hp + 1) * LANES] = jnp.where(lo_half, v2, ones)
        v_ref[0, :, (2 * hp + 1) * LANES:(2 * hp + 2) * LANES] = jnp.where(lo_half, ones, v2)
    wih = wih_ref[...]
    idx = (jnp.dot(hb, wih, preferred_element_type=F32)
           + jnp.dot(hl, wih, preferred_element_type=F32)
           + jnp.dot(hb, wil_ref[...], preferred_element_type=F32))
    nq = IDX_HEADS * IDX_DIM
    def split(tile):
        hi = tile.astype(BF16).astype(F32)
        return hi, tile - hi

    for pp in range(IDX_HEADS // 2):
        hi, lo = split(idx[:, pp * LANES:(pp + 1) * LANES] * (IDX_DIM ** -0.5))
        first = jnp.where(lo_half, hi, pltpu.roll(lo, IDX_DIM, axis=1)).astype(BF16)
        second = jnp.where(lo_half, pltpu.roll(hi, IDX_DIM, axis=1), lo).astype(BF16)
        for e, piece in enumerate((first, second)):
            hh = 2 * pp + e
            qcat_ref[0, :, 2 * hh * LANES:(2 * hh + 1) * LANES] = piece
            qcat_ref[0, :, (2 * hh + 1) * LANES:(2 * hh + 2) * LANES] = piece
    tail = idx[:, nq:nq + LANES]
    khi, klo = split(tail)
    kcat_ref[0, :, :LANES] = jnp.where(lo_half, khi, pltpu.roll(khi, IDX_DIM, axis=1)).astype(BF16)
    kcat_ref[0, :, LANES:] = jnp.where(lo_half, klo, pltpu.roll(klo, IDX_DIM, axis=1)).astype(BF16)
    wi_ref[0] = jnp.where(ln < IDX_HEADS, pltpu.roll(tail, IDX_DIM, axis=1) * (IDX_HEADS ** -0.5), 0.0)


def _attn_proj(x, g, sh, sc, w_in, *, tm):
    bsz, seq, d = x.shape
    nq = IDX_HEADS * IDX_DIM
    wqkv = w_in[:, :3 * d].astype(BF16)
    w_idx = w_in[:, 3 * d:]
    w_idx = jnp.pad(w_idx, ((0, 0), (0, nq + LANES - w_idx.shape[1])))
    w_hi = w_idx.astype(BF16)
    w_lo = (w_idx - w_hi.astype(F32)).astype(BF16)
    n_idx = w_idx.shape[1]
    row = lambda b, i: (b, i, 0)
    const = lambda b, i: (0, 0)
    per_b = lambda b, i: (b, 0, 0)
    return pl.pallas_call(
        functools.partial(_attn_proj_kernel, d=d),
        grid=(bsz, seq // tm),
        in_specs=[pl.BlockSpec((1, tm, d), row),
                  pl.BlockSpec((1, d), const),
                  pl.BlockSpec((1, 1, d), per_b),
                  pl.BlockSpec((1, 1, d), per_b),
                  pl.BlockSpec((d, 3 * d), const, pipeline_mode=pl.Buffered(1)),
                  pl.BlockSpec((d, n_idx), const, pipeline_mode=pl.Buffered(1)),
                  pl.BlockSpec((d, n_idx), const, pipeline_mode=pl.Buffered(1))],
        out_specs=[pl.BlockSpec((1, tm, d), row),
                   pl.BlockSpec((1, tm, 2 * d), row),
                   pl.BlockSpec((1, tm, 2 * d), row),
                   pl.BlockSpec((1, tm, 2 * IDX_HEADS * LANES), row),
                   pl.BlockSpec((1, tm, 2 * LANES), row),
                   pl.BlockSpec((1, tm, LANES), row)],
        out_shape=[jax.ShapeDtypeStruct((bsz, seq, d), BF16),
                   jax.ShapeDtypeStruct((bsz, seq, 2 * d), BF16),
                   jax.ShapeDtypeStruct((bsz, seq, 2 * d), BF16),
                   jax.ShapeDtypeStruct((bsz, seq, 2 * IDX_HEADS * LANES), BF16),
                   jax.ShapeDtypeStruct((bsz, seq, 2 * LANES), BF16),
                   jax.ShapeDtypeStruct((bsz, seq, LANES), F32)],
        compiler_params=_cparams(("arbitrary", "arbitrary")),
        name="attn_proj",
    )(x, g, sh, sc, wqkv, w_hi, w_lo)


def _topk_mask_kernel(qcat_ref, kcat_ref, wi_ref, bits_ref, hi_ref, lo_ref, pk_ref, qt_ref,
                      *, tq, tkr, n_keep, cr, unroll, tie_bits):
    qi = pl.program_id(1)
    n_adm = (qi + 1) * tq
    t_chunk = (qi * tq + lax.broadcasted_iota(I32, (1, tq), 1)) // CHUNK
    w_t = wi_ref[0].T

    for hh in range(IDX_HEADS):
        qt_ref[hh] = qcat_ref[0, :, 2 * hh * LANES:(2 * hh + 2) * LANES].astype(F32).T.astype(BF16)

    def score_body(kt, carry):
        r0 = pl.multiple_of(kt * tkr, tkr)
        kc = kcat_ref[0, pl.ds(r0, tkr), :]
        sc = jnp.zeros((tkr, tq), F32)
        for hh in range(IDX_HEADS):
            lg = jnp.dot(kc, qt_ref[hh], preferred_element_type=F32)
            sc = sc + w_t[hh:hh + 1, :] * jnp.maximum(lg, 0.0)
        sc = sc + 0.0
        raw = lax.bitcast_convert_type(sc, I32)
        key = raw ^ ((raw >> 31) & 0x7FFFFFFF)
        j_chunk = (r0 + lax.broadcasted_iota(I32, (tkr, 1), 0)) // CHUNK
        key = jnp.where(j_chunk <= t_chunk, key, KEY_NEG_INF)
        hi_ref[pl.ds(r0, tkr), :] = (key >> 16).astype(I16)
        lo_ref[pl.ds(r0, tkr), :] = ((key & 0xFFFF) - 2 ** 15).astype(I16)
        return carry

    lax.fori_loop(0, (n_adm + tkr - 1) // tkr, score_body, 0)

    n_ch = n_adm // (cr * unroll)
    one, zero = jnp.int16(1), jnp.int16(0)

    def count(pred):
        def body(c, acc):
            for uu in range(unroll):
                r0 = pl.multiple_of((c * unroll + uu) * cr, cr)
                acc = acc + jnp.where(pred(r0), one, zero)
            return acc
        acc = lax.fori_loop(0, n_ch, body, jnp.zeros((cr, tq), I16))
        return jnp.sum(acc.astype(I32), axis=0, keepdims=True)

    def hi_at(r0):
        return hi_ref[pl.ds(r0, cr), :]

    def lo_at(r0):
        return lo_ref[pl.ds(r0, cr), :]

    def bcast16(v):
        return jnp.broadcast_to(v, (cr, tq)).astype(I16)

    def hi_round(i, carry):
        thr, c_thr, c_gt = carry
        cand = thr + jnp.left_shift(jnp.int32(1), 15 - i)
        cand16 = bcast16(cand)
        cnt = count(lambda r0: hi_at(r0) >= cand16)
        ok = cnt >= n_keep
        return jnp.where(ok, cand, thr), jnp.where(ok, cnt, c_thr), jnp.where(ok, c_gt, cnt)

    thr_hi, c_ge_hi, c_above = lax.fori_loop(
        0, 16, hi_round,
        (jnp.full((1, tq), -2 ** 15, I32), jnp.full((1, tq), n_adm, I32),
         jnp.zeros((1, tq), I32)))
    thr_hi16 = bcast16(thr_hi)

    def bucket_body(c, carry):
        r0 = pl.multiple_of(c * cr, cr)
        lo_ref[pl.ds(r0, cr), :] = jnp.where(hi_at(r0) == thr_hi16, lo_at(r0),
                                             jnp.int16(-2 ** 15))
        return carry

    lax.fori_loop(0, n_adm // cr, bucket_body, 0)

    def lo_round(i, carry):
        thr, c_thr, c_gt = carry
        cand = thr + jnp.left_shift(jnp.int32(1), 15 - i)
        cand16 = bcast16(cand)
        cnt = c_above + count(lambda r0: lo_at(r0) >= cand16)
        ok = cnt >= n_keep
        return jnp.where(ok, cand, thr), jnp.where(ok, cnt, c_thr), jnp.where(ok, c_gt, cnt)

    thr_lo, c_thr, c_gt = lax.fori_loop(
        0, 16, lo_round, (jnp.full((1, tq), -2 ** 15, I32), c_ge_hi, c_above))
    thr_lo16 = bcast16(thr_lo)

    need_ties = jnp.max(c_thr) > n_keep
    n_tied_ok = n_keep - c_gt

    def tied(r0):
        return (lo_at(r0) == thr_lo16) & (hi_at(r0) == thr_hi16)

    def tie_round(i, cut):
        cand = cut + jnp.left_shift(jnp.int32(1), tie_bits - 1 - i)
        cand16 = bcast16(cand)
        cnt = count(lambda r0: tied(r0)
                    & ((r0 + lax.broadcasted_iota(I32, (cr, tq), 0)).astype(I16) < cand16))
        return jnp.where(cnt <= n_tied_ok, cand, cut)

    cut = lax.fori_loop(0, jnp.where(need_ties, tie_bits, 0), tie_round,
                        jnp.zeros((1, tq), I32))
    cut = jnp.where(need_ties, cut, jnp.int32(2 ** 15 - 1))
    cut16 = bcast16(cut)

    pk_ref[...] = jnp.zeros(pk_ref.shape, I32)
    n_sub = LANES // cr

    def pack_body(c, carry):
        pieces = []
        for uu in range(n_sub):
            r0 = pl.multiple_of(c * LANES + uu * cr, cr)
            j16 = (r0 + lax.broadcasted_iota(I32, (cr, tq), 0)).astype(I16)
            sel = ((hi_at(r0) > thr_hi16) | (lo_at(r0) > thr_lo16)
                   | (tied(r0) & (j16 < cut16)))
            adm16 = bcast16(((r0 // CHUNK) <= t_chunk).astype(I32))
            pieces.append(jnp.where(sel, adm16, zero).astype(I32))
        sel32 = jnp.concatenate(pieces, axis=0)
        w0 = pl.multiple_of((c // MASK_BITS) * LANES, LANES)
        pk_ref[pl.ds(w0, LANES), :] = pk_ref[pl.ds(w0, LANES), :] | jnp.left_shift(
            sel32, c % MASK_BITS)
        return carry

    lax.fori_loop(0, n_adm // LANES, pack_body, 0)
    bits_ref[0] = pk_ref[...].T


def _topk_mask(qcat, kcat, wi, *, tq, n_keep):
    bsz, seq, _ = qcat.shape
    assert seq < 2 ** 15, "key indices are compared as int16"
    n_words = -(-seq // (MASK_BITS * LANES))
    tie_bits = max(1, int(seq).bit_length())
    return pl.pallas_call(
        functools.partial(_topk_mask_kernel, tq=tq, tkr=min(2 * tq, seq), n_keep=n_keep,
                          cr=CHUNK, unroll=tq // CHUNK, tie_bits=tie_bits),
        grid=(bsz, seq // tq),
        in_specs=[pl.BlockSpec((1, tq, qcat.shape[2]), lambda b, i: (b, i, 0)),
                  pl.BlockSpec((1, seq, kcat.shape[2]), lambda b, i: (b, 0, 0)),
                  pl.BlockSpec((1, tq, LANES), lambda b, i: (b, i, 0))],
        out_specs=pl.BlockSpec((1, tq, n_words * LANES), lambda b, i: (b, i, 0)),
        out_shape=jax.ShapeDtypeStruct((bsz, seq, n_words * LANES), I32),
        scratch_shapes=[pltpu.VMEM((seq, tq), I16),
                        pltpu.VMEM((seq, tq), I16),
                        pltpu.VMEM((n_words * LANES, tq), I32),
                        pltpu.VMEM((IDX_HEADS, 2 * LANES, tq), BF16)],
        compiler_params=_cparams(("arbitrary", "arbitrary")),
        name="topk_mask",
    )(qcat, kcat, wi)


def _alibi_q_lanes(t_f, is_lane0, is_lane1, slope2):
    a = -slope2 * t_f
    a_hi = a.astype(BF16).astype(F32)
    lane = lax.broadcasted_iota(I32, (1, LANES), 1)
    const = jnp.zeros((1, LANES), F32)
    for n, piece in enumerate(_bf16_pieces(slope2)):
        const = jnp.where((lane == 2 + n) | (lane == 5 + n), piece, const)
    return jnp.where(is_lane0, a_hi, jnp.where(is_lane1, a - a_hi, const)).astype(BF16)


def _attn_kernel(q_ref, k_ref, v_ref, bits_ref, o_ref, qx_ref, m_ref, acc_ref, mb_ref,
                 corr_ref, *, tq, tk, nk):
    qi = pl.program_id(1)
    kt = pl.program_id(2)
    last = ((qi + 1) * tq - 1) // tk
    n_sub = tk // LANES
    lane = lax.broadcasted_iota(I32, (tq, LANES), 1)
    lo_half = lane < HEAD_DIM

    @pl.when(kt == 0)
    def _init():
        m_ref[...] = jnp.full(m_ref.shape, M_INIT, F32)
        acc_ref[...] = jnp.zeros(acc_ref.shape, F32)
        t_f = (qi * tq + lax.broadcasted_iota(I32, (tq, LANES), 0)).astype(F32)
        for h in range(N_HEADS):
            q2 = q_ref[0, :, (h // 2) * LANES:(h // 2 + 1) * LANES]
            zero = jnp.zeros_like(q2)
            rows = slice((h % 2) * tq, (h % 2 + 1) * tq)
            qx_ref[h // 2, rows, :LANES] = (jnp.where(lo_half, q2, zero) if h % 2 == 0
                                            else jnp.where(lo_half, zero, q2))
            qx_ref[h // 2, rows, LANES:] = _alibi_q_lanes(t_f, lane == 0, lane == 1,
                                                          _alibi_slope2(h))

    def step(diag):
        bits = bits_ref[0]
        sel = jnp.concatenate(
            [(bits >> ((kt * n_sub + i) % MASK_BITS)) & 1 for i in range(n_sub)], axis=1)
        mb_ref[...] = jnp.where(sel == 1, 0.0, MASK_BIAS)
        if diag:
            t = qi * tq + lax.broadcasted_iota(I32, (tq, 1), 0)
            j = kt * tk + lax.broadcasted_iota(I32, (1, tk), 1)
            corr_ref[...] = (2 * jnp.maximum(j - t, 0)).astype(F32)
        for h in range(N_HEADS):
            hp = h // 2
            if h % 2 == 0:
                s_pair = lax.dot_general(qx_ref[hp],
                                         k_ref[0, :, 2 * hp * LANES:(2 * hp + 2) * LANES],
                                         _NT, preferred_element_type=F32)
            s = s_pair[(h % 2) * tq:(h % 2 + 1) * tq]
            if diag:
                s = s + (mb_ref[...] - _alibi_slope2(h) * corr_ref[...])
            else:
                s = s + mb_ref[...]
            m_old = m_ref[h]
            m_new = jnp.maximum(m_old, jnp.max(s, axis=1, keepdims=True))
            p = jnp.exp2(s - jnp.tile(m_new, (1, n_sub)))
            m_ref[h] = m_new
            acc_ref[h] = (jnp.exp2(m_old - m_new) * acc_ref[h]
                          + jnp.dot(p.astype(BF16), v_ref[0, :, h * LANES:(h + 1) * LANES],
                                    preferred_element_type=F32))

    pl.when(kt < last)(functools.partial(step, False))
    pl.when(kt == last)(functools.partial(step, True))

    @pl.when(kt == nk - 1)
    def _finish():
        for hp in range(N_HEADS // 2):
            a0 = acc_ref[2 * hp]
            a1 = acc_ref[2 * hp + 1]
            o0 = a0 / pltpu.roll(a0, HEAD_DIM, axis=1)
            o1 = a1 / pltpu.roll(a1, HEAD_DIM, axis=1)
            o_ref[0, :, hp * LANES:(hp + 1) * LANES] = jnp.where(lo_half, o0, o1).astype(o_ref.dtype)


def _attention(q, k_ext, v_ext, bits, *, tq, tk):
    bsz, seq, d = q.shape
    nk = seq // tk
    n_sub = tk // LANES

    def last_kt(i):
        return ((i + 1) * tq - 1) // tk

    kv_map = lambda b, i, t: (b, jnp.minimum(t, last_kt(i)), 0)
    bits_map = lambda b, i, t: (b, i, (jnp.minimum(t, last_kt(i)) * n_sub) // MASK_BITS)
    return pl.pallas_call(
        functools.partial(_attn_kernel, tq=tq, tk=tk, nk=nk),
        grid=(bsz, seq // tq, nk),
        in_specs=[pl.BlockSpec((1, tq, d), lambda b, i, t: (b, i, 0)),
                  pl.BlockSpec((1, tk, 2 * d), kv_map),
                  pl.BlockSpec((1, tk, 2 * d), kv_map),
                  pl.BlockSpec((1, tq, LANES), bits_map)],
        out_specs=pl.BlockSpec((1, tq, d), lambda b, i, t: (b, i, 0)),
        out_shape=jax.ShapeDtypeStruct((bsz, seq, d), BF16),
        scratch_shapes=[pltpu.VMEM((N_HEADS // 2, 2 * tq, 2 * LANES), BF16),
                        pltpu.VMEM((N_HEADS, tq, LANES), F32),
                        pltpu.VMEM((N_HEADS, tq, LANES), F32),
                        pltpu.VMEM((tq, tk), F32),
                        pltpu.VMEM((tq, tk), F32)],
        compiler_params=_cparams(("arbitrary", "arbitrary", "arbitrary")),
        name="masked_attention",
    )(q, k_ext, v_ext, bits)


def _sigmoid(z):
    return 0.5 * jnp.tanh(0.5 * z) + 0.5


def _proj_residual_kernel(y_ref, w_ref, x_ref, gate_ref, g_ref, *rest, glu, final, d):
    r = jnp.dot(y_ref[0], w_ref[...], preferred_element_type=F32)
    if glu:
        r = r[:, :d] * _sigmoid(r[:, d:])
    x_new = x_ref[0] + gate_ref[0] * r
    if final:
        (o_ref,) = rest
        o_ref[0] = _rms(x_new, g_ref[...])
    else:
        sh_ref, sc_ref, o_ref, h_ref = rest
        o_ref[0] = x_new
        h_ref[0] = _norm_mod(x_new, g_ref[...], sh_ref[0], sc_ref[0]).astype(h_ref.dtype)


def _proj_residual(y, w, x, gate, g, sh=None, sc=None, *, tm, glu, name):
    bsz, seq, d = x.shape
    kdim, n = w.shape
    final = sh is None
    row = lambda b, i: (b, i, 0)
    per_b = lambda b, i: (b, 0, 0)
    const = lambda b, i: (0, 0)
    in_specs = [pl.BlockSpec((1, tm, kdim), row),
                pl.BlockSpec((kdim, n), const, pipeline_mode=pl.Buffered(1)),
                pl.BlockSpec((1, tm, d), row),
                pl.BlockSpec((1, 1, d), per_b),
                pl.BlockSpec((1, d), const)]
    args = [y, w.astype(BF16), x, gate, g]
    out_specs = [pl.BlockSpec((1, tm, d), row)]
    out_shape = [jax.ShapeDtypeStruct((bsz, seq, d), F32)]
    if not final:
        in_specs += [pl.BlockSpec((1, 1, d), per_b), pl.BlockSpec((1, 1, d), per_b)]
        args += [sh, sc]
        out_specs.append(pl.BlockSpec((1, tm, d), row))
        out_shape.append(jax.ShapeDtypeStruct((bsz, seq, d), BF16))
    out = pl.pallas_call(
        functools.partial(_proj_residual_kernel, glu=glu, final=final, d=d),
        grid=(bsz, seq // tm),
        in_specs=in_specs,
        out_specs=out_specs,
        out_shape=out_shape,
        compiler_params=_cparams(("arbitrary", "arbitrary")),
        name=name,
    )(*args)
    return out[0] if final else out


def _ffn_up_kernel(h_ref, wv_ref, wg_ref, cwv_ref, cwg_ref,
                   cbv_ref, cbg_ref, a_ref, zv_ref, zg_ref, *, tm):
    i = pl.program_id(2)
    ng = tm // SUBLANES
    tn = a_ref.shape[2]

    @pl.when(i == 0)
    def _seq_start():
        zv_ref[...] = jnp.zeros(zv_ref.shape, F32)
        zg_ref[...] = jnp.zeros(zg_ref.shape, F32)

    h = h_ref[0]
    sub = lax.broadcasted_iota(I32, (ng, SUBLANES, tn), 1)

    def conv(w_ref, carry_ref, cw_ref, cb_ref):
        z = jnp.dot(h, w_ref[...], preferred_element_type=F32).reshape(ng, SUBLANES, tn)
        z_all = jnp.concatenate([carry_ref[...][None], z], axis=0)
        carry_ref[...] = z[ng - 1]
        out = cb_ref[...] + cw_ref[CONV_WIDTH - 1:CONV_WIDTH, :] * z
        for back in range(1, CONV_WIDTH):
            rot = pltpu.roll(z_all, back, axis=1)
            shifted = jnp.where(sub < back, rot[:ng], rot[1:])
            out = out + cw_ref[CONV_WIDTH - 1 - back:CONV_WIDTH - back, :] * shifted
        return out

    val = conv(wv_ref, zv_ref, cwv_ref, cbv_ref)
    gt = conv(wg_ref, zg_ref, cwg_ref, cbg_ref)
    hg = 0.5 * gt
    hv = hg * val
    a_ref[0] = (hv + hv * jnp.tanh(hg)).reshape(tm, tn).astype(a_ref.dtype)


def _ffn_up(h, w_up, conv_w, conv_b, *, tm, tn):
    bsz, seq, d = h.shape
    dff = w_up.shape[1] // 2
    nj = dff // tn
    wv = w_up[:, :dff].astype(BF16)
    wg = w_up[:, dff:].astype(BF16)
    cwv = jnp.pad(conv_w[:, :dff], ((0, SUBLANES - CONV_WIDTH), (0, 0)))
    cwg = jnp.pad(conv_w[:, dff:], ((0, SUBLANES - CONV_WIDTH), (0, 0)))
    cbv = conv_b[:dff].reshape(1, dff)
    cbg = conv_b[dff:].reshape(1, dff)
    col = lambda j, b, i: (0, j)
    return pl.pallas_call(
        functools.partial(_ffn_up_kernel, tm=tm),
        grid=(nj, bsz, seq // tm),
        in_specs=[pl.BlockSpec((1, tm, d), lambda j, b, i: (b, i, 0)),
                  pl.BlockSpec((d, tn), col),
                  pl.BlockSpec((d, tn), col),
                  pl.BlockSpec((SUBLANES, tn), col),
                  pl.BlockSpec((SUBLANES, tn), col),
                  pl.BlockSpec((1, tn), col),
                  pl.BlockSpec((1, tn), col)],
        out_specs=pl.BlockSpec((1, tm, tn), lambda j, b, i: (b, i, j)),
        out_shape=jax.ShapeDtypeStruct((bsz, seq, dff), BF16),
        scratch_shapes=[pltpu.VMEM((SUBLANES, tn), F32),
                        pltpu.VMEM((SUBLANES, tn), F32)],
        compiler_params=_cparams(("arbitrary", "arbitrary", "arbitrary")),
        name="ffn_up",
    )(h, wv, wg, cwv, cwg, cbv, cbg)


def _ssm_in_kernel(h_ref, w_ref, u_ref):
    u_ref[0] = jnp.dot(h_ref[0], w_ref[...], preferred_element_type=F32)


def _ssm_in(h, w, *, tm):
    bsz, seq, d = h.shape
    n = w.shape[1]
    row = lambda b, i: (b, i, 0)
    return pl.pallas_call(
        _ssm_in_kernel,
        grid=(bsz, seq // tm),
        in_specs=[pl.BlockSpec((1, tm, d), row),
                  pl.BlockSpec((d, n), lambda b, i: (0, 0))],
        out_specs=pl.BlockSpec((1, tm, n), row),
        out_shape=jax.ShapeDtypeStruct((bsz, seq, n), F32),
        compiler_params=_cparams(("arbitrary", "arbitrary")),
        name="ssm_in",
    )(h, w.astype(BF16))


def _ssm_scan_kernel(u_ref, perm_ref, permt_ref, bbt_ref, tab_ref, pt_ref, cc_ref, dskip_ref,
                     y_ref, carry_ref, *, tt, ns):
    i = pl.program_id(1)
    nt = tt // SUBLANES

    @pl.when(i == 0)
    def _seq_start():
        carry_ref[...] = jnp.zeros(carry_ref.shape, F32)

    def cmul_add(x_re, x_im, a_re, a_im, y_re, y_im):
        return x_re + a_re * y_re - a_im * y_im, x_im + a_re * y_im + a_im * y_re

    nb = u_ref.shape[0]
    u = u_ref[...]
    lanes = lambda rows: jnp.concatenate([rows[b] for b in range(nb)], axis=1)
    unlanes = lambda wide: jnp.concatenate(
        [wide[:, b * LANES:(b + 1) * LANES] for b in range(nb)], axis=0)
    up = jnp.dot(perm_ref[...], lanes(u).astype(BF16), preferred_element_type=F32).astype(BF16)
    bu = jnp.dot(unlanes(up), bbt_ref[0], preferred_element_type=F32)
    bu_re = bu[:, :ns].reshape(nb, nt, SUBLANES, ns)
    bu_im = bu[:, ns:].reshape(nb, nt, SUBLANES, ns)
    a_re, a_im = tab_ref[0, 0, 0], tab_ref[0, 0, 1]
    h_re = jnp.zeros((nb, SUBLANES, ns), F32)
    h_im = jnp.zeros((nb, SUBLANES, ns), F32)
    loc_re, loc_im = [], []
    for g in range(nt):
        h_re, h_im = cmul_add(bu_re[:, g], bu_im[:, g], a_re, a_im, h_re, h_im)
        loc_re.append(h_re)
        loc_im.append(h_im)
    s_re, s_im = h_re, h_im
    for kk in range(3):
        s_re, s_im = cmul_add(s_re, s_im, tab_ref[0, 1 + kk, 0], tab_ref[0, 1 + kk, 1],
                              pltpu.roll(s_re, 1 << kk, axis=1),
                              pltpu.roll(s_im, 1 << kk, axis=1))
    c_re, c_im = carry_ref[:, 0], carry_ref[:, 1]
    first = lax.broadcasted_iota(I32, (nb, SUBLANES, ns), 1) == 0
    hin_re, hin_im = cmul_add(jnp.where(first, 0.0, pltpu.roll(s_re, 1, axis=1)),
                              jnp.where(first, 0.0, pltpu.roll(s_im, 1, axis=1)),
                              tab_ref[0, 4, 0], tab_ref[0, 4, 1], c_re, c_im)
    c_re, c_im = cmul_add(jnp.broadcast_to(s_re[:, SUBLANES - 1:], (nb, SUBLANES, ns)),
                          jnp.broadcast_to(s_im[:, SUBLANES - 1:], (nb, SUBLANES, ns)),
                          tab_ref[0, 5, 0], tab_ref[0, 5, 1], c_re, c_im)
    carry_ref[:, 0] = c_re
    carry_ref[:, 1] = c_im
    hg_re, hg_im = cmul_add(jnp.stack(loc_re, axis=1), jnp.stack(loc_im, axis=1),
                            pt_ref[0, 0], pt_ref[0, 1], hin_re[:, None], hin_im[:, None])
    hcat = jnp.concatenate([hg_re.reshape(nb * tt, ns), hg_im.reshape(nb * tt, ns)],
                           axis=1).astype(BF16)
    yp = jnp.dot(hcat, cc_ref[0], preferred_element_type=F32)
    yp = lanes(yp.reshape(nb, tt, LANES))
    yp_hi = yp.astype(BF16)
    yp_lo = (yp - yp_hi.astype(F32)).astype(BF16)
    y = (jnp.dot(permt_ref[...], yp_hi, preferred_element_type=F32)
         + jnp.dot(permt_ref[...], yp_lo, preferred_element_type=F32))
    y = unlanes(y).reshape(nb, tt, LANES) + dskip_ref[0] * u
    y_ref[...] = jax.nn.gelu(y).astype(y_ref.dtype)


def _ssm_scan(u, lam_re, lam_im, log_dt, b_re, b_im, c_re, c_im, d_skip, *, tt):
    bsz, seq, width = u.shape
    n_groups, n_state = lam_re.shape
    gs = SSM_GROUP
    gpb = LANES // gs
    n_gb = n_groups // gpb
    ns = gpb * n_state
    dt = jnp.exp(log_dt)[:, None]
    mag = jnp.exp(lam_re * dt)
    a_re = mag * jnp.cos(lam_im * dt)
    a_im = mag * jnp.sin(lam_im * dt)
    den = lam_re * lam_re + lam_im * lam_im
    coef_re = ((a_re - 1.0) * lam_re + a_im * lam_im) / den
    coef_im = (a_im * lam_re - (a_re - 1.0) * lam_im) / den
    bb_re = coef_re[..., None] * b_re - coef_im[..., None] * b_im
    bb_im = coef_re[..., None] * b_im + coef_im[..., None] * b_re
    eye = jnp.eye(gpb, dtype=F32)

    def block_diag_in(bb):
        t = bb.reshape(n_gb, gpb, n_state, gs)
        return jnp.einsum('ngpc,gh->ngchp', t, eye).reshape(n_gb, gpb * gs, gpb * n_state)

    def block_diag_out(cm):
        t = cm.reshape(n_gb, gpb, gs, n_state)
        return jnp.einsum('ngcp,gh->ngphc', t, eye).reshape(n_gb, gpb * n_state, gpb * gs)

    bbt = jnp.concatenate([block_diag_in(bb_re), block_diag_in(bb_im)], axis=2).astype(BF16)
    cc = jnp.concatenate([block_diag_out(c_re), -block_diag_out(c_im)], axis=1).astype(BF16)

    def powers(e):
        ang = lam_im[None] * dt[None] * e[:, None, None]
        mg = jnp.exp(lam_re[None] * dt[None] * e[:, None, None])
        pw = jnp.stack([mg * jnp.cos(ang), mg * jnp.sin(ang)], axis=1)
        pw = pw.reshape(e.shape[0], 2, n_gb, ns)
        return jnp.transpose(pw, (2, 0, 1, 3))

    nt = tt // SUBLANES
    rows = jnp.arange(SUBLANES, dtype=F32)
    rep = lambda p: jnp.broadcast_to(p[:, :, :, None, :], p.shape[:3] + (SUBLANES, ns))
    seg_steps = rep(powers(nt * 2.0 ** jnp.arange(3, dtype=F32)))
    live = rows[None, :] >= (2.0 ** jnp.arange(3, dtype=F32))[:, None]
    seg_steps = seg_steps * live[None, :, None, :, None].astype(F32)
    seg_rows = jnp.transpose(powers(nt * rows), (0, 2, 1, 3))[:, None]
    tab = jnp.concatenate([rep(powers(jnp.ones((1,), F32))), seg_steps, seg_rows,
                           rep(powers(jnp.full((1,), nt * SUBLANES, F32)))], axis=1)
    pt = jnp.transpose(rep(powers(jnp.arange(1, nt + 1, dtype=F32))), (0, 2, 1, 3, 4))
    new_row = jnp.arange(tt)
    old_row = nt * (new_row % SUBLANES) + new_row // SUBLANES
    perm = (old_row[:, None] == jnp.arange(tt)[None, :]).astype(BF16)
    dsk = d_skip.reshape(n_gb, 1, gpb * gs)
    const2 = lambda g, i: (0, 0)
    return pl.pallas_call(
        functools.partial(_ssm_scan_kernel, tt=tt, ns=ns),
        grid=(n_gb, seq // tt),
        in_specs=[pl.BlockSpec((bsz, tt, LANES), lambda g, i: (0, i, g)),
                  pl.BlockSpec((tt, tt), const2),
                  pl.BlockSpec((tt, tt), const2),
                  pl.BlockSpec((1, LANES, 2 * ns), lambda g, i: (g, 0, 0)),
                  pl.BlockSpec((1, 6, 2, SUBLANES, ns), lambda g, i: (g, 0, 0, 0, 0)),
                  pl.BlockSpec((1, 2, nt, SUBLANES, ns), lambda g, i: (g, 0, 0, 0, 0)),
                  pl.BlockSpec((1, 2 * ns, LANES), lambda g, i: (g, 0, 0)),
                  pl.BlockSpec((1, 1, LANES), lambda g, i: (g, 0, 0))],
        out_specs=pl.BlockSpec((bsz, tt, LANES), lambda g, i: (0, i, g)),
        out_shape=jax.ShapeDtypeStruct((bsz, seq, width), BF16),
        scratch_shapes=[pltpu.VMEM((bsz, 2, SUBLANES, ns), F32)],
        compiler_params=_cparams(("arbitrary", "arbitrary")),
        name="ssm_scan",
    )(u, perm, perm.T, bbt, tab, pt, cc, dsk)


class _Tiles(NamedTuple):
    proj_rows: int
    topk_q: int
    attn_q: int
    attn_k: int
    rows: int
    ffn_cols: int
    scan_rows: int


def _tiles(seq, dff):
    return _Tiles(proj_rows=min(512, seq), topk_q=2 * LANES, attn_q=min(512, seq),
                  attn_k=min(512, seq), rows=min(1024, seq), ffn_cols=dff // 2,
                  scan_rows=min(512, seq))


def kernel(x, c, mod_w, mod_b, norm_mix_g, norm_ffn_g, attn_w_in, attn_w_out, ssm_w_in,
           ssm_lam_re, ssm_lam_im, ssm_log_dt, ssm_b_re, ssm_b_im, ssm_c_re, ssm_c_im,
           ssm_d, ssm_w_glu, ffn_w_up, ffn_conv_w, ffn_conv_b, ffn_w_down, final_g):
    bsz, seq, d = x.shape
    depth = mod_w.shape[0]
    n_keep = min(TOPK_MAX, seq // 4)
    t = _tiles(seq, ffn_w_down.shape[1])

    mod = _mod_vectors(c, mod_w, mod_b)
    mods = [[mod[i, :, None, k * d:(k + 1) * d] for k in range(6)] for i in range(depth)]
    h = None
    for i in range(depth):
        sh1, sc1, g1, sh2, sc2, g2 = mods[i]
        gm = norm_mix_g[i].reshape(1, d)
        gf = norm_ffn_g[i].reshape(1, d)
        j = i // 2
        if i % 2 == 0:
            q, k, v, qcat, kcat, wi = _attn_proj(x, gm, sh1, sc1, attn_w_in[j], tm=t.proj_rows)
            bits = _topk_mask(qcat, kcat, wi, tq=t.topk_q, n_keep=n_keep)
            att = _attention(q, k, v, bits, tq=t.attn_q, tk=t.attn_k)
            x, h = _proj_residual(att, attn_w_out[j], x, g1, gf, sh2, sc2, tm=t.rows, glu=False,
                                  name="attn_out")
        else:
            u = _ssm_in(h, ssm_w_in[j], tm=t.rows)
            y = _ssm_scan(u, ssm_lam_re[j], ssm_lam_im[j], ssm_log_dt[j], ssm_b_re[j],
                          ssm_b_im[j], ssm_c_re[j], ssm_c_im[j], ssm_d[j], tt=t.scan_rows)
            x, h = _proj_residual(y, ssm_w_glu[j], x, g1, gf, sh2, sc2, tm=t.rows, glu=True,
                                  name="ssm_glu")
        a = _ffn_up(h, ffn_w_up[i], ffn_conv_w[i], ffn_conv_b[i], tm=t.rows, tn=t.ffn_cols)
        if i == depth - 1:
            x = _proj_residual(a, ffn_w_down[i], x, g2, final_g.reshape(1, d), tm=t.rows,
                               glu=False, name="ffn_down_final")
        else:
            sh_n, sc_n = mods[i + 1][0], mods[i + 1][1]
            x, h = _proj_residual(a, ffn_w_down[i], x, g2, norm_mix_g[i + 1].reshape(1, d),
                                  sh_n, sc_n, tm=t.rows, glu=False, name="ffn_down")
    return x
```

```python
import functools
import math
from typing import NamedTuple

import jax
import jax.numpy as jnp
import numpy as np
from jax import lax
from jax.experimental import pallas as pl
from jax.experimental.pallas import tpu as pltpu

F32 = jnp.float32
BF16 = jnp.bfloat16
I32 = jnp.int32
I16 = jnp.int16

EPS = 1e-6
CHUNK = 64
N_HEADS = 16
HEAD_DIM = 64
IDX_HEADS = 8
IDX_DIM = 64
TOPK_MAX = 256
SSM_GROUP = 16
SSM_STATE = 64
CONV_WIDTH = 3

LANES = 128
SUBLANES = 8
MASK_BITS = 32
VMEM_LIMIT = 56 * 1024 * 1024

INT_MIN = -2 ** 31
KEY_NEG_INF = INT_MIN + 0x7FFFFF
MASK_BIAS = -1e30
M_INIT = -3e38
LOG2E = math.log2(math.e)


def _bf16_pieces(x, n=3):
    out, rest = [], np.float32(x)
    for _ in range(n):
        piece = np.float32(np.asarray(rest, np.float32).astype(BF16).astype(np.float32))
        out.append(float(piece))
        rest = np.float32(rest - piece)
    return out


def _alibi_slope2(h):
    return 2.0 ** (-8.0 * (h + 1) / N_HEADS) * LOG2E

_NT = (((1,), (1,)), ((), ()))


def _cparams(sem):
    return pltpu.CompilerParams(dimension_semantics=sem, vmem_limit_bytes=VMEM_LIMIT)


def _norm_mod(x, g, sh, sc):
    ms = jnp.mean(x * x, axis=-1, keepdims=True)
    y = x * lax.rsqrt(ms + EPS)
    return (y * g) * (1.0 + sc) + sh


def _rms(x, g):
    ms = jnp.mean(x * x, axis=-1, keepdims=True)
    return (x * lax.rsqrt(ms + EPS)) * g


def _mod_kernel(c_ref, w_ref, b_ref, o_ref):
    c = c_ref[...]
    cond = c * (1.0 / (1.0 + jnp.exp(-c)))
    o_ref[...] = jnp.dot(cond, w_ref[...], preferred_element_type=F32,
                         precision=lax.Precision.HIGHEST) + b_ref[...]


def _mod_vectors(c, mod_w, mod_b):
    depth, d, n6 = mod_w.shape
    bsz = c.shape[0]
    rows = SUBLANES
    c_pad = jnp.zeros((rows, d), F32).at[:bsz].set(c)
    out = pl.pallas_call(
        _mod_kernel,
        grid=(depth, n6 // d),
        in_specs=[pl.BlockSpec((rows, d), lambda i, j: (0, 0)),
                  pl.BlockSpec((None, d, d), lambda i, j: (i, 0, j)),
                  pl.BlockSpec((None, 1, d), lambda i, j: (i, 0, j))],
        out_specs=pl.BlockSpec((None, rows, d), lambda i, j: (i, 0, j)),
        out_shape=jax.ShapeDtypeStruct((depth, rows, n6), F32),
        compiler_params=_cparams(("arbitrary", "arbitrary")),
        name="mod_vectors",
    )(c_pad, mod_w, mod_b.reshape(depth, 1, n6))
    return out[:, :bsz, :]


def _attn_proj_kernel(x_ref, g_ref, sh_ref, sc_ref, wqkv_ref, wih_ref, wil_ref,
                      q_ref, k_ref, v_ref, qcat_ref, kcat_ref, wi_ref, *, d):
    h = _norm_mod(x_ref[0], g_ref[...], sh_ref[0], sc_ref[0])
    hb = h.astype(BF16)
    hl = (h - hb.astype(F32)).astype(BF16)
    qkv = jnp.dot(hb, wqkv_ref[...], preferred_element_type=F32)
    q_ref[0] = (qkv[:, :d] * (HEAD_DIM ** -0.5 * LOG2E)).astype(BF16)
    kb = qkv[:, d:2 * d].astype(BF16)
    vb = qkv[:, 2 * d:3 * d].astype(BF16)
    tm = kb.shape[0]
    j = pl.program_id(1) * tm + lax.broadcasted_iota(I32, (tm, LANES), 0)
    ln = lax.broadcasted_iota(I32, (tm, LANES), 1)
    j_hi = ((j // CHUNK) * CHUNK).astype(F32)
    j_lo = (j % CHUNK).astype(F32)
    kx = jnp.where(ln < 2, 1.0, jnp.where(ln < 5, j_hi, jnp.where(ln < 8, j_lo, 0.0))).astype(BF16)
    lo_half = ln < HEAD_DIM
    ones = jnp.ones((tm, LANES), BF16)
    for hp in range(N_HEADS // 2):
        k_ref[0, :, 2 * hp * LANES:(2 * hp + 1) * LANES] = kb[:, hp * LANES:(hp + 1) * LANES]
        k_ref[0, :, (2 * hp + 1) * LANES:(2 * hp + 2) * LANES] = kx
        v2 = vb[:, hp * LANES:(hp + 1) * LANES]
        v_ref[0, :, 2 * hp * LANES:(2 * hp + 1) * LANES] = jnp.where(lo_half, v2, ones)
        v_ref[0, :, (2 * hp + 1) * LANES:(2 * hp + 2) * LANES] = jnp.where(lo_half, ones, v2)
    wih = wih_ref[...]
    idx = (jnp.dot(hb, wih, preferred_element_type=F32)
           + jnp.dot(hl, wih, preferred_element_type=F32)
           + jnp.dot(hb, wil_ref[...], preferred_element_type=F32))
    nq = IDX_HEADS * IDX_DIM
    def split(tile):
        hi = tile.astype(BF16).astype(F32)
        return hi, tile - hi

    for pp in range(IDX_HEADS // 2):
        hi, lo = split(idx[:, pp * LANES:(pp + 1) * LANES] * (IDX_DIM ** -0.5))
        first = jnp.where(lo_half, hi, pltpu.roll(lo, IDX_DIM, axis=1)).astype(BF16)
        second = jnp.where(lo_half, pltpu.roll(hi, IDX_DIM, axis=1), lo).astype(BF16)
        for e, piece in enumerate((first, second)):
            hh = 2 * pp + e
            qcat_ref[0, :, 2 * hh * LANES:(2 * hh + 1) * LANES] = piece
            qcat_ref[0, :, (2 * hh + 1) * LANES:(2 * hh + 2) * LANES] = piece
    tail = idx[:, nq:nq + LANES]
    khi, klo = split(tail)
    kcat_ref[0, :, :LANES] = jnp.where(lo_half, khi, pltpu.roll(khi, IDX_DIM, axis=1)).astype(BF16)
    kcat_ref[0, :, LANES:] = jnp.where(lo_half, klo, pltpu.roll(klo, IDX_DIM, axis=1)).astype(BF16)
    wi_ref[0] = jnp.where(ln < IDX_HEADS, pltpu.roll(tail, IDX_DIM, axis=1) * (IDX_HEADS ** -0.5), 0.0)


def _attn_proj(x, g, sh, sc, w_in, *, tm):
    bsz, seq, d = x.shape
    nq = IDX_HEADS * IDX_DIM
    wqkv = w_in[:, :3 * d].astype(BF16)
    w_idx = w_in[:, 3 * d:]
    w_idx = jnp.pad(w_idx, ((0, 0), (0, nq + LANES - w_idx.shape[1])))
    w_hi = w_idx.astype(BF16)
    w_lo = (w_idx - w_hi.astype(F32)).astype(BF16)
    n_idx = w_idx.shape[1]
    row = lambda b, i: (b, i, 0)
    const = lambda b, i: (0, 0)
    per_b = lambda b, i: (b, 0, 0)
    return pl.pallas_call(
        functools.partial(_attn_proj_kernel, d=d),
        grid=(bsz, seq // tm),
        in_specs=[pl.BlockSpec((1, tm, d), row),
                  pl.BlockSpec((1, d), const),
                  pl.BlockSpec((1, 1, d), per_b),
                  pl.BlockSpec((1, 1, d), per_b),
                  pl.BlockSpec((d, 3 * d), const, pipeline_mode=pl.Buffered(1)),
                  pl.BlockSpec((d, n_idx), const, pipeline_mode=pl.Buffered(1)),
                  pl.BlockSpec((d, n_idx), const, pipeline_mode=pl.Buffered(1))],
        out_specs=[pl.BlockSpec((1, tm, d), row),
                   pl.BlockSpec((1, tm, 2 * d), row),
                   pl.BlockSpec((1, tm, 2 * d), row),
                   pl.BlockSpec((1, tm, 2 * IDX_HEADS * LANES), row),
                   pl.BlockSpec((1, tm, 2 * LANES), row),
                   pl.BlockSpec((1, tm, LANES), row)],
        out_shape=[jax.ShapeDtypeStruct((bsz, seq, d), BF16),
                   jax.ShapeDtypeStruct((bsz, seq, 2 * d), BF16),
                   jax.ShapeDtypeStruct((bsz, seq, 2 * d), BF16),
                   jax.ShapeDtypeStruct((bsz, seq, 2 * IDX_HEADS * LANES), BF16),
                   jax.ShapeDtypeStruct((bsz, seq, 2 * LANES), BF16),
                   jax.ShapeDtypeStruct((bsz, seq, LANES), F32)],
        compiler_params=_cparams(("arbitrary", "arbitrary")),
        name="attn_proj",
    )(x, g, sh, sc, wqkv, w_hi, w_lo)


def _topk_mask_kernel(qcat_ref, kcat_ref, wi_ref, bits_ref, hi_ref, lo_ref, pk_ref,
                      *, tq, tkr, n_keep, cr, unroll, tie_bits):
    qi = pl.program_id(1)
    n_adm = (qi + 1) * tq
    t_chunk = (qi * tq + lax.broadcasted_iota(I32, (1, tq), 1)) // CHUNK
    w_t = wi_ref[0].T

    def score_body(kt, carry):
        r0 = pl.multiple_of(kt * tkr, tkr)
        kc = kcat_ref[0, pl.ds(r0, tkr), :]
        sc = jnp.zeros((tkr, tq), F32)
        for hh in range(IDX_HEADS):
            lg = lax.dot_general(kc, qcat_ref[0, :, 2 * hh * LANES:(2 * hh + 2) * LANES],
                                 _NT, preferred_element_type=F32)
            sc = sc + w_t[hh:hh + 1, :] * jnp.maximum(lg, 0.0)
        sc = sc + 0.0
        raw = lax.bitcast_convert_type(sc, I32)
        key = raw ^ ((raw >> 31) & 0x7FFFFFFF)
        j_chunk = (r0 + lax.broadcasted_iota(I32, (tkr, 1), 0)) // CHUNK
        key = jnp.where(j_chunk <= t_chunk, key, KEY_NEG_INF)
        hi_ref[pl.ds(r0, tkr), :] = (key >> 16).astype(I16)
        lo_ref[pl.ds(r0, tkr), :] = ((key & 0xFFFF) - 2 ** 15).astype(I16)
        return carry

    lax.fori_loop(0, (n_adm + tkr - 1) // tkr, score_body, 0)

    n_ch = n_adm // (cr * unroll)
    one, zero = jnp.int16(1), jnp.int16(0)

    def count(pred):
        def body(c, acc):
            for uu in range(unroll):
                r0 = pl.multiple_of((c * unroll + uu) * cr, cr)
                acc = acc + jnp.where(pred(r0), one, zero)
            return acc
        acc = lax.fori_loop(0, n_ch, body, jnp.zeros((cr, tq), I16))
        return jnp.sum(acc.astype(I32), axis=0, keepdims=True)

    def hi_at(r0):
        return hi_ref[pl.ds(r0, cr), :]

    def lo_at(r0):
        return lo_ref[pl.ds(r0, cr), :]

    def bcast16(v):
        return jnp.broadcast_to(v, (cr, tq)).astype(I16)

    def hi_round(i, carry):
        thr, c_thr, c_gt = carry
        cand = thr + jnp.left_shift(jnp.int32(1), 15 - i)
        cand16 = bcast16(cand)
        cnt = count(lambda r0: hi_at(r0) >= cand16)
        ok = cnt >= n_keep
        return jnp.where(ok, cand, thr), jnp.where(ok, cnt, c_thr), jnp.where(ok, c_gt, cnt)

    thr_hi, c_ge_hi, c_above = lax.fori_loop(
        0, 16, hi_round,
        (jnp.full((1, tq), -2 ** 15, I32), jnp.full((1, tq), n_adm, I32),
         jnp.zeros((1, tq), I32)))
    thr_hi16 = bcast16(thr_hi)

    def bucket_body(c, carry):
        r0 = pl.multiple_of(c * cr, cr)
        lo_ref[pl.ds(r0, cr), :] = jnp.where(hi_at(r0) == thr_hi16, lo_at(r0),
                                             jnp.int16(-2 ** 15))
        return carry

    lax.fori_loop(0, n_adm // cr, bucket_body, 0)

    def lo_round(i, carry):
        thr, c_thr, c_gt = carry
        cand = thr + jnp.left_shift(jnp.int32(1), 15 - i)
        cand16 = bcast16(cand)
        cnt = c_above + count(lambda r0: lo_at(r0) >= cand16)
        ok = cnt >= n_keep
        return jnp.where(ok, cand, thr), jnp.where(ok, cnt, c_thr), jnp.where(ok, c_gt, cnt)

    thr_lo, c_thr, c_gt = lax.fori_loop(
        0, 16, lo_round, (jnp.full((1, tq), -2 ** 15, I32), c_ge_hi, c_above))
    thr_lo16 = bcast16(thr_lo)

    need_ties = jnp.max(c_thr) > n_keep
    n_tied_ok = n_keep - c_gt

    def tied(r0):
        return (lo_at(r0) == thr_lo16) & (hi_at(r0) == thr_hi16)

    def tie_round(i, cut):
        cand = cut + jnp.left_shift(jnp.int32(1), tie_bits - 1 - i)
        cand16 = bcast16(cand)
        cnt = count(lambda r0: tied(r0)
                    & ((r0 + lax.broadcasted_iota(I32, (cr, tq), 0)).astype(I16) < cand16))
        return jnp.where(cnt <= n_tied_ok, cand, cut)

    cut = lax.fori_loop(0, jnp.where(need_ties, tie_bits, 0), tie_round,
                        jnp.zeros((1, tq), I32))
    cut = jnp.where(need_ties, cut, jnp.int32(2 ** 15 - 1))
    cut16 = bcast16(cut)

    pk_ref[...] = jnp.zeros(pk_ref.shape, I32)
    n_sub = LANES // cr

    def pack_body(c, carry):
        pieces = []
        for uu in range(n_sub):
            r0 = pl.multiple_of(c * LANES + uu * cr, cr)
            j16 = (r0 + lax.broadcasted_iota(I32, (cr, tq), 0)).astype(I16)
            sel = ((hi_at(r0) > thr_hi16) | (lo_at(r0) > thr_lo16)
                   | (tied(r0) & (j16 < cut16)))
            adm16 = bcast16(((r0 // CHUNK) <= t_chunk).astype(I32))
            pieces.append(jnp.where(sel, adm16, zero).astype(I32))
        sel32 = jnp.concatenate(pieces, axis=0)
        w0 = pl.multiple_of((c // MASK_BITS) * LANES, LANES)
        pk_ref[pl.ds(w0, LANES), :] = pk_ref[pl.ds(w0, LANES), :] | jnp.left_shift(
            sel32, c % MASK_BITS)
        return carry

    lax.fori_loop(0, n_adm // LANES, pack_body, 0)
    bits_ref[0] = pk_ref[...].T


def _topk_mask(qcat, kcat, wi, *, tq, n_keep):
    bsz, seq, _ = qcat.shape
    assert seq < 2 ** 15, "key indices are compared as int16"
    n_words = -(-seq // (MASK_BITS * LANES))
    tie_bits = max(1, int(seq).bit_length())
    return pl.pallas_call(
        functools.partial(_topk_mask_kernel, tq=tq, tkr=min(2 * tq, seq), n_keep=n_keep,
                          cr=CHUNK, unroll=tq // CHUNK, tie_bits=tie_bits),
        grid=(bsz, seq // tq),
        in_specs=[pl.BlockSpec((1, tq, qcat.shape[2]), lambda b, i: (b, i, 0)),
                  pl.BlockSpec((1, seq, kcat.shape[2]), lambda b, i: (b, 0, 0)),
                  pl.BlockSpec((1, tq, LANES), lambda b, i: (b, i, 0))],
        out_specs=pl.BlockSpec((1, tq, n_words * LANES), lambda b, i: (b, i, 0)),
        out_shape=jax.ShapeDtypeStruct((bsz, seq, n_words * LANES), I32),
        scratch_shapes=[pltpu.VMEM((seq, tq), I16),
                        pltpu.VMEM((seq, tq), I16),
                        pltpu.VMEM((n_words * LANES, tq), I32)],
        compiler_params=_cparams(("arbitrary", "arbitrary")),
        name="topk_mask",
    )(qcat, kcat, wi)


def _alibi_q_lanes(t_f, is_lane0, is_lane1, slope2):
    a = -slope2 * t_f
    a_hi = a.astype(BF16).astype(F32)
    lane = lax.broadcasted_iota(I32, (1, LANES), 1)
    const = jnp.zeros((1, LANES), F32)
    for n, piece in enumerate(_bf16_pieces(slope2)):
        const = jnp.where((lane == 2 + n) | (lane == 5 + n), piece, const)
    return jnp.where(is_lane0, a_hi, jnp.where(is_lane1, a - a_hi, const)).astype(BF16)


def _attn_kernel(qi_ref, kt_ref, q_ref, k_ref, v_ref, bits_ref, o_ref, qx_ref, m_ref, acc_ref,
                 mb_ref, corr_ref, *, tq, tk):
    qi = qi_ref[pl.program_id(1)]
    kt = kt_ref[pl.program_id(1)]
    last = ((qi + 1) * tq - 1) // tk
    n_sub = tk // LANES
    lane = lax.broadcasted_iota(I32, (tq, LANES), 1)
    lo_half = lane < HEAD_DIM

    @pl.when(kt == 0)
    def _init():
        m_ref[...] = jnp.full(m_ref.shape, M_INIT, F32)
        acc_ref[...] = jnp.zeros(acc_ref.shape, F32)
        t_f = (qi * tq + lax.broadcasted_iota(I32, (tq, LANES), 0)).astype(F32)
        for h in range(N_HEADS):
            q2 = q_ref[0, :, (h // 2) * LANES:(h // 2 + 1) * LANES]
            zero = jnp.zeros_like(q2)
            rows = slice((h % 2) * tq, (h % 2 + 1) * tq)
            qx_ref[h // 2, rows, :LANES] = (jnp.where(lo_half, q2, zero) if h % 2 == 0
                                            else jnp.where(lo_half, zero, q2))
            qx_ref[h // 2, rows, LANES:] = _alibi_q_lanes(t_f, lane == 0, lane == 1,
                                                          _alibi_slope2(h))

    def step(diag):
        bits = bits_ref[0]
        sel = jnp.concatenate(
            [(bits >> ((kt * n_sub + i) % MASK_BITS)) & 1 for i in range(n_sub)], axis=1)
        mb_ref[...] = jnp.where(sel == 1, 0.0, MASK_BIAS)
        if diag:
            t = qi * tq + lax.broadcasted_iota(I32, (tq, 1), 0)
            j = kt * tk + lax.broadcasted_iota(I32, (1, tk), 1)
            corr_ref[...] = (2 * jnp.maximum(j - t, 0)).astype(F32)
        for h in range(N_HEADS):
            hp = h // 2
            if h % 2 == 0:
                s_pair = lax.dot_general(qx_ref[hp],
                                         k_ref[0, :, 2 * hp * LANES:(2 * hp + 2) * LANES],
                                         _NT, preferred_element_type=F32)
            s = s_pair[(h % 2) * tq:(h % 2 + 1) * tq]
            if diag:
                s = s + (mb_ref[...] - _alibi_slope2(h) * corr_ref[...])
            else:
                s = s + mb_ref[...]
            m_old = m_ref[h]
            m_new = jnp.maximum(m_old, jnp.max(s, axis=1, keepdims=True))
            p = jnp.exp2(s - jnp.tile(m_new, (1, n_sub)))
            m_ref[h] = m_new
            acc_ref[h] = (jnp.exp2(m_old - m_new) * acc_ref[h]
                          + jnp.dot(p.astype(BF16), v_ref[0, :, h * LANES:(h + 1) * LANES],
                                    preferred_element_type=F32))

    pl.when(kt < last)(functools.partial(step, False))
    pl.when(kt == last)(functools.partial(step, True))

    @pl.when(kt == last)
    def _finish():
        for hp in range(N_HEADS // 2):
            a0 = acc_ref[2 * hp]
            a1 = acc_ref[2 * hp + 1]
            o0 = a0 / pltpu.roll(a0, HEAD_DIM, axis=1)
            o1 = a1 / pltpu.roll(a1, HEAD_DIM, axis=1)
            o_ref[0, :, hp * LANES:(hp + 1) * LANES] = jnp.where(lo_half, o0, o1).astype(o_ref.dtype)


def _attention(q, k_ext, v_ext, bits, *, tq, tk):
    bsz, seq, d = q.shape
    n_sub = tk // LANES
    pairs = [(i, t) for i in range(seq // tq) for t in range(((i + 1) * tq - 1) // tk + 1)]
    qi_tab = jnp.asarray([p[0] for p in pairs], I32)
    kt_tab = jnp.asarray([p[1] for p in pairs], I32)
    q_map = lambda b, s, qi, kt: (b, qi[s], 0)
    kv_map = lambda b, s, qi, kt: (b, kt[s], 0)
    bits_map = lambda b, s, qi, kt: (b, qi[s], (kt[s] * n_sub) // MASK_BITS)
    grid_spec = pltpu.PrefetchScalarGridSpec(
        num_scalar_prefetch=2,
        grid=(bsz, len(pairs)),
        in_specs=[pl.BlockSpec((1, tq, d), q_map),
                  pl.BlockSpec((1, tk, 2 * d), kv_map),
                  pl.BlockSpec((1, tk, 2 * d), kv_map),
                  pl.BlockSpec((1, tq, LANES), bits_map)],
        out_specs=pl.BlockSpec((1, tq, d), q_map),
        scratch_shapes=[pltpu.VMEM((N_HEADS // 2, 2 * tq, 2 * LANES), BF16),
                        pltpu.VMEM((N_HEADS, tq, LANES), F32),
                        pltpu.VMEM((N_HEADS, tq, LANES), F32),
                        pltpu.VMEM((tq, tk), F32),
                        pltpu.VMEM((tq, tk), F32)])
    return pl.pallas_call(
        functools.partial(_attn_kernel, tq=tq, tk=tk),
        grid_spec=grid_spec,
        out_shape=jax.ShapeDtypeStruct((bsz, seq, d), BF16),
        compiler_params=_cparams(("arbitrary", "arbitrary")),
        name="masked_attention",
    )(qi_tab, kt_tab, q, k_ext, v_ext, bits)


def _sigmoid(z):
    return 0.5 * jnp.tanh(0.5 * z) + 0.5


def _proj_residual_kernel(y_ref, w_ref, x_ref, gate_ref, g_ref, *rest, glu, final, d):
    r = jnp.dot(y_ref[0], w_ref[...], preferred_element_type=F32)
    if glu:
        r = r[:, :d] * _sigmoid(r[:, d:])
    x_new = x_ref[0] + gate_ref[0] * r
    if final:
        (o_ref,) = rest
        o_ref[0] = _rms(x_new, g_ref[...])
    else:
        sh_ref, sc_ref, o_ref, h_ref = rest
        o_ref[0] = x_new
        h_ref[0] = _norm_mod(x_new, g_ref[...], sh_ref[0], sc_ref[0]).astype(h_ref.dtype)


def _proj_residual(y, w, x, gate, g, sh=None, sc=None, *, tm, glu, name):
    bsz, seq, d = x.shape
    kdim, n = w.shape
    final = sh is None
    row = lambda b, i: (b, i, 0)
    per_b = lambda b, i: (b, 0, 0)
    const = lambda b, i: (0, 0)
    in_specs = [pl.BlockSpec((1, tm, kdim), row),
                pl.BlockSpec((kdim, n), const, pipeline_mode=pl.Buffered(1)),
                pl.BlockSpec((1, tm, d), row),
                pl.BlockSpec((1, 1, d), per_b),
                pl.BlockSpec((1, d), const)]
    args = [y, w.astype(BF16), x, gate, g]
    out_specs = [pl.BlockSpec((1, tm, d), row)]
    out_shape = [jax.ShapeDtypeStruct((bsz, seq, d), F32)]
    if not final:
        in_specs += [pl.BlockSpec((1, 1, d), per_b), pl.BlockSpec((1, 1, d), per_b)]
        args += [sh, sc]
        out_specs.append(pl.BlockSpec((1, tm, d), row))
        out_shape.append(jax.ShapeDtypeStruct((bsz, seq, d), BF16))
    out = pl.pallas_call(
        functools.partial(_proj_residual_kernel, glu=glu, final=final, d=d),
        grid=(bsz, seq // tm),
        in_specs=in_specs,
        out_specs=out_specs,
        out_shape=out_shape,
        compiler_params=_cparams(("arbitrary", "arbitrary")),
        name=name,
    )(*args)
    return out[0] if final else out


def _ffn_up_kernel(h_ref, wv_ref, wg_ref, cwv_ref, cwg_ref,
                   cbv_ref, cbg_ref, a_ref, zv_ref, zg_ref, *, tm):
    i = pl.program_id(2)
    ng = tm // SUBLANES
    tn = a_ref.shape[2]

    @pl.when(i == 0)
    def _seq_start():
        zv_ref[...] = jnp.zeros(zv_ref.shape, F32)
        zg_ref[...] = jnp.zeros(zg_ref.shape, F32)

    h = h_ref[0]
    sub = lax.broadcasted_iota(I32, (ng, SUBLANES, tn), 1)

    def conv(w_ref, carry_ref, cw_ref, cb_ref):
        z = jnp.dot(h, w_ref[...], preferred_element_type=F32).reshape(ng, SUBLANES, tn)
        z_all = jnp.concatenate([carry_ref[...][None], z], axis=0)
        carry_ref[...] = z[ng - 1]
        out = cb_ref[...] + cw_ref[CONV_WIDTH - 1:CONV_WIDTH, :] * z
        for back in range(1, CONV_WIDTH):
            rot = pltpu.roll(z_all, back, axis=1)
            shifted = jnp.where(sub < back, rot[:ng], rot[1:])
            out = out + cw_ref[CONV_WIDTH - 1 - back:CONV_WIDTH - back, :] * shifted
        return out

    val = conv(wv_ref, zv_ref, cwv_ref, cbv_ref)
    gt = conv(wg_ref, zg_ref, cwg_ref, cbg_ref)
    hg = 0.5 * gt
    hv = hg * val
    a_ref[0] = (hv + hv * jnp.tanh(hg)).reshape(tm, tn).astype(a_ref.dtype)


def _ffn_up(h, w_up, conv_w, conv_b, *, tm, tn):
    bsz, seq, d = h.shape
    dff = w_up.shape[1] // 2
    nj = dff // tn
    wv = w_up[:, :dff].astype(BF16)
    wg = w_up[:, dff:].astype(BF16)
    cwv = jnp.pad(conv_w[:, :dff], ((0, SUBLANES - CONV_WIDTH), (0, 0)))
    cwg = jnp.pad(conv_w[:, dff:], ((0, SUBLANES - CONV_WIDTH), (0, 0)))
    cbv = conv_b[:dff].reshape(1, dff)
    cbg = conv_b[dff:].reshape(1, dff)
    col = lambda j, b, i: (0, j)
    return pl.pallas_call(
        functools.partial(_ffn_up_kernel, tm=tm),
        grid=(nj, bsz, seq // tm),
        in_specs=[pl.BlockSpec((1, tm, d), lambda j, b, i: (b, i, 0)),
                  pl.BlockSpec((d, tn), col),
                  pl.BlockSpec((d, tn), col),
                  pl.BlockSpec((SUBLANES, tn), col),
                  pl.BlockSpec((SUBLANES, tn), col),
                  pl.BlockSpec((1, tn), col),
                  pl.BlockSpec((1, tn), col)],
        out_specs=pl.BlockSpec((1, tm, tn), lambda j, b, i: (b, i, j)),
        out_shape=jax.ShapeDtypeStruct((bsz, seq, dff), BF16),
        scratch_shapes=[pltpu.VMEM((SUBLANES, tn), F32),
                        pltpu.VMEM((SUBLANES, tn), F32)],
        compiler_params=_cparams(("arbitrary", "arbitrary", "arbitrary")),
        name="ffn_up",
    )(h, wv, wg, cwv, cwg, cbv, cbg)


def _ssm_in_kernel(h_ref, w_ref, u_ref):
    u_ref[0] = jnp.dot(h_ref[0], w_ref[...], preferred_element_type=F32)


def _ssm_in(h, w, *, tm):
    bsz, seq, d = h.shape
    n = w.shape[1]
    row = lambda b, i: (b, i, 0)
    return pl.pallas_call(
        _ssm_in_kernel,
        grid=(bsz, seq // tm),
        in_specs=[pl.BlockSpec((1, tm, d), row),
                  pl.BlockSpec((d, n), lambda b, i: (0, 0))],
        out_specs=pl.BlockSpec((1, tm, n), row),
        out_shape=jax.ShapeDtypeStruct((bsz, seq, n), F32),
        compiler_params=_cparams(("arbitrary", "arbitrary")),
        name="ssm_in",
    )(h, w.astype(BF16))


def _ssm_scan_kernel(u_ref, perm_ref, permt_ref, bbt_ref, tab_ref, pt_ref, cc_ref, dskip_ref,
                     y_ref, carry_ref, *, tt, ns):
    i = pl.program_id(1)
    nt = tt // SUBLANES

    @pl.when(i == 0)
    def _seq_start():
        carry_ref[...] = jnp.zeros(carry_ref.shape, F32)

    def cmul_add(x_re, x_im, a_re, a_im, y_re, y_im):
        return x_re + a_re * y_re - a_im * y_im, x_im + a_re * y_im + a_im * y_re

    nb = u_ref.shape[0]
    u = u_ref[...]
    lanes = lambda rows: jnp.concatenate([rows[b] for b in range(nb)], axis=1)
    unlanes = lambda wide: jnp.concatenate(
        [wide[:, b * LANES:(b + 1) * LANES] for b in range(nb)], axis=0)
    up = jnp.dot(perm_ref[...], lanes(u).astype(BF16), preferred_element_type=F32).astype(BF16)
    bu = jnp.dot(unlanes(up), bbt_ref[0], preferred_element_type=F32)
    bu_re = bu[:, :ns].reshape(nb, nt, SUBLANES, ns)
    bu_im = bu[:, ns:].reshape(nb, nt, SUBLANES, ns)
    a_re, a_im = tab_ref[0, 0, 0], tab_ref[0, 0, 1]
    h_re = jnp.zeros((nb, SUBLANES, ns), F32)
    h_im = jnp.zeros((nb, SUBLANES, ns), F32)
    loc_re, loc_im = [], []
    for g in range(nt):
        h_re, h_im = cmul_add(bu_re[:, g], bu_im[:, g], a_re, a_im, h_re, h_im)
        loc_re.append(h_re)
        loc_im.append(h_im)
    s_re, s_im = h_re, h_im
    for kk in range(3):
        s_re, s_im = cmul_add(s_re, s_im, tab_ref[0, 1 + kk, 0], tab_ref[0, 1 + kk, 1],
                              pltpu.roll(s_re, 1 << kk, axis=1),
                              pltpu.roll(s_im, 1 << kk, axis=1))
    c_re, c_im = carry_ref[:, 0], carry_ref[:, 1]
    first = lax.broadcasted_iota(I32, (nb, SUBLANES, ns), 1) == 0
    hin_re, hin_im = cmul_add(jnp.where(first, 0.0, pltpu.roll(s_re, 1, axis=1)),
                              jnp.where(first, 0.0, pltpu.roll(s_im, 1, axis=1)),
                              tab_ref[0, 4, 0], tab_ref[0, 4, 1], c_re, c_im)
    c_re, c_im = cmul_add(jnp.broadcast_to(s_re[:, SUBLANES - 1:], (nb, SUBLANES, ns)),
                          jnp.broadcast_to(s_im[:, SUBLANES - 1:], (nb, SUBLANES, ns)),
                          tab_ref[0, 5, 0], tab_ref[0, 5, 1], c_re, c_im)
    carry_ref[:, 0] = c_re
    carry_ref[:, 1] = c_im
    hg_re, hg_im = cmul_add(jnp.stack(loc_re, axis=1), jnp.stack(loc_im, axis=1),
                            pt_ref[0, 0], pt_ref[0, 1], hin_re[:, None], hin_im[:, None])
    hcat = jnp.concatenate([hg_re.reshape(nb * tt, ns), hg_im.reshape(nb * tt, ns)],
                           axis=1).astype(BF16)
    yp = jnp.dot(hcat, cc_ref[0], preferred_element_type=F32)
    yp = lanes(yp.reshape(nb, tt, LANES))
    yp_hi = yp.astype(BF16)
    yp_lo = (yp - yp_hi.astype(F32)).astype(BF16)
    y = (jnp.dot(permt_ref[...], yp_hi, preferred_element_type=F32)
         + jnp.dot(permt_ref[...], yp_lo, preferred_element_type=F32))
    y = unlanes(y).reshape(nb, tt, LANES) + dskip_ref[0] * u
    y_ref[...] = jax.nn.gelu(y).astype(y_ref.dtype)


def _ssm_scan(u, lam_re, lam_im, log_dt, b_re, b_im, c_re, c_im, d_skip, *, tt):
    bsz, seq, width = u.shape
    n_groups, n_state = lam_re.shape
    gs = SSM_GROUP
    gpb = LANES // gs
    n_gb = n_groups // gpb
    ns = gpb * n_state
    dt = jnp.exp(log_dt)[:, None]
    mag = jnp.exp(lam_re * dt)
    a_re = mag * jnp.cos(lam_im * dt)
    a_im = mag * jnp.sin(lam_im * dt)
    den = lam_re * lam_re + lam_im * lam_im
    coef_re = ((a_re - 1.0) * lam_re + a_im * lam_im) / den
    coef_im = (a_im * lam_re - (a_re - 1.0) * lam_im) / den
    bb_re = coef_re[..., None] * b_re - coef_im[..., None] * b_im
    bb_im = coef_re[..., None] * b_im + coef_im[..., None] * b_re
    eye = jnp.eye(gpb, dtype=F32)

    def block_diag_in(bb):
        t = bb.reshape(n_gb, gpb, n_state, gs)
        return jnp.einsum('ngpc,gh->ngchp', t, eye).reshape(n_gb, gpb * gs, gpb * n_state)

    def block_diag_out(cm):
        t = cm.reshape(n_gb, gpb, gs, n_state)
        return jnp.einsum('ngcp,gh->ngphc', t, eye).reshape(n_gb, gpb * n_state, gpb * gs)

    bbt = jnp.concatenate([block_diag_in(bb_re), block_diag_in(bb_im)], axis=2).astype(BF16)
    cc = jnp.concatenate([block_diag_out(c_re), -block_diag_out(c_im)], axis=1).astype(BF16)

    def powers(e):
        ang = lam_im[None] * dt[None] * e[:, None, None]
        mg = jnp.exp(lam_re[None] * dt[None] * e[:, None, None])
        pw = jnp.stack([mg * jnp.cos(ang), mg * jnp.sin(ang)], axis=1)
        pw = pw.reshape(e.shape[0], 2, n_gb, ns)
        return jnp.transpose(pw, (2, 0, 1, 3))

    nt = tt // SUBLANES
    rows = jnp.arange(SUBLANES, dtype=F32)
    rep = lambda p: jnp.broadcast_to(p[:, :, :, None, :], p.shape[:3] + (SUBLANES, ns))
    seg_steps = rep(powers(nt * 2.0 ** jnp.arange(3, dtype=F32)))
    live = rows[None, :] >= (2.0 ** jnp.arange(3, dtype=F32))[:, None]
    seg_steps = seg_steps * live[None, :, None, :, None].astype(F32)
    seg_rows = jnp.transpose(powers(nt * rows), (0, 2, 1, 3))[:, None]
    tab = jnp.concatenate([rep(powers(jnp.ones((1,), F32))), seg_steps, seg_rows,
                           rep(powers(jnp.full((1,), nt * SUBLANES, F32)))], axis=1)
    pt = jnp.transpose(rep(powers(jnp.arange(1, nt + 1, dtype=F32))), (0, 2, 1, 3, 4))
    new_row = jnp.arange(tt)
    old_row = nt * (new_row % SUBLANES) + new_row // SUBLANES
    perm = (old_row[:, None] == jnp.arange(tt)[None, :]).astype(BF16)
    dsk = d_skip.reshape(n_gb, 1, gpb * gs)
    const2 = lambda g, i: (0, 0)
    return pl.pallas_call(
        functools.partial(_ssm_scan_kernel, tt=tt, ns=ns),
        grid=(n_gb, seq // tt),
        in_specs=[pl.BlockSpec((bsz, tt, LANES), lambda g, i: (0, i, g)),
                  pl.BlockSpec((tt, tt), const2),
                  pl.BlockSpec((tt, tt), const2),
                  pl.BlockSpec((1, LANES, 2 * ns), lambda g, i: (g, 0, 0)),
                  pl.BlockSpec((1, 6, 2, SUBLANES, ns), lambda g, i: (g, 0, 0, 0, 0)),
                  pl.BlockSpec((1, 2, nt, SUBLANES, ns), lambda g, i: (g, 0, 0, 0, 0)),
                  pl.BlockSpec((1, 2 * ns, LANES), lambda g, i: (g, 0, 0)),
                  pl.BlockSpec((1, 1, LANES), lambda g, i: (g, 0, 0))],
        out_specs=pl.BlockSpec((bsz, tt, LANES), lambda g, i: (0, i, g)),
        out_shape=jax.ShapeDtypeStruct((bsz, seq, width), BF16),
        scratch_shapes=[pltpu.VMEM((bsz, 2, SUBLANES, ns), F32)],
        compiler_params=_cparams(("arbitrary", "arbitrary")),
        name="ssm_scan",
    )(u, perm, perm.T, bbt, tab, pt, cc, dsk)


class _Tiles(NamedTuple):
    proj_rows: int
    topk_q: int
    attn_q: int
    attn_k: int
    rows: int
    ffn_cols: int
    scan_rows: int


def _tiles(seq, dff):
    return _Tiles(proj_rows=min(512, seq), topk_q=2 * LANES, attn_q=min(512, seq),
                  attn_k=min(512, seq), rows=min(1024, seq), ffn_cols=dff // 2,
                  scan_rows=min(512, seq))


def kernel(x, c, mod_w, mod_b, norm_mix_g, norm_ffn_g, attn_w_in, attn_w_out, ssm_w_in,
           ssm_lam_re, ssm_lam_im, ssm_log_dt, ssm_b_re, ssm_b_im, ssm_c_re, ssm_c_im,
           ssm_d, ssm_w_glu, ffn_w_up, ffn_conv_w, ffn_conv_b, ffn_w_down, final_g):
    bsz, seq, d = x.shape
    depth = mod_w.shape[0]
    n_keep = min(TOPK_MAX, seq // 4)
    t = _tiles(seq, ffn_w_down.shape[1])

    mod = _mod_vectors(c, mod_w, mod_b)
    mods = [[mod[i, :, None, k * d:(k + 1) * d] for k in range(6)] for i in range(depth)]
    h = None
    for i in range(depth):
        sh1, sc1, g1, sh2, sc2, g2 = mods[i]
        gm = norm_mix_g[i].reshape(1, d)
        gf = norm_ffn_g[i].reshape(1, d)
        j = i // 2
        if i % 2 == 0:
            q, k, v, qcat, kcat, wi = _attn_proj(x, gm, sh1, sc1, attn_w_in[j], tm=t.proj_rows)
            bits = _topk_mask(qcat, kcat, wi, tq=t.topk_q, n_keep=n_keep)
            att = _attention(q, k, v, bits, tq=t.attn_q, tk=t.attn_k)
            x, h = _proj_residual(att, attn_w_out[j], x, g1, gf, sh2, sc2, tm=t.rows, glu=False,
                                  name="attn_out")
        else:
            u = _ssm_in(h, ssm_w_in[j], tm=t.rows)
            y = _ssm_scan(u, ssm_lam_re[j], ssm_lam_im[j], ssm_log_dt[j], ssm_b_re[j],
                          ssm_b_im[j], ssm_c_re[j], ssm_c_im[j], ssm_d[j], tt=t.scan_rows)
            x, h = _proj_residual(y, ssm_w_glu[j], x, g1, gf, sh2, sc2, tm=t.rows, glu=True,
                                  name="ssm_glu")
        a = _ffn_up(h, ffn_w_up[i], ffn_conv_w[i], ffn_conv_b[i], tm=t.rows, tn=t.ffn_cols)
        if i == depth - 1:
            x = _proj_residual(a, ffn_w_down[i], x, g2, final_g.reshape(1, d), tm=t.rows,
                               glu=False, name="ffn_down_final")
        else:
            sh_n, sc_n = mods[i + 1][0], mods[i + 1][1]
            x, h = _proj_residual(a, ffn_w_down[i], x, g2, norm_mix_g[i + 1].reshape(1, d),
                                  sh_n, sc_n, tm=t.rows, glu=False, name="ffn_down")
    return x
```
